```python
import math
import jax, jax.numpy as jnp
from jax import lax
import numpy as np

D_MODEL = 1024
BATCH = 8
SEQ = 2048
DEPTH = 2

MIX_WIDTH = D_MODEL
SSM_WIDTH = MIX_WIDTH // 2
POOL_WIDTH = MIX_WIDTH - SSM_WIDTH
SSM_GROUP = 16
SSM_GROUPS = SSM_WIDTH // SSM_GROUP
SSM_STATE = 64
POOL_WINDOWS = (2, 4, 8, 16)
POOL_GROUPS = 4
POOL_GROUP_WIDTH = POOL_WIDTH // POOL_GROUPS
PEER_HEADS = 8
PEER_N_KEYS = 128
PEER_N_EXPERTS = PEER_N_KEYS * PEER_N_KEYS
PEER_KEY_DIM = 256
PEER_HALF = PEER_KEY_DIM // 2
PEER_TOPK = 16
PEER_CHUNK = 128
RMS_EPS = 1e-6
DT_MIN = 1e-3
DT_MAX = 1e-1

kernel_name = 'hybrid_s5_pool_peer'


def rmsnorm(x, g):
    xf = x.astype(jnp.float32)
    y = xf * lax.rsqrt(jnp.mean(xf * xf, axis=-1, keepdims=True) + RMS_EPS)
    return (y * g.astype(jnp.float32)).astype(x.dtype)


def _complex_affine_combine(left, right):
    a1r, a1i, b1r, b1i = left
    a2r, a2i, b2r, b2i = right
    ar = a1r * a2r - a1i * a2i
    ai = a1r * a2i + a1i * a2r
    br = a2r * b1r - a2i * b1i + b2r
    bi = a2r * b1i + a2i * b1r + b2i
    return ar, ai, br, bi


def s5_mixer(u, a_re, a_im, log_dt, b_re, b_im, c_re, c_im, d_skip, w_glu, b_glu):
    bsz, seq, _ = u.shape
    f32 = jnp.float32
    uf = u.astype(f32).reshape(bsz, seq, SSM_GROUPS, SSM_GROUP)
    lam_re = a_re.astype(f32)
    lam_im = a_im.astype(f32)
    dt = jnp.exp(log_dt.astype(f32))[:, None]
    decay = jnp.exp(lam_re * dt)
    abar_re = decay * jnp.cos(lam_im * dt)
    abar_im = decay * jnp.sin(lam_im * dt)
    inv_den = 1.0 / (lam_re * lam_re + lam_im * lam_im)
    num_re = abar_re - 1.0
    zoh_re = (num_re * lam_re + abar_im * lam_im) * inv_den
    zoh_im = (abar_im * lam_re - num_re * lam_im) * inv_den
    bu_re = jnp.einsum('blgh,gph->blgp', uf, b_re.astype(f32))
    bu_im = jnp.einsum('blgh,gph->blgp', uf, b_im.astype(f32))
    in_re = zoh_re * bu_re - zoh_im * bu_im
    in_im = zoh_re * bu_im + zoh_im * bu_re
    a_seq_re = jnp.broadcast_to(abar_re, in_re.shape)
    a_seq_im = jnp.broadcast_to(abar_im, in_re.shape)
    _, _, s_re, s_im = lax.associative_scan(
        _complex_affine_combine, (a_seq_re, a_seq_im, in_re, in_im), axis=1)
    y = (jnp.einsum('blgp,ghp->blgh', s_re, c_re.astype(f32))
         - jnp.einsum('blgp,ghp->blgh', s_im, c_im.astype(f32))
         + d_skip.astype(f32) * uf)
    y = jax.nn.gelu(y.reshape(bsz, seq, SSM_WIDTH), approximate=False)
    y = y * jax.nn.sigmoid(y @ w_glu.astype(f32) + b_glu.astype(f32))
    return y.astype(u.dtype)


def pool_mixer(u, w_pool, pool_scale):
    bsz, seq, _ = u.shape
    f32 = jnp.float32
    uf = u.astype(f32).reshape(bsz, seq, POOL_GROUPS, POOL_GROUP_WIDTH)
    cs = jnp.cumsum(uf, axis=1)
    pos = jnp.arange(1, seq + 1, dtype=f32)
    outs = []
    for gi, win in enumerate(POOL_WINDOWS):
        csg = cs[:, :, gi]
        lagged = jnp.pad(csg, ((0, 0), (win, 0), (0, 0)))[:, :seq]
        count = jnp.minimum(pos, float(win))[:, None]
        outs.append((csg - lagged) / count - uf[:, :, gi])
    pooled = jnp.stack(outs, axis=2)
    mixed = jnp.einsum('blgc,gcd->blgd', pooled, w_pool.astype(f32))
    y = mixed.reshape(bsz, seq, POOL_WIDTH) * pool_scale.astype(f32)
    return y.astype(u.dtype)


def peer_ffn(h, w_q, k1, k2, u_tab, v_tab):
    bsz, seq, dm = h.shape
    q = jnp.einsum('bld,dk->blk', h, w_q).reshape(bsz, seq, PEER_HEADS, 2, PEER_HALF)
    s1 = jnp.einsum('blhk,nk->blhn', q[..., 0, :], k1).astype(jnp.float32)
    s2 = jnp.einsum('blhk,nk->blhn', q[..., 1, :], k2).astype(jnp.float32)
    v1, i1 = lax.top_k(s1, PEER_TOPK)
    v2, i2 = lax.top_k(s2, PEER_TOPK)
    cand_score = (v1[..., :, None] + v2[..., None, :]).reshape(bsz, seq, PEER_HEADS, PEER_TOPK * PEER_TOPK)
    cand_idx = (i1[..., :, None] * PEER_N_KEYS + i2[..., None, :]).reshape(bsz, seq, PEER_HEADS, PEER_TOPK * PEER_TOPK)
    best, sel = lax.top_k(cand_score, PEER_TOPK)
    experts = jnp.take_along_axis(cand_idx, sel, axis=-1)
    gates = jax.nn.softmax(best, axis=-1).astype(h.dtype)
    n_chunks = (bsz * seq) // PEER_CHUNK
    hc = h.reshape(n_chunks, PEER_CHUNK, dm)
    ec = experts.reshape(n_chunks, PEER_CHUNK, PEER_HEADS, PEER_TOPK)
    gc = gates.reshape(n_chunks, PEER_CHUNK, PEER_HEADS, PEER_TOPK)

    def chunk(args):
        xc, ei, gi = args
        u_sel = jnp.take(u_tab, ei, axis=0)
        act = jax.nn.gelu(jnp.einsum('cd,chkd->chk', xc, u_sel), approximate=False) * gi
        v_sel = jnp.take(v_tab, ei, axis=0)
        return jnp.einsum('chk,chkd->cd', act, v_sel)

    out = lax.map(chunk, (hc, ec, gc))
    return out.reshape(bsz, seq, dm)


def setup_inputs(seed: int = 0) -> dict:
    key = jax.random.key(seed)
    ks = jax.random.split(key, 26)
    f32 = jnp.float32

    def nrm(k, shape, scale):
        return jax.random.normal(k, shape, f32) * scale

    def gain(k, shape):
        return 1.0 + 0.05 * jax.random.normal(k, shape, f32)

    nl, g, p, hh = DEPTH, SSM_GROUPS, SSM_STATE, SSM_GROUP
    x = jax.random.normal(ks[0], (BATCH, SEQ, D_MODEL), f32)
    norm_mix = gain(ks[1], (nl, D_MODEL))
    w_in = nrm(ks[2], (nl, D_MODEL, MIX_WIDTH), D_MODEL ** -0.5)
    a_re = -0.5 + 0.01 * jax.random.normal(ks[3], (nl, g, p), f32)
    a_im = math.pi * jnp.arange(p, dtype=f32) + 0.01 * jax.random.normal(ks[4], (nl, g, p), f32)
    log_dt = jax.random.uniform(ks[5], (nl, g), f32, math.log(DT_MIN), math.log(DT_MAX))
    b_re = nrm(ks[6], (nl, g, p, hh), (2 * hh) ** -0.5)
    b_im = nrm(ks[7], (nl, g, p, hh), (2 * hh) ** -0.5)
    c_re = nrm(ks[8], (nl, g, hh, p), (2 * p) ** -0.5)
    c_im = nrm(ks[9], (nl, g, hh, p), (2 * p) ** -0.5)
    d_skip = nrm(ks[10], (nl, g, hh), 1.0)
    w_glu = nrm(ks[11], (nl, SSM_WIDTH, SSM_WIDTH), SSM_WIDTH ** -0.5)
    b_glu = nrm(ks[12], (nl, SSM_WIDTH), 0.02)
    w_pool = nrm(ks[13], (nl, POOL_GROUPS, POOL_GROUP_WIDTH, POOL_GROUP_WIDTH), POOL_GROUP_WIDTH ** -0.5)
    pool_scale = gain(ks[14], (nl, POOL_WIDTH))
    g_out_ssm = gain(ks[15], (nl, SSM_WIDTH))
    g_out_pool = gain(ks[16], (nl, POOL_WIDTH))
    w_out = nrm(ks[17], (nl, MIX_WIDTH, D_MODEL), MIX_WIDTH ** -0.5)
    norm_ffn = gain(ks[18], (nl, D_MODEL))
    w_q = nrm(ks[19], (nl, D_MODEL, PEER_HEADS * PEER_KEY_DIM), D_MODEL ** -0.5)
    k1 = nrm(ks[20], (nl, PEER_N_KEYS, PEER_HALF), PEER_HALF ** -0.5)
    k2 = nrm(ks[21], (nl, PEER_N_KEYS, PEER_HALF), PEER_HALF ** -0.5)
    u_experts = nrm(ks[22], (nl, PEER_N_EXPERTS, D_MODEL), D_MODEL ** -0.5)
    v_experts = nrm(ks[23], (nl, PEER_N_EXPERTS, D_MODEL), PEER_HEADS ** -0.5)
    norm_final = gain(ks[24], (D_MODEL,))
    return {'x': x, 'norm_mix': norm_mix, 'w_in': w_in, 'a_re': a_re, 'a_im': a_im,
            'log_dt': log_dt, 'b_re': b_re, 'b_im': b_im, 'c_re': c_re, 'c_im': c_im,
            'd_skip': d_skip, 'w_glu': w_glu, 'b_glu': b_glu, 'w_pool': w_pool,
            'pool_scale': pool_scale, 'g_out_ssm': g_out_ssm, 'g_out_pool': g_out_pool,
            'w_out': w_out, 'norm_ffn': norm_ffn, 'w_q': w_q, 'k1': k1, 'k2': k2,
            'u_experts': u_experts, 'v_experts': v_experts, 'norm_final': norm_final}


def reference(x, norm_mix, w_in, a_re, a_im, log_dt, b_re, b_im, c_re, c_im, d_skip,
              w_glu, b_glu, w_pool, pool_scale, g_out_ssm, g_out_pool, w_out, norm_ffn,
              w_q, k1, k2, u_experts, v_experts, norm_final):
    for i in range(DEPTH):
        hn = rmsnorm(x, norm_mix[i])
        proj = jnp.einsum('bld,dm->blm', hn, w_in[i])
        u_ssm = proj[..., :SSM_WIDTH]
        u_pool = proj[..., SSM_WIDTH:]
        y_ssm = s5_mixer(u_ssm, a_re[i], a_im[i], log_dt[i], b_re[i], b_im[i],
                         c_re[i], c_im[i], d_skip[i], w_glu[i], b_glu[i])
        y_pool = pool_mixer(u_pool, w_pool[i], pool_scale[i])
        mixed = jnp.concatenate([rmsnorm(y_ssm, g_out_ssm[i]), rmsnorm(y_pool, g_out_pool[i])], axis=-1)
        x = x + jnp.einsum('blm,md->bld', mixed, w_out[i])
        x = x + peer_ffn(rmsnorm(x, norm_ffn[i]), w_q[i], k1[i], k2[i], u_experts[i], v_experts[i])
    return rmsnorm(x, norm_final)
```

```python
import functools
import math

import jax
import jax.numpy as jnp
from jax import lax
from jax.experimental import pallas as pl
from jax.experimental.pallas import tpu as pltpu

F32 = jnp.float32
BF16 = jnp.bfloat16

D_MODEL = 1024
SSM_WIDTH = 512
POOL_WIDTH = 512
SSM_GROUP = 16
SSM_GROUPS = 32
SSM_STATE = 64
STATE_W = SSM_GROUPS * SSM_STATE
POOL_WINDOWS = (2, 4, 8, 16)
POOL_GROUP_WIDTH = 128
PEER_HEADS = 8
PEER_N_KEYS = 128
PEER_N_EXPERTS = PEER_N_KEYS * PEER_N_KEYS
PEER_HALF = 128
PEER_TOPK = 16
RMS_EPS = 1e-6

SUBLANES = 8
MIX_ROWS = 512
POOL_HIST_ROWS = 128
ROUTE_TM = 256
PEER_TM = 512
PEER_TE = 1024
VMEM_LIMIT = 56 * 1024 * 1024

NEG_INF = float("-inf")
INV_SQRT2 = 0.7071067811865476


def _rms(x, g):
    return x * lax.rsqrt(jnp.mean(x * x, axis=-1, keepdims=True) + RMS_EPS) * g


def _gelu(x):
    return 0.5 * x * (1.0 + lax.erf(x * INV_SQRT2))


def _dot(a, b):
    return jnp.dot(a, b, preferred_element_type=F32)


def _disc_kernel(are_ref, aim_ref, ldt_ref, bre_ref, bim_ref,
                 abr_ref, abi_ref, btr_ref, bti_ref):
    lam_re = are_ref[...]
    lam_im = aim_ref[...]
    dt = jnp.exp(ldt_ref[...])
    decay = jnp.exp(lam_re * dt)
    abar_re = decay * jnp.cos(lam_im * dt)
    abar_im = decay * jnp.sin(lam_im * dt)
    inv_den = 1.0 / (lam_re * lam_re + lam_im * lam_im)
    num_re = abar_re - 1.0
    zoh_re = (num_re * lam_re + abar_im * lam_im) * inv_den
    zoh_im = (abar_im * lam_re - num_re * lam_im) * inv_den
    b_re = bre_ref[...]
    b_im = bim_ref[...]
    abr_ref[...] = abar_re
    abi_ref[...] = abar_im
    btr_ref[...] = zoh_re * b_re - zoh_im * b_im
    bti_ref[...] = zoh_re * b_im + zoh_im * b_re


def _discretise(a_re, a_im, log_dt, b_re, b_im):
    nl = a_re.shape[0]
    rows = nl * SSM_GROUPS * SSM_GROUP
    shp = (nl, SSM_GROUPS, SSM_GROUP, SSM_STATE)

    def rep(a):
        return jnp.broadcast_to(a[:, :, None, :], shp).reshape(rows, SSM_STATE)

    ldt = jnp.broadcast_to(log_dt[:, :, None, None], shp).reshape(rows, SSM_STATE)
    bre = jnp.transpose(b_re, (0, 1, 3, 2)).reshape(rows, SSM_STATE)
    bim = jnp.transpose(b_im, (0, 1, 3, 2)).reshape(rows, SSM_STATE)
    out = jax.ShapeDtypeStruct((rows, SSM_STATE), F32)
    abr, abi, btr, bti = pl.pallas_call(
        _disc_kernel, out_shape=(out, out, out, out), name="s5_discretise",
    )(rep(a_re), rep(a_im), ldt, bre, bim)
    abr = abr.reshape(shp)[:, :, 0, :].reshape(nl, STATE_W)
    abi = abi.reshape(shp)[:, :, 0, :].reshape(nl, STATE_W)
    return abr, abi, btr.reshape(shp), bti.reshape(shp)


def _mixer_kernel(x_ref, gmix_ref, win_ref, bmat_ref, are_ref, aim_ref,
                  cre_ref, cim_ref, dskip_ref, wglu_ref, bglu_ref, wpool_ref,
                  pscale_ref, gssm_ref, gpool_ref, wout_ref, o_ref,
                  st_ref, sre_ref, sim_ref, ext_ref):
    c = pl.program_id(0)
    rows = x_ref.shape[0]
    steps = rows // SUBLANES

    @pl.when(c == 0)
    def _():
        sre_ref[...] = jnp.zeros_like(sre_ref)
        sim_ref[...] = jnp.zeros_like(sim_ref)
        ext_ref[0:POOL_HIST_ROWS, :] = jnp.zeros((POOL_HIST_ROWS, POOL_WIDTH), F32)

    xr = x_ref[...]
    hn = _rms(xr, gmix_ref[...])
    proj = _dot(hn.astype(BF16), win_ref[...])
    u_ssm = proj[:, :SSM_WIDTH]
    u_pool = proj[:, SSM_WIDTH:]

    st_ref[...] = _dot(u_ssm.astype(BF16), bmat_ref[...])
    a_re = jnp.broadcast_to(are_ref[...], (SUBLANES, STATE_W))
    a_im = jnp.broadcast_to(aim_ref[...], (SUBLANES, STATE_W))

    def step(t, carry):
        s_re, s_im = carry
        r = pl.multiple_of(t * SUBLANES, SUBLANES)
        in_re = st_ref[pl.ds(r, SUBLANES), 0:STATE_W]
        in_im = st_ref[pl.ds(r, SUBLANES), STATE_W:2 * STATE_W]
        n_re = a_re * s_re - a_im * s_im + in_re
        n_im = a_re * s_im + a_im * s_re + in_im
        st_ref[pl.ds(r, SUBLANES), 0:STATE_W] = n_re
        st_ref[pl.ds(r, SUBLANES), STATE_W:2 * STATE_W] = n_im
        return n_re, n_im

    s_re, s_im = lax.fori_loop(0, steps, step, (sre_ref[...], sim_ref[...]))
    sre_ref[...] = s_re
    sim_ref[...] = s_im

    y = (_dot(st_ref[:, 0:STATE_W].astype(BF16), cre_ref[...])
         - _dot(st_ref[:, STATE_W:2 * STATE_W].astype(BF16), cim_ref[...])
         + dskip_ref[...] * u_ssm)
    y = _gelu(y)
    y = y * jax.nn.sigmoid(_dot(y.astype(BF16), wglu_ref[...]) + bglu_ref[...])
    ssm_n = _rms(y, gssm_ref[...])

    ext_ref[POOL_HIST_ROWS:, :] = u_pool
    t_idx = c * steps + jnp.right_shift(
        lax.broadcasted_iota(jnp.int32, (rows, POOL_GROUP_WIDTH), 0), 3)
    pooled = []
    for gi, win in enumerate(POOL_WINDOWS):
        lo = gi * POOL_GROUP_WIDTH
        hi = lo + POOL_GROUP_WIDTH
        acc = ext_ref[POOL_HIST_ROWS:, lo:hi]
        for k in range(1, win):
            off = POOL_HIST_ROWS - SUBLANES * k
            acc = acc + ext_ref[off:off + rows, lo:hi]
        count = jnp.minimum(t_idx + 1, win).astype(F32)
        pooled.append(acc / count - ext_ref[POOL_HIST_ROWS:, lo:hi])
    ext_ref[0:POOL_HIST_ROWS, :] = ext_ref[rows:rows + POOL_HIST_ROWS, :]
    pooled = jnp.concatenate(pooled, axis=1)
    y_pool = _dot(pooled.astype(BF16), wpool_ref[...]) * pscale_ref[...]
    pool_n = _rms(y_pool, gpool_ref[...])

    res = (_dot(ssm_n.astype(BF16), wout_ref[0:SSM_WIDTH, :])
           + _dot(pool_n.astype(BF16), wout_ref[SSM_WIDTH:, :]))
    o_ref[...] = xr + res


def _const_spec(shape):
    zeros = (0,) * len(shape)
    return pl.BlockSpec(shape, lambda *_: zeros, pipeline_mode=pl.Buffered(1))


def _mixer(x, gmix, win, bmat, are, aim, cre, cim, dskip, wglu, bglu, wpool,
           pscale, gssm, gpool, wout):
    n = x.shape[0]
    consts = (gmix, win, bmat, are, aim, cre, cim, dskip, wglu, bglu, wpool,
              pscale, gssm, gpool, wout)
    row_spec = pl.BlockSpec((MIX_ROWS, D_MODEL), lambda c: (c, 0))
    return pl.pallas_call(
        _mixer_kernel,
        grid=(n // MIX_ROWS,),
        in_specs=[row_spec] + [_const_spec(a.shape) for a in consts],
        out_specs=row_spec,
        out_shape=jax.ShapeDtypeStruct((n, D_MODEL), F32),
        scratch_shapes=[
            pltpu.VMEM((MIX_ROWS, 2 * STATE_W), F32),
            pltpu.VMEM((SUBLANES, STATE_W), F32),
            pltpu.VMEM((SUBLANES, STATE_W), F32),
            pltpu.VMEM((POOL_HIST_ROWS + MIX_ROWS, POOL_WIDTH), F32),
        ],
        compiler_params=pltpu.CompilerParams(
            dimension_semantics=("arbitrary",), vmem_limit_bytes=VMEM_LIMIT),
        name="mixer",
    )(x, *consts)


def _sort16_pairs():
    n, pairs, p = 16, [], 1
    while p < n:
        k = p
        while k >= 1:
            for j in range(k % p, n - k, 2 * k):
                for i in range(min(k, n - j - k)):
                    if (i + j) // (2 * p) == (i + j + k) // (2 * p):
                        pairs.append((i + j, i + j + k))
            k //= 2
        p *= 2
    return pairs


_SORT16 = _sort16_pairs()
N_TOP = PEER_TOPK + 1


def _top_sorted(s, out_ref):
    v = [s[SUBLANES * k:SUBLANES * (k + 1), :] for k in range(16)]
    for i, j in _SORT16:
        hi = jnp.maximum(v[i], v[j])
        lo = jnp.minimum(v[i], v[j])
        v[i], v[j] = hi, lo
    for i in range(N_TOP):
        head = v[0]
        m = jnp.max(head, axis=0, keepdims=True)
        out_ref[i:i + 1, :] = m
        if i + 1 < N_TOP:
            pop = head == m
            depth = N_TOP - i
            v = [jnp.where(pop, v[k + 1] if k + 1 < len(v) else NEG_INF, v[k])
                 for k in range(depth - 1)]


def _route_kernel(x_ref, gffn_ref, wqt_ref, k1_ref, k2_ref,
                  hb_ref, th_ref, e1_ref, s2_ref, w2_ref,
                  qt_ref, l1_ref, l2_ref):
    tm = x_ref.shape[0]
    h = _rms(x_ref[...], gffn_ref[...])
    hb = h.astype(BF16)
    hb_ref[...] = hb
    qt_ref[...] = lax.dot_general(wqt_ref[...], hb, (((1,), (1,)), ((), ())),
                                  preferred_element_type=F32)
    row = lax.broadcasted_iota(jnp.int32, (SUBLANES, tm), 0)

    def head(hd, carry):
        base = pl.multiple_of(hd * 2 * PEER_HALF, 2 * PEER_HALF)
        q1 = qt_ref[pl.ds(base, PEER_HALF), :].astype(BF16)
        q2 = qt_ref[pl.ds(base + PEER_HALF, PEER_HALF), :].astype(BF16)
        s1 = _dot(k1_ref[...], q1)
        s2 = _dot(k2_ref[...], q2)
        _top_sorted(s1, l1_ref)
        _top_sorted(s2, l2_ref)
        m1 = l1_ref[0:1, :]
        m2 = l2_ref[0:1, :]
        a = l1_ref[1:9, :]
        b = l2_ref[1:9, :]
        cands = [
            m1 + l2_ref[0:8, :],
            m1 + l2_ref[8:16, :],
            m2 + a,
            m2 + l1_ref[9:17, :],
            jnp.where(row < 7, l1_ref[1:2, :] + b, NEG_INF),
            jnp.where(row < 4, l1_ref[2:3, :] + b, NEG_INF),
            jnp.where(row < 3, l1_ref[3:4, :] + b, NEG_INF),
            jnp.where(row < 2, l1_ref[4:5, :] + b, NEG_INF),
            jnp.where((row >= 4) & (row < 7), l2_ref[1:2, :] + a, NEG_INF),
            jnp.where(row == 7, m1 + l2_ref[9:17, :], NEG_INF),
        ]
        tops = []
        for i in range(N_TOP):
            m = cands[0]
            for cnd in cands[1:]:
                m = jnp.maximum(m, cnd)
            m = jnp.max(m, axis=0, keepdims=True)
            tops.append(m)
            if i + 1 < N_TOP:
                cands = [jnp.where(cnd == m, NEG_INF, cnd) for cnd in cands]
        tau = 0.5 * (tops[PEER_TOPK - 1] + tops[PEER_TOPK])
        z = jnp.zeros_like(tau)
        for i in range(PEER_TOPK):
            z = z + jnp.exp(tops[i] - tops[0])
        th_ref[hd] = tau - s1
        e1_ref[hd] = jnp.exp(s1 - m1)
        s2_ref[hd] = s2
        w2_ref[hd] = jnp.exp(s2 - m2) * (0.5 / z)
        return carry

    lax.fori_loop(0, PEER_HEADS, head, 0)


def _route(x, gffn, wqt, k1, k2):
    n = x.shape[0]
    tm = ROUTE_TM
    key_spec = pl.BlockSpec((PEER_HEADS, PEER_N_KEYS, tm), lambda i: (0, 0, i))
    key_shape = jax.ShapeDtypeStruct((PEER_HEADS, PEER_N_KEYS, n), F32)
    return pl.pallas_call(
        _route_kernel,
        grid=(n // tm,),
        in_specs=[pl.BlockSpec((tm, D_MODEL), lambda i: (i, 0)),
                  _const_spec(gffn.shape), _const_spec(wqt.shape),
                  _const_spec(k1.shape), _const_spec(k2.shape)],
        out_specs=[pl.BlockSpec((tm, D_MODEL), lambda i: (i, 0)),
                   key_spec, key_spec, key_spec, key_spec],
        out_shape=[jax.ShapeDtypeStruct((n, D_MODEL), BF16),
                   key_shape, key_shape, key_shape, key_shape],
        scratch_shapes=[
            pltpu.VMEM((PEER_HEADS * 2 * PEER_HALF, tm), F32),
            pltpu.VMEM((24, tm), F32),
            pltpu.VMEM((24, tm), F32),
        ],
        compiler_params=pltpu.CompilerParams(
            dimension_semantics=("arbitrary",), vmem_limit_bytes=VMEM_LIMIT),
        name="peer_route",
    )(x, gffn, wqt, k1, k2)


def _peer_kernel(x_ref, hb_ref, th_ref, e1_ref, s2_ref, w2_ref, u_ref, vt_ref,
                 gfin_ref, o_ref, s_ref, a_ref, acc_ref, *, final_norm):
    j = pl.program_id(1)
    te = u_ref.shape[0]
    blocks = te // PEER_N_KEYS

    @pl.when(j == 0)
    def _():
        acc_ref[...] = jnp.zeros_like(acc_ref)

    s_ref[...] = lax.dot_general(u_ref[...], hb_ref[...], (((1,), (1,)), ((), ())),
                                 preferred_element_type=F32)

    def block(ib, carry):
        i1 = j * blocks + ib
        r = pl.multiple_of(ib * PEER_N_KEYS, PEER_N_KEYS)
        g = None
        for hd in range(PEER_HEADS):
            th = th_ref[hd, pl.ds(i1, 1), :]
            e1 = e1_ref[hd, pl.ds(i1, 1), :]
            term = jnp.where(s2_ref[hd] >= th, w2_ref[hd] * e1, 0.0)
            g = term if g is None else g + term
        s = s_ref[pl.ds(r, PEER_N_KEYS), :]
        act = s * (1.0 + lax.erf(s * INV_SQRT2)) * g
        a_ref[pl.ds(r, PEER_N_KEYS), :] = act.astype(BF16)
        return carry

    lax.fori_loop(0, blocks, block, 0)
    acc_ref[...] += _dot(vt_ref[...], a_ref[...])

    @pl.when(j == pl.num_programs(1) - 1)
    def _():
        out = x_ref[...] + acc_ref[...].T
        if final_norm:
            out = _rms(out, gfin_ref[...])
        o_ref[...] = out


def _peer(x, hb, th, e1, s2, w2, u_bf, vt_bf, gfin, final_norm):
    n = x.shape[0]
    tm, te = PEER_TM, PEER_TE
    key_spec = pl.BlockSpec((PEER_HEADS, PEER_N_KEYS, tm), lambda i, j: (0, 0, i))
    row_spec = pl.BlockSpec((tm, D_MODEL), lambda i, j: (i, 0))
    return pl.pallas_call(
        functools.partial(_peer_kernel, final_norm=final_norm),
        grid=(n // tm, PEER_N_EXPERTS // te),
        in_specs=[row_spec, row_spec, key_spec, key_spec, key_spec, key_spec,
                  pl.BlockSpec((te, D_MODEL), lambda i, j: (j, 0)),
                  pl.BlockSpec((D_MODEL, te), lambda i, j: (0, j)),
                  pl.BlockSpec(gfin.shape, lambda i, j: (0, 0))],
        out_specs=row_spec,
        out_shape=jax.ShapeDtypeStruct((n, D_MODEL), F32),
        scratch_shapes=[
            pltpu.VMEM((te, tm), F32),
            pltpu.VMEM((te, tm), BF16),
            pltpu.VMEM((D_MODEL, tm), F32),
        ],
        compiler_params=pltpu.CompilerParams(
            dimension_semantics=("arbitrary", "arbitrary"),
            vmem_limit_bytes=VMEM_LIMIT),
        name="peer_dense",
    )(x, hb, th, e1, s2, w2, u_bf, vt_bf, gfin)


def _block_diag(blocks):
    g, r, c = blocks.shape
    eye = jnp.eye(g, dtype=blocks.dtype)
    return (blocks[:, :, None, :] * eye[:, None, :, None]).reshape(g * r, g * c)


def kernel(x, norm_mix, w_in, a_re, a_im, log_dt, b_re, b_im, c_re, c_im, d_skip, w_glu, b_glu, w_pool, pool_scale, g_out_ssm, g_out_pool, w_out, norm_ffn, w_q, k1, k2, u_experts, v_experts, norm_final):
    bsz, seq, dm = x.shape
    assert (bsz, dm) == (SUBLANES, D_MODEL) and seq % (MIX_ROWS // SUBLANES) == 0
    depth = w_in.shape[0]
    n = bsz * seq
    row = lambda a: a.reshape(1, -1).astype(F32)

    abr, abi, btr, bti = _discretise(a_re, a_im, log_dt, b_re, b_im)
    xt = jnp.transpose(x, (1, 0, 2)).reshape(n, dm)
    gfin = row(norm_final)

    for i in range(depth):
        bmat = jnp.concatenate([_block_diag(btr[i]), _block_diag(bti[i])], axis=1).astype(BF16)
        cre = _block_diag(jnp.transpose(c_re[i], (0, 2, 1))).astype(BF16)
        cim = _block_diag(jnp.transpose(c_im[i], (0, 2, 1))).astype(BF16)
        xt = _mixer(
            xt, row(norm_mix[i]), w_in[i].astype(BF16), bmat,
            abr[i].reshape(1, STATE_W), abi[i].reshape(1, STATE_W), cre, cim,
            row(d_skip[i]), w_glu[i].astype(BF16), row(b_glu[i]),
            _block_diag(w_pool[i]).astype(BF16), row(pool_scale[i]),
            row(g_out_ssm[i]), row(g_out_pool[i]), w_out[i].astype(BF16))
        hb, th, e1, s2, w2 = _route(
            xt, row(norm_ffn[i]), jnp.transpose(w_q[i]).astype(BF16),
            k1[i].astype(BF16), k2[i].astype(BF16))
        xt = _peer(xt, hb, th, e1, s2, w2, u_experts[i].astype(BF16),
                   jnp.transpose(v_experts[i]).astype(BF16), gfin,
                   final_norm=(i == depth - 1))
    return jnp.transpose(xt.reshape(seq, bsz, dm), (1, 0, 2))
```

```python
import functools
import math

import jax
import jax.numpy as jnp
from jax import lax
from jax.experimental import pallas as pl
from jax.experimental.pallas import tpu as pltpu

F32 = jnp.float32
BF16 = jnp.bfloat16

D_MODEL = 1024
SSM_WIDTH = 512
POOL_WIDTH = 512
SSM_GROUP = 16
SSM_GROUPS = 32
SSM_STATE = 64
STATE_W = SSM_GROUPS * SSM_STATE
POOL_WINDOWS = (2, 4, 8, 16)
POOL_GROUP_WIDTH = 128
PEER_HEADS = 8
PEER_N_KEYS = 128
PEER_N_EXPERTS = PEER_N_KEYS * PEER_N_KEYS
PEER_HALF = 128
PEER_TOPK = 16
RMS_EPS = 1e-6

SUBLANES = 8
LANES = 128
MXU_DEPTH = 256
MIX_ROWS = 512
POOL_HIST_ROWS = 128
ROUTE_TM = 256
PEER_TM = 512
PEER_TE = 1024
PEER_PIECE = 512
VMEM_LIMIT = 56 * 1024 * 1024

NEG_INF = float("-inf")
INV_SQRT2 = 0.7071067811865476


def _rms(x, g):
    return x * lax.rsqrt(jnp.mean(x * x, axis=-1, keepdims=True) + RMS_EPS) * g


def _gelu(x):
    return 0.5 * x * (1.0 + lax.erf(x * INV_SQRT2))


def _dot(a, b):
    return jnp.dot(a, b, preferred_element_type=F32)


def _words(x):
    return pltpu.bitcast(x, jnp.uint32)


def _halves(w):
    return pltpu.bitcast(w, BF16)


def _pack_rows(a):
    *lead, r, c = a.shape
    pairs = jnp.swapaxes(a.astype(BF16).reshape(*lead, r // 2, 2, c), -1, -2)
    return lax.bitcast_convert_type(pairs, jnp.uint32)


def _disc_kernel(are_ref, aim_ref, ldt_ref, bre_ref, bim_ref,
                 abr_ref, abi_ref, btr_ref, bti_ref):
    lam_re = are_ref[...]
    lam_im = aim_ref[...]
    dt = jnp.exp(ldt_ref[...])
    decay = jnp.exp(lam_re * dt)
    abar_re = decay * jnp.cos(lam_im * dt)
    abar_im = decay * jnp.sin(lam_im * dt)
    inv_den = 1.0 / (lam_re * lam_re + lam_im * lam_im)
    num_re = abar_re - 1.0
    zoh_re = (num_re * lam_re + abar_im * lam_im) * inv_den
    zoh_im = (abar_im * lam_re - num_re * lam_im) * inv_den
    b_re = bre_ref[...]
    b_im = bim_ref[...]
    abr_ref[...] = abar_re
    abi_ref[...] = abar_im
    btr_ref[...] = zoh_re * b_re - zoh_im * b_im
    bti_ref[...] = zoh_re * b_im + zoh_im * b_re


def _discretise(a_re, a_im, log_dt, b_re, b_im):
    nl = a_re.shape[0]
    rows = nl * SSM_GROUPS * SSM_GROUP
    shp = (nl, SSM_GROUPS, SSM_GROUP, SSM_STATE)

    def rep(a):
        return jnp.broadcast_to(a[:, :, None, :], shp).reshape(rows, SSM_STATE)

    ldt = jnp.broadcast_to(log_dt[:, :, None, None], shp).reshape(rows, SSM_STATE)
    bre = jnp.transpose(b_re, (0, 1, 3, 2)).reshape(rows, SSM_STATE)
    bim = jnp.transpose(b_im, (0, 1, 3, 2)).reshape(rows, SSM_STATE)
    out = jax.ShapeDtypeStruct((rows, SSM_STATE), F32)
    abr, abi, btr, bti = pl.pallas_call(
        _disc_kernel, out_shape=(out, out, out, out), name="s5_discretise",
    )(rep(a_re), rep(a_im), ldt, bre, bim)
    abr = abr.reshape(shp)[:, :, 0, :].reshape(nl, STATE_W)
    abi = abi.reshape(shp)[:, :, 0, :].reshape(nl, STATE_W)
    return abr, abi, btr.reshape(shp), bti.reshape(shp)


def _mixer_kernel(x_ref, gmix_ref, win_ref, bmat_ref, are_ref, aim_ref,
                  cre_ref, cim_ref, dskip_ref, wglu_ref, bglu_ref, wpool_ref,
                  pscale_ref, gssm_ref, gpool_ref, wout_ref, o_ref,
                  st_ref, sre_ref, sim_ref, ext_ref):
    c = pl.program_id(0)
    rows = x_ref.shape[0]
    steps = rows // SUBLANES

    @pl.when(c == 0)
    def _():
        sre_ref[...] = jnp.zeros_like(sre_ref)
        sim_ref[...] = jnp.zeros_like(sim_ref)
        ext_ref[0:POOL_HIST_ROWS, :] = jnp.zeros((POOL_HIST_ROWS, POOL_WIDTH), F32)

    xr = x_ref[...]
    hn = _rms(xr, gmix_ref[...])
    proj = _dot(hn.astype(BF16), win_ref[...])
    u_ssm = proj[:, :SSM_WIDTH]
    u_pool = proj[:, SSM_WIDTH:]

    st_ref[...] = _dot(u_ssm.astype(BF16), bmat_ref[...])
    a_re = jnp.broadcast_to(are_ref[...], (SUBLANES, STATE_W))
    a_im = jnp.broadcast_to(aim_ref[...], (SUBLANES, STATE_W))

    def step(t, carry):
        s_re, s_im = carry
        r = pl.multiple_of(t * SUBLANES, SUBLANES)
        in_re = st_ref[pl.ds(r, SUBLANES), 0:STATE_W]
        in_im = st_ref[pl.ds(r, SUBLANES), STATE_W:2 * STATE_W]
        n_re = a_re * s_re - a_im * s_im + in_re
        n_im = a_re * s_im + a_im * s_re + in_im
        st_ref[pl.ds(r, SUBLANES), 0:STATE_W] = n_re
        st_ref[pl.ds(r, SUBLANES), STATE_W:2 * STATE_W] = n_im
        return n_re, n_im

    s_re, s_im = lax.fori_loop(0, steps, step, (sre_ref[...], sim_ref[...]))
    sre_ref[...] = s_re
    sim_ref[...] = s_im

    y = (_dot(st_ref[:, 0:STATE_W].astype(BF16), cre_ref[...])
         - _dot(st_ref[:, STATE_W:2 * STATE_W].astype(BF16), cim_ref[...])
         + dskip_ref[...] * u_ssm)
    y = _gelu(y)
    y = y * jax.nn.sigmoid(_dot(y.astype(BF16), wglu_ref[...]) + bglu_ref[...])
    ssm_n = _rms(y, gssm_ref[...])

    ext_ref[POOL_HIST_ROWS:, :] = u_pool
    t_idx = c * steps + jnp.right_shift(
        lax.broadcasted_iota(jnp.int32, (rows, POOL_GROUP_WIDTH), 0), 3)
    pooled = []
    for gi, win in enumerate(POOL_WINDOWS):
        lo = gi * POOL_GROUP_WIDTH
        hi = lo + POOL_GROUP_WIDTH
        acc = ext_ref[POOL_HIST_ROWS:, lo:hi]
        for k in range(1, win):
            off = POOL_HIST_ROWS - SUBLANES * k
            acc = acc + ext_ref[off:off + rows, lo:hi]
        count = jnp.minimum(t_idx + 1, win).astype(F32)
        pooled.append(acc / count - ext_ref[POOL_HIST_ROWS:, lo:hi])
    ext_ref[0:POOL_HIST_ROWS, :] = ext_ref[rows:rows + POOL_HIST_ROWS, :]
    pooled = jnp.concatenate(pooled, axis=1)
    y_pool = _dot(pooled.astype(BF16), wpool_ref[...]) * pscale_ref[...]
    pool_n = _rms(y_pool, gpool_ref[...])

    res = (_dot(ssm_n.astype(BF16), wout_ref[0:SSM_WIDTH, :])
           + _dot(pool_n.astype(BF16), wout_ref[SSM_WIDTH:, :]))
    o_ref[...] = xr + res


def _const_spec(shape):
    zeros = (0,) * len(shape)
    return pl.BlockSpec(shape, lambda *_: zeros, pipeline_mode=pl.Buffered(1))


def _mixer(x, gmix, win, bmat, are, aim, cre, cim, dskip, wglu, bglu, wpool,
           pscale, gssm, gpool, wout):
    n = x.shape[0]
    consts = (gmix, win, bmat, are, aim, cre, cim, dskip, wglu, bglu, wpool,
              pscale, gssm, gpool, wout)
    row_spec = pl.BlockSpec((MIX_ROWS, D_MODEL), lambda c: (c, 0))
    return pl.pallas_call(
        _mixer_kernel,
        grid=(n // MIX_ROWS,),
        in_specs=[row_spec] + [_const_spec(a.shape) for a in consts],
        out_specs=row_spec,
        out_shape=jax.ShapeDtypeStruct((n, D_MODEL), F32),
        scratch_shapes=[
            pltpu.VMEM((MIX_ROWS, 2 * STATE_W), F32),
            pltpu.VMEM((SUBLANES, STATE_W), F32),
            pltpu.VMEM((SUBLANES, STATE_W), F32),
            pltpu.VMEM((POOL_HIST_ROWS + MIX_ROWS, POOL_WIDTH), F32),
        ],
        compiler_params=pltpu.CompilerParams(
            dimension_semantics=("arbitrary",), vmem_limit_bytes=VMEM_LIMIT),
        name="mixer",
    )(x, *consts)


def _sort16_pairs():
    n, pairs, p = 16, [], 1
    while p < n:
        k = p
        while k >= 1:
            for j in range(k % p, n - k, 2 * k):
                for i in range(min(k, n - j - k)):
                    if (i + j) // (2 * p) == (i + j + k) // (2 * p):
                        pairs.append((i + j, i + j + k))
            k //= 2
        p *= 2
    return pairs


_SORT16 = _sort16_pairs()
N_TOP = PEER_TOPK + 1


def _top_sorted(s, out_ref):
    v = [s[SUBLANES * k:SUBLANES * (k + 1), :] for k in range(16)]
    for i, j in _SORT16:
        hi = jnp.maximum(v[i], v[j])
        lo = jnp.minimum(v[i], v[j])
        v[i], v[j] = hi, lo
    for i in range(N_TOP):
        head = v[0]
        m = jnp.max(head, axis=0, keepdims=True)
        out_ref[i:i + 1, :] = m
        if i + 1 < N_TOP:
            pop = head == m
            depth = N_TOP - i
            v = [jnp.where(pop, v[k + 1] if k + 1 < len(v) else NEG_INF, v[k])
                 for k in range(depth - 1)]


def _bf16_pair_words(x):
    bits = pltpu.bitcast(x.astype(BF16).astype(F32), jnp.uint32)
    return bits | (bits >> 16)


def _route_kernel(x_ref, gffn_ref, wqt_ref, k1_ref, k2_ref,
                  hb_ref, cnt_ref, e1_ref, rank_ref, w2_ref,
                  qt_ref, l1_ref, l2_ref):
    tm = x_ref.shape[0]
    h = _rms(x_ref[...], gffn_ref[...])
    hb = h.astype(BF16)
    hb_ref[...] = _words(hb)
    qt_ref[...] = lax.dot_general(wqt_ref[...], hb, (((1,), (1,)), ((), ())),
                                  preferred_element_type=F32)
    row = lax.broadcasted_iota(jnp.int32, (SUBLANES, tm), 0)

    def head(hd, carry):
        base = pl.multiple_of(hd * 2 * PEER_HALF, 2 * PEER_HALF)
        q1 = qt_ref[pl.ds(base, PEER_HALF), :].astype(BF16)
        q2 = qt_ref[pl.ds(base + PEER_HALF, PEER_HALF), :].astype(BF16)
        s1 = _dot(k1_ref[...], q1)
        s2 = _dot(k2_ref[...], q2)
        _top_sorted(s1, l1_ref)
        _top_sorted(s2, l2_ref)
        m1 = l1_ref[0:1, :]
        m2 = l2_ref[0:1, :]
        a = l1_ref[1:9, :]
        b = l2_ref[1:9, :]
        cands = [
            m1 + l2_ref[0:8, :],
            m1 + l2_ref[8:16, :],
            m2 + a,
            m2 + l1_ref[9:17, :],
            jnp.where(row < 7, l1_ref[1:2, :] + b, NEG_INF),
            jnp.where(row < 4, l1_ref[2:3, :] + b, NEG_INF),
            jnp.where(row < 3, l1_ref[3:4, :] + b, NEG_INF),
            jnp.where(row < 2, l1_ref[4:5, :] + b, NEG_INF),
            jnp.where((row >= 4) & (row < 7), l2_ref[1:2, :] + a, NEG_INF),
            jnp.where(row == 7, m1 + l2_ref[9:17, :], NEG_INF),
        ]
        tops = []
        for i in range(N_TOP):
            m = cands[0]
            for cnd in cands[1:]:
                m = jnp.maximum(m, cnd)
            m = jnp.max(m, axis=0, keepdims=True)
            tops.append(m)
            if i + 1 < N_TOP:
                cands = [jnp.where(cnd == m, NEG_INF, cnd) for cnd in cands]
        tau = 0.5 * (tops[PEER_TOPK - 1] + tops[PEER_TOPK])
        z = jnp.zeros_like(tau)
        for i in range(PEER_TOPK):
            z = z + jnp.exp(tops[i] - tops[0])
        theta = tau - s1
        cnt = jnp.zeros_like(s1)
        rank = jnp.zeros_like(s2)
        for j in range(PEER_TOPK):
            v2j = l2_ref[j:j + 1, :]
            cnt = cnt + jnp.where(v2j >= theta, 1.0, 0.0)
            rank = rank + jnp.where(v2j > s2, 1.0, 0.0)
        e1 = jnp.exp(s1 - m1)
        w2 = jnp.exp(s2 - m2) * (0.5 / z)
        outs = ((cnt_ref, _bf16_pair_words(cnt)), (e1_ref, _bf16_pair_words(e1)),
                (rank_ref, _words(rank.astype(BF16))), (w2_ref, _words(w2.astype(BF16))))
        for ref, val in outs:
            for lt in range(tm // LANES):
                ref[hd, lt] = val[:, lt * LANES:(lt + 1) * LANES]
        return carry

    lax.fori_loop(0, PEER_HEADS, head, 0)


def _route(x, gffn, wqt, k1, k2):
    n = x.shape[0]
    tm = ROUTE_TM
    def key_spec(rows):
        return pl.BlockSpec((PEER_HEADS, tm // LANES, rows, LANES), lambda i: (0, i, 0, 0))

    def key_shape(rows):
        return jax.ShapeDtypeStruct((PEER_HEADS, n // LANES, rows, LANES), jnp.uint32)

    return pl.pallas_call(
        _route_kernel,
        grid=(n // tm,),
        in_specs=[pl.BlockSpec((tm, D_MODEL), lambda i: (i, 0)),
                  _const_spec(gffn.shape), _const_spec(wqt.shape),
                  _const_spec(k1.shape), _const_spec(k2.shape)],
        out_specs=[pl.BlockSpec((tm // 2, D_MODEL), lambda i: (i, 0)),
                   key_spec(PEER_N_KEYS), key_spec(PEER_N_KEYS),
                   key_spec(PEER_N_KEYS // 2), key_spec(PEER_N_KEYS // 2)],
        out_shape=[jax.ShapeDtypeStruct((n // 2, D_MODEL), jnp.uint32),
                   key_shape(PEER_N_KEYS), key_shape(PEER_N_KEYS),
                   key_shape(PEER_N_KEYS // 2), key_shape(PEER_N_KEYS // 2)],
        scratch_shapes=[
            pltpu.VMEM((PEER_HEADS * 2 * PEER_HALF, tm), F32),
            pltpu.VMEM((24, tm), F32),
            pltpu.VMEM((24, tm), F32),
        ],
        compiler_params=pltpu.CompilerParams(
            dimension_semantics=("arbitrary",), vmem_limit_bytes=VMEM_LIMIT),
        name="peer_route",
    )(x, gffn, wqt, k1, k2)


def _packed_row(words):
    tile = _halves(jnp.broadcast_to(words, (SUBLANES, LANES)))
    return jnp.tile(tile, (PEER_N_KEYS // tile.shape[0], 1))


def _peer_kernel(x_ref, hb_ref, cnt_ref, e1_ref, rank_ref, w2_ref, u_ref, vt_ref,
                 gfin_ref, o_ref, s_ref, a_ref, acc_ref, *, final_norm, e_tiles, n_tiles):
    g = pl.program_id(0)
    te = 2 * u_ref.shape[0]
    tm = 2 * hb_ref.shape[0]
    blocks = te // PEER_N_KEYS
    half_piece = PEER_PIECE // 2
    half_keys = PEER_N_KEYS // 2
    cur = g % 2
    prev = 1 - cur

    @pl.when(g == 0)
    def _():
        s_ref[...] = jnp.zeros_like(s_ref)
        a_ref[...] = jnp.zeros_like(a_ref)
        acc_ref[...] = jnp.zeros_like(acc_ref)

    tile2 = jnp.clip(g - 1, 0, n_tiles - 1)
    tile3 = jnp.clip(g - 2, 0, n_tiles - 1)
    i1_base = (tile2 % e_tiles) * blocks

    def piece(r, carry):
        rows = pl.ds(pl.multiple_of(r * PEER_PIECE, PEER_PIECE), PEER_PIECE)
        wrows = pl.ds(pl.multiple_of(r * half_piece, half_piece), half_piece)
        acc_ref[...] += _dot(_halves(vt_ref[r]), _halves(a_ref[cur, wrows, :]))
        for sub in range(PEER_PIECE // PEER_N_KEYS):
            ib = r * (PEER_PIECE // PEER_N_KEYS) + sub
            i1 = i1_base + ib
            brow = pl.ds(pl.multiple_of(ib * PEER_N_KEYS, PEER_N_KEYS), PEER_N_KEYS)
            bwrow = pl.ds(pl.multiple_of(ib * half_keys, half_keys), half_keys)
            for lt in range(tm // LANES):
                cols = slice(lt * LANES, (lt + 1) * LANES)
                gate = None
                for hd in range(PEER_HEADS):
                    cnt = _packed_row(cnt_ref[hd, lt, pl.ds(i1, 1), :])
                    e1 = _packed_row(e1_ref[hd, lt, pl.ds(i1, 1), :])
                    term = jnp.where(_halves(rank_ref[hd, lt]) < cnt,
                                     _halves(w2_ref[hd, lt]) * e1, jnp.zeros((), BF16))
                    gate = term if gate is None else gate + term
                s = s_ref[prev, brow, cols]
                act = s * (1.0 + lax.erf(s * INV_SQRT2))
                a_ref[prev, bwrow, cols] = _words(act.astype(BF16) * gate)
        s_ref[cur, rows, :] = lax.dot_general(
            _halves(u_ref[wrows, :]), _halves(hb_ref[...]), (((1,), (1,)), ((), ())),
            preferred_element_type=F32)
        return carry

    lax.fori_loop(0, te // PEER_PIECE, piece, 0)

    @pl.when((g >= 2) & (tile3 % e_tiles == e_tiles - 1))
    def _():
        out = x_ref[...] + acc_ref[...].T
        if final_norm:
            out = _rms(out, gfin_ref[...])
        o_ref[...] = out
        acc_ref[...] = jnp.zeros_like(acc_ref)


def _peer(x, hb_w, cnt, e1, rank_w, w2_w, u_w, vt_w, gfin, final_norm):
    n = x.shape[0]
    tm, te = PEER_TM, PEER_TE
    e_tiles = PEER_N_EXPERTS // te
    n_tiles = (n // tm) * e_tiles
    last = n_tiles - 1

    def t1(g):
        return jnp.minimum(g, last)

    def t2(g):
        return jnp.clip(g - 1, 0, last)

    def t3(g):
        return jnp.clip(g - 2, 0, last)

    def key_spec(rows):
        return pl.BlockSpec((PEER_HEADS, tm // LANES, rows, LANES),
                            lambda g: (0, t2(g) // e_tiles, 0, 0))

    out_row_spec = pl.BlockSpec((tm, D_MODEL), lambda g: (t3(g) // e_tiles, 0))
    return pl.pallas_call(
        functools.partial(_peer_kernel, final_norm=final_norm, e_tiles=e_tiles,
                          n_tiles=n_tiles),
        grid=(n_tiles + 2,),
        in_specs=[out_row_spec,
                  pl.BlockSpec((tm // 2, D_MODEL), lambda g: (t1(g) // e_tiles, 0)),
                  key_spec(PEER_N_KEYS), key_spec(PEER_N_KEYS),
                  key_spec(PEER_N_KEYS // 2), key_spec(PEER_N_KEYS // 2),
                  pl.BlockSpec((te // 2, D_MODEL), lambda g: (t1(g) % e_tiles, 0)),
                  pl.BlockSpec((te // PEER_PIECE, D_MODEL // 2, PEER_PIECE),
                               lambda g: (t3(g) % e_tiles, 0, 0)),
                  pl.BlockSpec(gfin.shape, lambda g: (0, 0))],
        out_specs=out_row_spec,
        out_shape=jax.ShapeDtypeStruct((n, D_MODEL), F32),
        scratch_shapes=[
            pltpu.VMEM((2, te, tm), F32),
            pltpu.VMEM((2, te // 2, tm), jnp.uint32),
            pltpu.VMEM((D_MODEL, tm), F32),
        ],
        compiler_params=pltpu.CompilerParams(
            dimension_semantics=("arbitrary",), vmem_limit_bytes=VMEM_LIMIT),
        name="peer_dense",
    )(x, hb_w, cnt, e1, rank_w, w2_w, u_w, vt_w, gfin)


def _block_diag(blocks):
    g, r, c = blocks.shape
    eye = jnp.eye(g, dtype=blocks.dtype)
    return (blocks[:, :, None, :] * eye[:, None, :, None]).reshape(g * r, g * c)


def _value_slices(v):
    e, d = v.shape
    return _pack_rows(jnp.transpose(v.reshape(e // PEER_PIECE, PEER_PIECE, d), (0, 2, 1)))


def kernel(x, norm_mix, w_in, a_re, a_im, log_dt, b_re, b_im, c_re, c_im, d_skip, w_glu, b_glu, w_pool, pool_scale, g_out_ssm, g_out_pool, w_out, norm_ffn, w_q, k1, k2, u_experts, v_experts, norm_final):
    bsz, seq, dm = x.shape
    assert (bsz, dm) == (SUBLANES, D_MODEL) and seq % (MIX_ROWS // SUBLANES) == 0
    depth = w_in.shape[0]
    n = bsz * seq
    row = lambda a: a.reshape(1, -1).astype(F32)

    abr, abi, btr, bti = _discretise(a_re, a_im, log_dt, b_re, b_im)
    xt = jnp.transpose(x, (1, 0, 2)).reshape(n, dm)
    gfin = row(norm_final)

    for i in range(depth):
        bmat = jnp.concatenate([_block_diag(btr[i]), _block_diag(bti[i])], axis=1).astype(BF16)
        cre = _block_diag(jnp.transpose(c_re[i], (0, 2, 1))).astype(BF16)
        cim = _block_diag(jnp.transpose(c_im[i], (0, 2, 1))).astype(BF16)
        xt = _mixer(
            xt, row(norm_mix[i]), w_in[i].astype(BF16), bmat,
            abr[i].reshape(1, STATE_W), abi[i].reshape(1, STATE_W), cre, cim,
            row(d_skip[i]), w_glu[i].astype(BF16), row(b_glu[i]),
            _block_diag(w_pool[i]).astype(BF16), row(pool_scale[i]),
            row(g_out_ssm[i]), row(g_out_pool[i]), w_out[i].astype(BF16))
        hb, cnt, e1, rank, w2 = _route(
            xt, row(norm_ffn[i]), jnp.transpose(w_q[i]).astype(BF16),
            k1[i].astype(BF16), k2[i].astype(BF16))
        xt = _peer(xt, hb, cnt, e1, rank, w2, _pack_rows(u_experts[i]),
                   _value_slices(v_experts[i]), gfin,
                   final_norm=(i == depth - 1))
    return jnp.transpose(xt.reshape(seq, bsz, dm), (1, 0, 2))
```

```python
import functools
import math

import jax
import jax.numpy as jnp
from jax import lax
from jax.experimental import pallas as pl
from jax.experimental.pallas import tpu as pltpu

F32 = jnp.float32
BF16 = jnp.bfloat16

D_MODEL = 1024
SSM_WIDTH = 512
POOL_WIDTH = 512
SSM_GROUP = 16
SSM_GROUPS = 32
SSM_STATE = 64
STATE_W = SSM_GROUPS * SSM_STATE
HALF_SSM = SSM_WIDTH // 2
HALF_STATE = STATE_W // 2
POOL_WINDOWS = (2, 4, 8, 16)
POOL_GROUP_WIDTH = 128
PEER_HEADS = 8
PEER_N_KEYS = 128
PEER_N_EXPERTS = PEER_N_KEYS * PEER_N_KEYS
PEER_HALF = 128
PEER_TOPK = 16
RMS_EPS = 1e-6

SUBLANES = 8
LANES = 128
MXU_DEPTH = 256
MIX_ROWS = 512
POOL_HIST_ROWS = 128
ROUTE_TM = 512
PEER_TM = 1024
PEER_TE = 1024
PEER_PIECE = 512
VMEM_LIMIT = 60 * 1024 * 1024

NEG_INF = float("-inf")
INV_SQRT2 = 0.7071067811865476


def _rms(x, g):
    return x * lax.rsqrt(jnp.mean(x * x, axis=-1, keepdims=True) + RMS_EPS) * g


def _gelu(x):
    return 0.5 * x * (1.0 + lax.erf(x * INV_SQRT2))


def _dot(a, b):
    return jnp.dot(a, b, preferred_element_type=F32)


def _words(x):
    return pltpu.bitcast(x, jnp.uint32)


def _halves(w):
    return pltpu.bitcast(w, BF16)


def _disc_kernel(are_ref, aim_ref, ldt_ref, bre_ref, bim_ref,
                 abr_ref, abi_ref, btr_ref, bti_ref):
    lam_re = are_ref[...]
    lam_im = aim_ref[...]
    dt = jnp.exp(ldt_ref[...])
    decay = jnp.exp(lam_re * dt)
    abar_re = decay * jnp.cos(lam_im * dt)
    abar_im = decay * jnp.sin(lam_im * dt)
    inv_den = 1.0 / (lam_re * lam_re + lam_im * lam_im)
    num_re = abar_re - 1.0
    zoh_re = (num_re * lam_re + abar_im * lam_im) * inv_den
    zoh_im = (abar_im * lam_re - num_re * lam_im) * inv_den
    b_re = bre_ref[...]
    b_im = bim_ref[...]
    abr_ref[...] = abar_re
    abi_ref[...] = abar_im
    btr_ref[...] = zoh_re * b_re - zoh_im * b_im
    bti_ref[...] = zoh_re * b_im + zoh_im * b_re


def _discretise(a_re, a_im, log_dt, b_re, b_im):
    nl = a_re.shape[0]
    rows = nl * SSM_GROUPS * SSM_GROUP
    shp = (nl, SSM_GROUPS, SSM_GROUP, SSM_STATE)

    def rep(a):
        return jnp.broadcast_to(a[:, :, None, :], shp).reshape(rows, SSM_STATE)

    ldt = jnp.broadcast_to(log_dt[:, :, None, None], shp).reshape(rows, SSM_STATE)
    bre = jnp.transpose(b_re, (0, 1, 3, 2)).reshape(rows, SSM_STATE)
    bim = jnp.transpose(b_im, (0, 1, 3, 2)).reshape(rows, SSM_STATE)
    out = jax.ShapeDtypeStruct((rows, SSM_STATE), F32)
    abr, abi, btr, bti = pl.pallas_call(
        _disc_kernel, out_shape=(out, out, out, out), name="s5_discretise",
    )(rep(a_re), rep(a_im), ldt, bre, bim)
    abr = abr.reshape(shp)[:, :, 0, :].reshape(nl, STATE_W)
    abi = abi.reshape(shp)[:, :, 0, :].reshape(nl, STATE_W)
    return abr, abi, btr.reshape(shp), bti.reshape(shp)


def _mixer_kernel(x_ref, gmix_ref, win_ref, bmat_ref, are_ref, aim_ref,
                  cre_ref, cim_ref, dskip_ref, wglu_ref, bglu_ref, wpool_ref,
                  pscale_ref, gssm_ref, gpool_ref, wout_ref, o_ref,
                  st_ref, sre_ref, sim_ref, ext_ref):
    c = pl.program_id(0)
    rows = x_ref.shape[0]
    steps = rows // SUBLANES

    @pl.when(c == 0)
    def _():
        sre_ref[...] = jnp.zeros_like(sre_ref)
        sim_ref[...] = jnp.zeros_like(sim_ref)
        ext_ref[0:POOL_HIST_ROWS, :] = jnp.zeros((POOL_HIST_ROWS, POOL_WIDTH), F32)

    xr = x_ref[...]
    hn = _rms(xr, gmix_ref[...])
    proj = _dot(hn.astype(BF16), win_ref[...])
    u_ssm = proj[:, :SSM_WIDTH]
    u_pool = proj[:, SSM_WIDTH:]

    u_b = u_ssm.astype(BF16)
    for part in range(2):
        for hf in range(2):
            lo = part * STATE_W + hf * HALF_STATE
            st_ref[:, lo:lo + HALF_STATE] = _dot(
                u_b[:, hf * HALF_SSM:(hf + 1) * HALF_SSM], bmat_ref[2 * part + hf])
    a_re = jnp.broadcast_to(are_ref[...], (SUBLANES, STATE_W))
    a_im = jnp.broadcast_to(aim_ref[...], (SUBLANES, STATE_W))

    def step(t, carry):
        s_re, s_im = carry
        r = pl.multiple_of(t * SUBLANES, SUBLANES)
        in_re = st_ref[pl.ds(r, SUBLANES), 0:STATE_W]
        in_im = st_ref[pl.ds(r, SUBLANES), STATE_W:2 * STATE_W]
        n_re = a_re * s_re - a_im * s_im + in_re
        n_im = a_re * s_im + a_im * s_re + in_im
        st_ref[pl.ds(r, SUBLANES), 0:STATE_W] = n_re
        st_ref[pl.ds(r, SUBLANES), STATE_W:2 * STATE_W] = n_im
        return n_re, n_im

    s_re, s_im = lax.fori_loop(0, steps, step, (sre_ref[...], sim_ref[...]))
    sre_ref[...] = s_re
    sim_ref[...] = s_im

    y = []
    for hf in range(2):
        lo = hf * HALF_STATE
        y.append(_dot(st_ref[:, lo:lo + HALF_STATE].astype(BF16), cre_ref[hf])
                 - _dot(st_ref[:, STATE_W + lo:STATE_W + lo + HALF_STATE].astype(BF16),
                        cim_ref[hf]))
    y = jnp.concatenate(y, axis=1) + dskip_ref[...] * u_ssm
    y = _gelu(y)
    y = y * jax.nn.sigmoid(_dot(y.astype(BF16), wglu_ref[...]) + bglu_ref[...])
    ssm_n = _rms(y, gssm_ref[...])

    ext_ref[POOL_HIST_ROWS:, :] = u_pool
    t_idx = c * steps + jnp.right_shift(
        lax.broadcasted_iota(jnp.int32, (rows, POOL_GROUP_WIDTH), 0), 3)
    pooled = []
    for gi, win in enumerate(POOL_WINDOWS):
        lo = gi * POOL_GROUP_WIDTH
        hi = lo + POOL_GROUP_WIDTH
        acc = ext_ref[POOL_HIST_ROWS:, lo:hi]
        for k in range(1, win):
            off = POOL_HIST_ROWS - SUBLANES * k
            acc = acc + ext_ref[off:off + rows, lo:hi]
        count = jnp.minimum(t_idx + 1, win).astype(F32)
        pooled.append(acc / count - ext_ref[POOL_HIST_ROWS:, lo:hi])
    ext_ref[0:POOL_HIST_ROWS, :] = ext_ref[rows:rows + POOL_HIST_ROWS, :]
    pooled = jnp.concatenate(pooled, axis=1)
    y_pool = _dot(pooled.astype(BF16), wpool_ref[...]) * pscale_ref[...]
    pool_n = _rms(y_pool, gpool_ref[...])

    res = (_dot(ssm_n.astype(BF16), wout_ref[0:SSM_WIDTH, :])
           + _dot(pool_n.astype(BF16), wout_ref[SSM_WIDTH:, :]))
    o_ref[...] = xr + res


def _const_spec(shape):
    zeros = (0,) * len(shape)
    return pl.BlockSpec(shape, lambda *_: zeros, pipeline_mode=pl.Buffered(1))


def _mixer(x, gmix, win, bmat, are, aim, cre, cim, dskip, wglu, bglu, wpool,
           pscale, gssm, gpool, wout):
    n = x.shape[0]
    consts = (gmix, win, bmat, are, aim, cre, cim, dskip, wglu, bglu, wpool,
              pscale, gssm, gpool, wout)
    row_spec = pl.BlockSpec((MIX_ROWS, D_MODEL), lambda c: (c, 0))
    return pl.pallas_call(
        _mixer_kernel,
        grid=(n // MIX_ROWS,),
        in_specs=[row_spec] + [_const_spec(a.shape) for a in consts],
        out_specs=row_spec,
        out_shape=jax.ShapeDtypeStruct((n, D_MODEL), F32),
        scratch_shapes=[
            pltpu.VMEM((MIX_ROWS, 2 * STATE_W), F32),
            pltpu.VMEM((SUBLANES, STATE_W), F32),
            pltpu.VMEM((SUBLANES, STATE_W), F32),
            pltpu.VMEM((POOL_HIST_ROWS + MIX_ROWS, POOL_WIDTH), F32),
        ],
        compiler_params=pltpu.CompilerParams(
            dimension_semantics=("arbitrary",), vmem_limit_bytes=VMEM_LIMIT),
        name="mixer",
    )(x, *consts)


def _sort16_pairs():
    n, pairs, p = 16, [], 1
    while p < n:
        k = p
        while k >= 1:
            for j in range(k % p, n - k, 2 * k):
                for i in range(min(k, n - j - k)):
                    if (i + j) // (2 * p) == (i + j + k) // (2 * p):
                        pairs.append((i + j, i + j + k))
            k //= 2
        p *= 2
    return pairs


_SORT16 = _sort16_pairs()
N_TOP = PEER_TOPK + 1


def _top_sorted(s, out_ref):
    v = [s[SUBLANES * k:SUBLANES * (k + 1), :] for k in range(16)]
    for i, j in _SORT16:
        hi = jnp.maximum(v[i], v[j])
        lo = jnp.minimum(v[i], v[j])
        v[i], v[j] = hi, lo
    for i in range(N_TOP):
        head = v[0]
        m = jnp.max(head, axis=0, keepdims=True)
        out_ref[i:i + 1, :] = m
        if i + 1 < N_TOP:
            pop = head == m
            depth = N_TOP - i
            v = [jnp.where(pop, v[k + 1] if k + 1 < len(v) else NEG_INF, v[k])
                 for k in range(depth - 1)]


def _route_kernel(x_ref, gffn_ref, wqt_ref, k1_ref, k2_ref,
                  hb_ref, cnt_ref, e1_ref, rank_ref, w2_ref,
                  qt_ref, l1_ref, l2_ref):
    tm = x_ref.shape[0]
    h = _rms(x_ref[...], gffn_ref[...])
    hb = h.astype(BF16)
    hb_ref[...] = _words(hb)
    qt_ref[...] = lax.dot_general(wqt_ref[...], hb, (((1,), (1,)), ((), ())),
                                  preferred_element_type=F32)
    row = lax.broadcasted_iota(jnp.int32, (SUBLANES, tm), 0)

    def head(hd, carry):
        base = pl.multiple_of(hd * 2 * PEER_HALF, 2 * PEER_HALF)
        q1 = qt_ref[pl.ds(base, PEER_HALF), :].astype(BF16)
        q2 = qt_ref[pl.ds(base + PEER_HALF, PEER_HALF), :].astype(BF16)
        s1 = _dot(k1_ref[...], q1)
        s2 = _dot(k2_ref[...], q2)
        _top_sorted(s1, l1_ref)
        _top_sorted(s2, l2_ref)
        m1 = l1_ref[0:1, :]
        m2 = l2_ref[0:1, :]
        a = l1_ref[1:9, :]
        b = l2_ref[1:9, :]
        cands = [
            m1 + l2_ref[0:8, :],
            m1 + l2_ref[8:16, :],
            m2 + a,
            m2 + l1_ref[9:17, :],
            jnp.where(row < 7, l1_ref[1:2, :] + b, NEG_INF),
            jnp.where(row < 4, l1_ref[2:3, :] + b, NEG_INF),
            jnp.where(row < 3, l1_ref[3:4, :] + b, NEG_INF),
            jnp.where(row < 2, l1_ref[4:5, :] + b, NEG_INF),
            jnp.where((row >= 4) & (row < 7), l2_ref[1:2, :] + a, NEG_INF),
            jnp.where(row == 7, m1 + l2_ref[9:17, :], NEG_INF),
        ]
        tops = []
        for i in range(N_TOP):
            m = cands[0]
            for cnd in cands[1:]:
                m = jnp.maximum(m, cnd)
            m = jnp.max(m, axis=0, keepdims=True)
            tops.append(m)
            if i + 1 < N_TOP:
                cands = [jnp.where(cnd == m, NEG_INF, cnd) for cnd in cands]
        tau = 0.5 * (tops[PEER_TOPK - 1] + tops[PEER_TOPK])
        z = jnp.zeros_like(tau)
        for i in range(PEER_TOPK):
            z = z + jnp.exp(tops[i] - tops[0])
        theta = tau - s1
        cnt = jnp.zeros_like(s1)
        rank = jnp.zeros_like(s2)
        for j in range(PEER_TOPK):
            v2j = l2_ref[j:j + 1, :]
            cnt = jnp.where(v2j >= theta, j + 1.0, cnt)
            rank = jnp.where(v2j > s2, j + 1.0, rank)
        e1 = jnp.exp(s1 - m1)
        w2 = jnp.exp(s2 - m2) * (0.5 / z)
        outs = ((cnt_ref, cnt), (e1_ref, e1),
                (rank_ref, _words(rank.astype(BF16))), (w2_ref, _words(w2.astype(BF16))))
        for ref, val in outs:
            for lt in range(tm // LANES):
                ref[hd, lt] = val[:, lt * LANES:(lt + 1) * LANES]
        return carry

    lax.fori_loop(0, PEER_HEADS, head, 0)


def _route(x, gffn, wqt, k1, k2):
    n = x.shape[0]
    tm = ROUTE_TM
    def key_spec(rows):
        return pl.BlockSpec((PEER_HEADS, tm // LANES, rows, LANES), lambda i: (0, i, 0, 0))

    def key_shape(rows, dtype):
        return jax.ShapeDtypeStruct((PEER_HEADS, n // LANES, rows, LANES), dtype)

    return pl.pallas_call(
        _route_kernel,
        grid=(n // tm,),
        in_specs=[pl.BlockSpec((tm, D_MODEL), lambda i: (i, 0)),
                  _const_spec(gffn.shape), _const_spec(wqt.shape),
                  _const_spec(k1.shape), _const_spec(k2.shape)],
        out_specs=[pl.BlockSpec((tm // 2, D_MODEL), lambda i: (i, 0)),
                   key_spec(PEER_N_KEYS), key_spec(PEER_N_KEYS),
                   key_spec(PEER_N_KEYS // 2), key_spec(PEER_N_KEYS // 2)],
        out_shape=[jax.ShapeDtypeStruct((n // 2, D_MODEL), jnp.uint32),
                   key_shape(PEER_N_KEYS, F32), key_shape(PEER_N_KEYS, F32),
                   key_shape(PEER_N_KEYS // 2, jnp.uint32),
                   key_shape(PEER_N_KEYS // 2, jnp.uint32)],
        scratch_shapes=[
            pltpu.VMEM((PEER_HEADS * 2 * PEER_HALF, tm), F32),
            pltpu.VMEM((24, tm), F32),
            pltpu.VMEM((24, tm), F32),
        ],
        compiler_params=pltpu.CompilerParams(
            dimension_semantics=("arbitrary",), vmem_limit_bytes=VMEM_LIMIT),
        name="peer_route",
    )(x, gffn, wqt, k1, k2)


def _packed_row(row):
    tile = jnp.broadcast_to(row, (2 * SUBLANES, LANES)).astype(BF16)
    return jnp.tile(tile, (PEER_N_KEYS // tile.shape[0], 1))


def _peer_kernel(x_ref, hb_ref, cnt_ref, e1_ref, rank_ref, w2_ref, u_ref, vt_ref,
                 gfin_ref, o_ref, s_ref, a_ref, acc_ref, *, final_norm, e_tiles, n_tiles):
    g = pl.program_id(0)
    te = 2 * u_ref.shape[0]
    tm = 2 * hb_ref.shape[0]
    blocks = te // PEER_N_KEYS
    half_piece = PEER_PIECE // 2
    half_keys = PEER_N_KEYS // 2
    cur = g % 2
    prev = 1 - cur

    @pl.when(g == 0)
    def _():
        s_ref[...] = jnp.zeros_like(s_ref)
        a_ref[...] = jnp.zeros_like(a_ref)
        acc_ref[...] = jnp.zeros_like(acc_ref)

    tile2 = jnp.clip(g - 1, 0, n_tiles - 1)
    tile3 = jnp.clip(g - 2, 0, n_tiles - 1)
    i1_base = (tile2 % e_tiles) * blocks

    def piece(r, carry):
        rows = pl.ds(pl.multiple_of(r * PEER_PIECE, PEER_PIECE), PEER_PIECE)
        wrows = pl.ds(pl.multiple_of(r * half_piece, half_piece), half_piece)
        acc_ref[...] += _dot(_halves(vt_ref[r]), _halves(a_ref[cur, wrows, :]))
        for sub in range(PEER_PIECE // PEER_N_KEYS):
            ib = r * (PEER_PIECE // PEER_N_KEYS) + sub
            i1 = i1_base + ib
            brow = pl.ds(pl.multiple_of(ib * PEER_N_KEYS, PEER_N_KEYS), PEER_N_KEYS)
            bwrow = pl.ds(pl.multiple_of(ib * half_keys, half_keys), half_keys)
            for lt in range(tm // LANES):
                cols = slice(lt * LANES, (lt + 1) * LANES)
                gate = None
                for hd in range(PEER_HEADS):
                    cnt = _packed_row(cnt_ref[hd, lt, pl.ds(i1, 1), :])
                    e1 = _packed_row(e1_ref[hd, lt, pl.ds(i1, 1), :])
                    term = jnp.where(_halves(rank_ref[hd, lt]) < cnt,
                                     _halves(w2_ref[hd, lt]) * e1, jnp.zeros((), BF16))
                    gate = term if gate is None else gate + term
                s = s_ref[prev, brow, cols]
                act = s * (1.0 + lax.erf(s * INV_SQRT2))
                a_ref[prev, bwrow, cols] = _words(act.astype(BF16) * gate)
        s_ref[cur, rows, :] = lax.dot_general(
            _halves(u_ref[wrows, :]), _halves(hb_ref[...]), (((1,), (1,)), ((), ())),
            preferred_element_type=F32)
        return carry

    lax.fori_loop(0, te // PEER_PIECE, piece, 0)

    @pl.when((g >= 2) & (tile3 % e_tiles == e_tiles - 1))
    def _():
        out = x_ref[...] + acc_ref[...].T
        if final_norm:
            out = _rms(out, gfin_ref[...])
        o_ref[...] = out
        acc_ref[...] = jnp.zeros_like(acc_ref)


def _peer(x, hb_w, cnt, e1, rank_w, w2_w, u_w, vt_w, gfin, final_norm):
    n = x.shape[0]
    tm, te = PEER_TM, PEER_TE
    e_tiles = PEER_N_EXPERTS // te
    n_tiles = (n // tm) * e_tiles
    last = n_tiles - 1

    def t1(g):
        return jnp.minimum(g, last)

    def t2(g):
        return jnp.clip(g - 1, 0, last)

    def t3(g):
        return jnp.clip(g - 2, 0, last)

    def key_spec(rows):
        return pl.BlockSpec((PEER_HEADS, tm // LANES, rows, LANES),
                            lambda g: (0, t2(g) // e_tiles, 0, 0),
                            pipeline_mode=pl.Buffered(1))

    out_row_spec = pl.BlockSpec((tm, D_MODEL), lambda g: (t3(g) // e_tiles, 0))
    return pl.pallas_call(
        functools.partial(_peer_kernel, final_norm=final_norm, e_tiles=e_tiles,
                          n_tiles=n_tiles),
        grid=(n_tiles + 2,),
        in_specs=[pl.BlockSpec((tm, D_MODEL), lambda g: (t3(g) // e_tiles, 0),
                               pipeline_mode=pl.Buffered(1)),
                  pl.BlockSpec((tm // 2, D_MODEL), lambda g: (t1(g) // e_tiles, 0)),
                  key_spec(PEER_N_KEYS), key_spec(PEER_N_KEYS),
                  key_spec(PEER_N_KEYS // 2), key_spec(PEER_N_KEYS // 2),
                  pl.BlockSpec((te // 2, D_MODEL), lambda g: (t1(g) % e_tiles, 0)),
                  pl.BlockSpec((te // PEER_PIECE, D_MODEL // 2, PEER_PIECE),
                               lambda g: (t3(g) % e_tiles, 0, 0)),
                  pl.BlockSpec(gfin.shape, lambda g: (0, 0))],
        out_specs=out_row_spec,
        out_shape=jax.ShapeDtypeStruct((n, D_MODEL), F32),
        scratch_shapes=[
            pltpu.VMEM((2, te, tm), F32),
            pltpu.VMEM((2, te // 2, tm), jnp.uint32),
            pltpu.VMEM((D_MODEL, tm), F32),
        ],
        compiler_params=pltpu.CompilerParams(
            dimension_semantics=("arbitrary",), vmem_limit_bytes=VMEM_LIMIT),
        name="peer_dense",
    )(x, hb_w, cnt, e1, rank_w, w2_w, u_w, vt_w, gfin)


def _tables_kernel(u_ref, v_ref, uw_ref, vw_ref):
    uw_ref[...] = _words(u_ref[0].astype(BF16))
    vw_ref[0] = _words(v_ref[0].T.astype(BF16))


def _expert_tables(u, v, layer):
    _, e, d = u.shape
    rows = PEER_PIECE
    table_spec = pl.BlockSpec((1, rows, d), lambda i: (layer, i, 0))
    return pl.pallas_call(
        _tables_kernel,
        grid=(e // rows,),
        in_specs=[table_spec, table_spec],
        out_specs=[pl.BlockSpec((rows // 2, d), lambda i: (i, 0)),
                   pl.BlockSpec((1, d // 2, rows), lambda i: (i, 0, 0))],
        out_shape=[jax.ShapeDtypeStruct((e // 2, d), jnp.uint32),
                   jax.ShapeDtypeStruct((e // rows, d // 2, rows), jnp.uint32)],
        compiler_params=pltpu.CompilerParams(
            dimension_semantics=("arbitrary",), vmem_limit_bytes=VMEM_LIMIT),
        name="expert_tables",
    )(u, v)


def _block_diag(blocks):
    g, r, c = blocks.shape
    eye = jnp.eye(g, dtype=blocks.dtype)
    return (blocks[:, :, None, :] * eye[:, None, :, None]).reshape(g * r, g * c)


def kernel(x, norm_mix, w_in, a_re, a_im, log_dt, b_re, b_im, c_re, c_im, d_skip, w_glu, b_glu, w_pool, pool_scale, g_out_ssm, g_out_pool, w_out, norm_ffn, w_q, k1, k2, u_experts, v_experts, norm_final):
    bsz, seq, dm = x.shape
    assert (bsz, dm) == (SUBLANES, D_MODEL) and seq % (MIX_ROWS // SUBLANES) == 0
    depth = w_in.shape[0]
    n = bsz * seq
    row = lambda a: a.reshape(1, -1).astype(F32)

    abr, abi, btr, bti = _discretise(a_re, a_im, log_dt, b_re, b_im)
    xt = jnp.transpose(x, (1, 0, 2)).reshape(n, dm)
    gfin = row(norm_final)

    for i in range(depth):
        hg = SSM_GROUPS // 2
        bmat = jnp.stack([_block_diag(b[lo:lo + hg]) for b in (btr[i], bti[i])
                          for lo in (0, hg)]).astype(BF16)
        cre, cim = (jnp.stack([_block_diag(jnp.transpose(c[lo:lo + hg], (0, 2, 1)))
                               for lo in (0, hg)]).astype(BF16)
                    for c in (c_re[i], c_im[i]))
        xt = _mixer(
            xt, row(norm_mix[i]), w_in[i].astype(BF16), bmat,
            abr[i].reshape(1, STATE_W), abi[i].reshape(1, STATE_W), cre, cim,
            row(d_skip[i]), w_glu[i].astype(BF16), row(b_glu[i]),
            _block_diag(w_pool[i]).astype(BF16), row(pool_scale[i]),
            row(g_out_ssm[i]), row(g_out_pool[i]), w_out[i].astype(BF16))
        hb, cnt, e1, rank, w2 = _route(
            xt, row(norm_ffn[i]), jnp.transpose(w_q[i]).astype(BF16),
            k1[i].astype(BF16), k2[i].astype(BF16))
        u_w, vt_w = _expert_tables(u_experts, v_experts, i)
        xt = _peer(xt, hb, cnt, e1, rank, w2, u_w, vt_w, gfin,
                   final_norm=(i == depth - 1))
    return jnp.transpose(xt.reshape(seq, bsz, dm), (1, 0, 2))
```

```python
import functools
import math

import jax
import jax.numpy as jnp
from jax import lax
from jax.experimental import pallas as pl
from jax.experimental.pallas import tpu as pltpu

F32 = jnp.float32
BF16 = jnp.bfloat16

D_MODEL = 1024
SSM_WIDTH = 512
POOL_WIDTH = 512
SSM_GROUP = 16
SSM_GROUPS = 32
SSM_STATE = 64
STATE_W = SSM_GROUPS * SSM_STATE
HALF_SSM = SSM_WIDTH // 2
HALF_STATE = STATE_W // 2
POOL_WINDOWS = (2, 4, 8, 16)
POOL_GROUP_WIDTH = 128
PEER_HEADS = 8
PEER_N_KEYS = 128
PEER_N_EXPERTS = PEER_N_KEYS * PEER_N_KEYS
PEER_HALF = 128
PEER_TOPK = 16
RMS_EPS = 1e-6

SUBLANES = 8
LANES = 128
MXU_DEPTH = 256
MIX_ROWS = 512
POOL_HIST_ROWS = 128
ROUTE_TM = 512
PEER_TM = 512
PEER_TE = 2048
PEER_PIECE = 1024
VMEM_LIMIT = 60 * 1024 * 1024

NEG_INF = float("-inf")
INV_SQRT2 = 0.7071067811865476


def _rms(x, g):
    return x * lax.rsqrt(jnp.mean(x * x, axis=-1, keepdims=True) + RMS_EPS) * g


def _gelu(x):
    return 0.5 * x * (1.0 + lax.erf(x * INV_SQRT2))


def _dot(a, b):
    return jnp.dot(a, b, preferred_element_type=F32)


def _words(x):
    return pltpu.bitcast(x, jnp.uint32)


def _halves(w):
    return pltpu.bitcast(w, BF16)


def _disc_kernel(are_ref, aim_ref, ldt_ref, bre_ref, bim_ref,
                 abr_ref, abi_ref, btr_ref, bti_ref):
    lam_re = are_ref[...]
    lam_im = aim_ref[...]
    dt = jnp.exp(ldt_ref[...])
    decay = jnp.exp(lam_re * dt)
    abar_re = decay * jnp.cos(lam_im * dt)
    abar_im = decay * jnp.sin(lam_im * dt)
    inv_den = 1.0 / (lam_re * lam_re + lam_im * lam_im)
    num_re = abar_re - 1.0
    zoh_re = (num_re * lam_re + abar_im * lam_im) * inv_den
    zoh_im = (abar_im * lam_re - num_re * lam_im) * inv_den
    b_re = bre_ref[...]
    b_im = bim_ref[...]
    abr_ref[...] = abar_re
    abi_ref[...] = abar_im
    btr_ref[...] = zoh_re * b_re - zoh_im * b_im
    bti_ref[...] = zoh_re * b_im + zoh_im * b_re


def _discretise(a_re, a_im, log_dt, b_re, b_im):
    nl = a_re.shape[0]
    rows = nl * SSM_GROUPS * SSM_GROUP
    shp = (nl, SSM_GROUPS, SSM_GROUP, SSM_STATE)

    def rep(a):
        return jnp.broadcast_to(a[:, :, None, :], shp).reshape(rows, SSM_STATE)

    ldt = jnp.broadcast_to(log_dt[:, :, None, None], shp).reshape(rows, SSM_STATE)
    bre = jnp.transpose(b_re, (0, 1, 3, 2)).reshape(rows, SSM_STATE)
    bim = jnp.transpose(b_im, (0, 1, 3, 2)).reshape(rows, SSM_STATE)
    out = jax.ShapeDtypeStruct((rows, SSM_STATE), F32)
    abr, abi, btr, bti = pl.pallas_call(
        _disc_kernel, out_shape=(out, out, out, out), name="s5_discretise",
    )(rep(a_re), rep(a_im), ldt, bre, bim)
    abr = abr.reshape(shp)[:, :, 0, :].reshape(nl, STATE_W)
    abi = abi.reshape(shp)[:, :, 0, :].reshape(nl, STATE_W)
    return abr, abi, btr.reshape(shp), bti.reshape(shp)


def _mixer_kernel(x_ref, gmix_ref, win_ref, bmat_ref, are_ref, aim_ref,
                  cre_ref, cim_ref, dskip_ref, wglu_ref, bglu_ref, wpool_ref,
                  pscale_ref, gssm_ref, gpool_ref, wout_ref, o_ref,
                  st_ref, sre_ref, sim_ref, ext_ref):
    c = pl.program_id(0)
    rows = x_ref.shape[0]
    steps = rows // SUBLANES

    @pl.when(c == 0)
    def _():
        sre_ref[...] = jnp.zeros_like(sre_ref)
        sim_ref[...] = jnp.zeros_like(sim_ref)
        ext_ref[0:POOL_HIST_ROWS, :] = jnp.zeros((POOL_HIST_ROWS, POOL_WIDTH), F32)

    xr = x_ref[...]
    hn = _rms(xr, gmix_ref[...])
    proj = _dot(hn.astype(BF16), win_ref[...])
    u_ssm = proj[:, :SSM_WIDTH]
    u_pool = proj[:, SSM_WIDTH:]

    u_b = u_ssm.astype(BF16)
    for part in range(2):
        for hf in range(2):
            lo = part * STATE_W + hf * HALF_STATE
            st_ref[:, lo:lo + HALF_STATE] = _dot(
                u_b[:, hf * HALF_SSM:(hf + 1) * HALF_SSM], bmat_ref[2 * part + hf])
    a_re = jnp.broadcast_to(are_ref[...], (SUBLANES, STATE_W))
    a_im = jnp.broadcast_to(aim_ref[...], (SUBLANES, STATE_W))

    def step(t, carry):
        s_re, s_im = carry
        r = pl.multiple_of(t * SUBLANES, SUBLANES)
        in_re = st_ref[pl.ds(r, SUBLANES), 0:STATE_W]
        in_im = st_ref[pl.ds(r, SUBLANES), STATE_W:2 * STATE_W]
        n_re = a_re * s_re - a_im * s_im + in_re
        n_im = a_re * s_im + a_im * s_re + in_im
        st_ref[pl.ds(r, SUBLANES), 0:STATE_W] = n_re
        st_ref[pl.ds(r, SUBLANES), STATE_W:2 * STATE_W] = n_im
        return n_re, n_im

    s_re, s_im = lax.fori_loop(0, steps, step, (sre_ref[...], sim_ref[...]))
    sre_ref[...] = s_re
    sim_ref[...] = s_im

    y = []
    for hf in range(2):
        lo = hf * HALF_STATE
        y.append(_dot(st_ref[:, lo:lo + HALF_STATE].astype(BF16), cre_ref[hf])
                 - _dot(st_ref[:, STATE_W + lo:STATE_W + lo + HALF_STATE].astype(BF16),
                        cim_ref[hf]))
    y = jnp.concatenate(y, axis=1) + dskip_ref[...] * u_ssm
    y = _gelu(y)
    y = y * jax.nn.sigmoid(_dot(y.astype(BF16), wglu_ref[...]) + bglu_ref[...])
    ssm_n = _rms(y, gssm_ref[...])

    ext_ref[POOL_HIST_ROWS:, :] = u_pool
    t_idx = c * steps + jnp.right_shift(
        lax.broadcasted_iota(jnp.int32, (rows, POOL_GROUP_WIDTH), 0), 3)
    pooled = []
    for gi, win in enumerate(POOL_WINDOWS):
        lo = gi * POOL_GROUP_WIDTH
        hi = lo + POOL_GROUP_WIDTH
        acc = ext_ref[POOL_HIST_ROWS:, lo:hi]
        for k in range(1, win):
            off = POOL_HIST_ROWS - SUBLANES * k
            acc = acc + ext_ref[off:off + rows, lo:hi]
        count = jnp.minimum(t_idx + 1, win).astype(F32)
        pooled.append(acc / count - ext_ref[POOL_HIST_ROWS:, lo:hi])
    ext_ref[0:POOL_HIST_ROWS, :] = ext_ref[rows:rows + POOL_HIST_ROWS, :]
    pooled = jnp.concatenate(pooled, axis=1)
    y_pool = _dot(pooled.astype(BF16), wpool_ref[...]) * pscale_ref[...]
    pool_n = _rms(y_pool, gpool_ref[...])

    res = (_dot(ssm_n.astype(BF16), wout_ref[0:SSM_WIDTH, :])
           + _dot(pool_n.astype(BF16), wout_ref[SSM_WIDTH:, :]))
    o_ref[...] = xr + res


def _const_spec(shape):
    zeros = (0,) * len(shape)
    return pl.BlockSpec(shape, lambda *_: zeros, pipeline_mode=pl.Buffered(1))


def _mixer(x, gmix, win, bmat, are, aim, cre, cim, dskip, wglu, bglu, wpool,
           pscale, gssm, gpool, wout):
    n = x.shape[0]
    consts = (gmix, win, bmat, are, aim, cre, cim, dskip, wglu, bglu, wpool,
              pscale, gssm, gpool, wout)
    row_spec = pl.BlockSpec((MIX_ROWS, D_MODEL), lambda c: (c, 0))
    return pl.pallas_call(
        _mixer_kernel,
        grid=(n // MIX_ROWS,),
        in_specs=[row_spec] + [_const_spec(a.shape) for a in consts],
        out_specs=row_spec,
        out_shape=jax.ShapeDtypeStruct((n, D_MODEL), F32),
        scratch_shapes=[
            pltpu.VMEM((MIX_ROWS, 2 * STATE_W), F32),
            pltpu.VMEM((SUBLANES, STATE_W), F32),
            pltpu.VMEM((SUBLANES, STATE_W), F32),
            pltpu.VMEM((POOL_HIST_ROWS + MIX_ROWS, POOL_WIDTH), F32),
        ],
        compiler_params=pltpu.CompilerParams(
            dimension_semantics=("arbitrary",), vmem_limit_bytes=VMEM_LIMIT),
        name="mixer",
    )(x, *consts)


def _sort16_pairs():
    n, pairs, p = 16, [], 1
    while p < n:
        k = p
        while k >= 1:
            for j in range(k % p, n - k, 2 * k):
                for i in range(min(k, n - j - k)):
                    if (i + j) // (2 * p) == (i + j + k) // (2 * p):
                        pairs.append((i + j, i + j + k))
            k //= 2
        p *= 2
    return pairs


_SORT16 = _sort16_pairs()
N_TOP = PEER_TOPK + 1


def _top_sorted(s, out_ref):
    v = [s[SUBLANES * k:SUBLANES * (k + 1), :] for k in range(16)]
    for i, j in _SORT16:
        hi = jnp.maximum(v[i], v[j])
        lo = jnp.minimum(v[i], v[j])
        v[i], v[j] = hi, lo
    for i in range(N_TOP):
        head = v[0]
        m = jnp.max(head, axis=0, keepdims=True)
        out_ref[i:i + 1, :] = m
        if i + 1 < N_TOP:
            pop = head == m
            depth = N_TOP - i
            v = [jnp.where(pop, v[k + 1] if k + 1 < len(v) else NEG_INF, v[k])
                 for k in range(depth - 1)]


def _route_kernel(x_ref, gffn_ref, wqt_ref, k1_ref, k2_ref,
                  hb_ref, cnt_ref, e1_ref, rank_ref, w2_ref,
                  qt_ref, l1_ref, l2_ref):
    tm = x_ref.shape[0]
    h = _rms(x_ref[...], gffn_ref[...])
    hb = h.astype(BF16)
    hb_ref[...] = _words(hb)
    qt_ref[...] = lax.dot_general(wqt_ref[...], hb, (((1,), (1,)), ((), ())),
                                  preferred_element_type=F32)
    row = lax.broadcasted_iota(jnp.int32, (SUBLANES, tm), 0)

    def head(hd, carry):
        base = pl.multiple_of(hd * 2 * PEER_HALF, 2 * PEER_HALF)
        q1 = qt_ref[pl.ds(base, PEER_HALF), :].astype(BF16)
        q2 = qt_ref[pl.ds(base + PEER_HALF, PEER_HALF), :].astype(BF16)
        s1 = _dot(k1_ref[...], q1)
        s2 = _dot(k2_ref[...], q2)
        _top_sorted(s1, l1_ref)
        _top_sorted(s2, l2_ref)
        m1 = l1_ref[0:1, :]
        m2 = l2_ref[0:1, :]
        a = l1_ref[1:9, :]
        b = l2_ref[1:9, :]
        cands = [
            m1 + l2_ref[0:8, :],
            m1 + l2_ref[8:16, :],
            m2 + a,
            m2 + l1_ref[9:17, :],
            jnp.where(row < 7, l1_ref[1:2, :] + b, NEG_INF),
            jnp.where(row < 4, l1_ref[2:3, :] + b, NEG_INF),
            jnp.where(row < 3, l1_ref[3:4, :] + b, NEG_INF),
            jnp.where(row < 2, l1_ref[4:5, :] + b, NEG_INF),
            jnp.where((row >= 4) & (row < 7), l2_ref[1:2, :] + a, NEG_INF),
            jnp.where(row == 7, m1 + l2_ref[9:17, :], NEG_INF),
        ]
        tops = []
        for i in range(N_TOP):
            m = cands[0]
            for cnd in cands[1:]:
                m = jnp.maximum(m, cnd)
            m = jnp.max(m, axis=0, keepdims=True)
            tops.append(m)
            if i + 1 < N_TOP:
                cands = [jnp.where(cnd == m, NEG_INF, cnd) for cnd in cands]
        tau = 0.5 * (tops[PEER_TOPK - 1] + tops[PEER_TOPK])
        z = jnp.zeros_like(tau)
        for i in range(PEER_TOPK):
            z = z + jnp.exp(tops[i] - tops[0])
        theta = tau - s1
        cnt = jnp.zeros_like(s1)
        rank = jnp.zeros_like(s2)
        for j in range(PEER_TOPK):
            v2j = l2_ref[j:j + 1, :]
            cnt = jnp.where(v2j >= theta, j + 1.0, cnt)
            rank = jnp.where(v2j > s2, j + 1.0, rank)
        e1 = jnp.exp(s1 - m1)
        w2 = jnp.exp(s2 - m2) * (0.5 / z)
        outs = ((cnt_ref, cnt), (e1_ref, e1),
                (rank_ref, _words(rank.astype(BF16))), (w2_ref, _words(w2.astype(BF16))))
        for ref, val in outs:
            for lt in range(tm // LANES):
                ref[hd, lt] = val[:, lt * LANES:(lt + 1) * LANES]
        return carry

    lax.fori_loop(0, PEER_HEADS, head, 0)


def _route(x, gffn, wqt, k1, k2):
    n = x.shape[0]
    tm = ROUTE_TM
    def key_spec(rows):
        return pl.BlockSpec((PEER_HEADS, tm // LANES, rows, LANES), lambda i: (0, i, 0, 0))

    def key_shape(rows, dtype):
        return jax.ShapeDtypeStruct((PEER_HEADS, n // LANES, rows, LANES), dtype)

    return pl.pallas_call(
        _route_kernel,
        grid=(n // tm,),
        in_specs=[pl.BlockSpec((tm, D_MODEL), lambda i: (i, 0)),
                  _const_spec(gffn.shape), _const_spec(wqt.shape),
                  _const_spec(k1.shape), _const_spec(k2.shape)],
        out_specs=[pl.BlockSpec((tm // 2, D_MODEL), lambda i: (i, 0)),
                   key_spec(PEER_N_KEYS), key_spec(PEER_N_KEYS),
                   key_spec(PEER_N_KEYS // 2), key_spec(PEER_N_KEYS // 2)],
        out_shape=[jax.ShapeDtypeStruct((n // 2, D_MODEL), jnp.uint32),
                   key_shape(PEER_N_KEYS, F32), key_shape(PEER_N_KEYS, F32),
                   key_shape(PEER_N_KEYS // 2, jnp.uint32),
                   key_shape(PEER_N_KEYS // 2, jnp.uint32)],
        scratch_shapes=[
            pltpu.VMEM((PEER_HEADS * 2 * PEER_HALF, tm), F32),
            pltpu.VMEM((24, tm), F32),
            pltpu.VMEM((24, tm), F32),
        ],
        compiler_params=pltpu.CompilerParams(
            dimension_semantics=("arbitrary",), vmem_limit_bytes=VMEM_LIMIT),
        name="peer_route",
    )(x, gffn, wqt, k1, k2)


def _packed_row(row):
    tile = jnp.broadcast_to(row, (2 * SUBLANES, LANES)).astype(BF16)
    return jnp.tile(tile, (PEER_N_KEYS // tile.shape[0], 1))


def _peer_kernel(x_ref, hb_ref, cnt_ref, e1_ref, rank_ref, w2_ref, u_ref, vt_ref,
                 gfin_ref, o_ref, s_ref, a_ref, acc_ref, *, final_norm, e_tiles, n_tiles):
    g = pl.program_id(0)
    te = 2 * u_ref.shape[0]
    tm = 2 * hb_ref.shape[0]
    blocks = te // PEER_N_KEYS
    half_piece = PEER_PIECE // 2
    half_keys = PEER_N_KEYS // 2
    cur = g % 2
    prev = 1 - cur

    @pl.when(g == 0)
    def _():
        s_ref[...] = jnp.zeros_like(s_ref)
        a_ref[...] = jnp.zeros_like(a_ref)
        acc_ref[...] = jnp.zeros_like(acc_ref)

    tile2 = jnp.clip(g - 1, 0, n_tiles - 1)
    tile3 = jnp.clip(g - 2, 0, n_tiles - 1)
    i1_base = (tile2 % e_tiles) * blocks

    def piece(r, carry):
        rows = pl.ds(pl.multiple_of(r * PEER_PIECE, PEER_PIECE), PEER_PIECE)
        wrows = pl.ds(pl.multiple_of(r * half_piece, half_piece), half_piece)
        acc_ref[...] += _dot(_halves(vt_ref[r]), _halves(a_ref[cur, wrows, :]))
        for sub in range(PEER_PIECE // PEER_N_KEYS):
            ib = r * (PEER_PIECE // PEER_N_KEYS) + sub
            i1 = i1_base + ib
            brow = pl.ds(pl.multiple_of(ib * PEER_N_KEYS, PEER_N_KEYS), PEER_N_KEYS)
            bwrow = pl.ds(pl.multiple_of(ib * half_keys, half_keys), half_keys)
            for lt in range(tm // LANES):
                cols = slice(lt * LANES, (lt + 1) * LANES)
                gate = None
                for hd in range(PEER_HEADS):
                    cnt = _packed_row(cnt_ref[hd, lt, pl.ds(i1, 1), :])
                    e1 = _packed_row(e1_ref[hd, lt, pl.ds(i1, 1), :])
                    term = jnp.where(_halves(rank_ref[hd, lt]) < cnt,
                                     _halves(w2_ref[hd, lt]) * e1, jnp.zeros((), BF16))
                    gate = term if gate is None else gate + term
                s = s_ref[prev, brow, cols]
                act = s * (1.0 + lax.erf(s * INV_SQRT2))
                a_ref[prev, bwrow, cols] = _words(act.astype(BF16) * gate)
        s_ref[cur, rows, :] = lax.dot_general(
            _halves(u_ref[wrows, :]), _halves(hb_ref[...]), (((1,), (1,)), ((), ())),
            preferred_element_type=F32)
        return carry

    lax.fori_loop(0, te // PEER_PIECE, piece, 0)

    @pl.when((g >= 2) & (tile3 % e_tiles == e_tiles - 1))
    def _():
        out = x_ref[...] + acc_ref[...].T
        if final_norm:
            out = _rms(out, gfin_ref[...])
        o_ref[...] = out
        acc_ref[...] = jnp.zeros_like(acc_ref)


def _peer(x, hb_w, cnt, e1, rank_w, w2_w, u_w, vt_w, gfin, final_norm):
    n = x.shape[0]
    tm, te = PEER_TM, PEER_TE
    e_tiles = PEER_N_EXPERTS // te
    n_tiles = (n // tm) * e_tiles
    last = n_tiles - 1

    def t1(g):
        return jnp.minimum(g, last)

    def t2(g):
        return jnp.clip(g - 1, 0, last)

    def t3(g):
        return jnp.clip(g - 2, 0, last)

    def key_spec(rows):
        return pl.BlockSpec((PEER_HEADS, tm // LANES, rows, LANES),
                            lambda g: (0, t2(g) // e_tiles, 0, 0),
                            pipeline_mode=pl.Buffered(1))

    out_row_spec = pl.BlockSpec((tm, D_MODEL), lambda g: (t3(g) // e_tiles, 0))
    return pl.pallas_call(
        functools.partial(_peer_kernel, final_norm=final_norm, e_tiles=e_tiles,
                          n_tiles=n_tiles),
        grid=(n_tiles + 2,),
        in_specs=[pl.BlockSpec((tm, D_MODEL), lambda g: (t3(g) // e_tiles, 0),
                               pipeline_mode=pl.Buffered(1)),
                  pl.BlockSpec((tm // 2, D_MODEL), lambda g: (t1(g) // e_tiles, 0)),
                  key_spec(PEER_N_KEYS), key_spec(PEER_N_KEYS),
                  key_spec(PEER_N_KEYS // 2), key_spec(PEER_N_KEYS // 2),
                  pl.BlockSpec((te // 2, D_MODEL), lambda g: (t1(g) % e_tiles, 0)),
                  pl.BlockSpec((te // PEER_PIECE, D_MODEL // 2, PEER_PIECE),
                               lambda g: (t3(g) % e_tiles, 0, 0)),
                  pl.BlockSpec(gfin.shape, lambda g: (0, 0))],
        out_specs=out_row_spec,
        out_shape=jax.ShapeDtypeStruct((n, D_MODEL), F32),
        scratch_shapes=[
            pltpu.VMEM((2, te, tm), F32),
            pltpu.VMEM((2, te // 2, tm), jnp.uint32),
            pltpu.VMEM((D_MODEL, tm), F32),
        ],
        compiler_params=pltpu.CompilerParams(
            dimension_semantics=("arbitrary",), vmem_limit_bytes=VMEM_LIMIT),
        name="peer_dense",
    )(x, hb_w, cnt, e1, rank_w, w2_w, u_w, vt_w, gfin)


def _tables_kernel(u_ref, v_ref, uw_ref, vw_ref):
    uw_ref[...] = _words(u_ref[0].astype(BF16))
    vw_ref[0] = _words(v_ref[0].T.astype(BF16))


def _expert_tables(u, v, layer):
    _, e, d = u.shape
    rows = PEER_PIECE
    table_spec = pl.BlockSpec((1, rows, d), lambda i: (layer, i, 0))
    return pl.pallas_call(
        _tables_kernel,
        grid=(e // rows,),
        in_specs=[table_spec, table_spec],
        out_specs=[pl.BlockSpec((rows // 2, d), lambda i: (i, 0)),
                   pl.BlockSpec((1, d // 2, rows), lambda i: (i, 0, 0))],
        out_shape=[jax.ShapeDtypeStruct((e // 2, d), jnp.uint32),
                   jax.ShapeDtypeStruct((e // rows, d // 2, rows), jnp.uint32)],
        compiler_params=pltpu.CompilerParams(
            dimension_semantics=("arbitrary",), vmem_limit_bytes=VMEM_LIMIT),
        name="expert_tables",
    )(u, v)


def _block_diag(blocks):
    g, r, c = blocks.shape
    eye = jnp.eye(g, dtype=blocks.dtype)
    return (blocks[:, :, None, :] * eye[:, None, :, None]).reshape(g * r, g * c)


def kernel(x, norm_mix, w_in, a_re, a_im, log_dt, b_re, b_im, c_re, c_im, d_skip, w_glu, b_glu, w_pool, pool_scale, g_out_ssm, g_out_pool, w_out, norm_ffn, w_q, k1, k2, u_experts, v_experts, norm_final):
    bsz, seq, dm = x.shape
    assert (bsz, dm) == (SUBLANES, D_MODEL) and seq % (MIX_ROWS // SUBLANES) == 0
    depth = w_in.shape[0]
    n = bsz * seq
    row = lambda a: a.reshape(1, -1).astype(F32)

    abr, abi, btr, bti = _discretise(a_re, a_im, log_dt, b_re, b_im)
    xt = jnp.transpose(x, (1, 0, 2)).reshape(n, dm)
    gfin = row(norm_final)

    for i in range(depth):
        hg = SSM_GROUPS // 2
        bmat = jnp.stack([_block_diag(b[lo:lo + hg]) for b in (btr[i], bti[i])
                          for lo in (0, hg)]).astype(BF16)
        cre, cim = (jnp.stack([_block_diag(jnp.transpose(c[lo:lo + hg], (0, 2, 1)))
                               for lo in (0, hg)]).astype(BF16)
                    for c in (c_re[i], c_im[i]))
        xt = _mixer(
            xt, row(norm_mix[i]), w_in[i].astype(BF16), bmat,
            abr[i].reshape(1, STATE_W), abi[i].reshape(1, STATE_W), cre, cim,
            row(d_skip[i]), w_glu[i].astype(BF16), row(b_glu[i]),
            _block_diag(w_pool[i]).astype(BF16), row(pool_scale[i]),
            row(g_out_ssm[i]), row(g_out_pool[i]), w_out[i].astype(BF16))
        hb, cnt, e1, rank, w2 = _route(
            xt, row(norm_ffn[i]), jnp.transpose(w_q[i]).astype(BF16),
            k1[i].astype(BF16), k2[i].astype(BF16))
        u_w, vt_w = _expert_tables(u_experts, v_experts, i)
        xt = _peer(xt, hb, cnt, e1, rank, w2, u_w, vt_w, gfin,
                   final_norm=(i == depth - 1))
    return jnp.transpose(xt.reshape(seq, bsz, dm), (1, 0, 2))
```

```python
import functools
import math

import jax
import jax.numpy as jnp
from jax import lax
from jax.experimental import pallas as pl
from jax.experimental.pallas import tpu as pltpu

F32 = jnp.float32
BF16 = jnp.bfloat16

D_MODEL = 1024
SSM_WIDTH = 512
POOL_WIDTH = 512
SSM_GROUP = 16
SSM_GROUPS = 32
SSM_STATE = 64
STATE_W = SSM_GROUPS * SSM_STATE
HALF_SSM = SSM_WIDTH // 2
HALF_STATE = STATE_W // 2
POOL_WINDOWS = (2, 4, 8, 16)
POOL_GROUP_WIDTH = 128
PEER_HEADS = 8
PEER_N_KEYS = 128
PEER_N_EXPERTS = PEER_N_KEYS * PEER_N_KEYS
PEER_HALF = 128
PEER_TOPK = 16
RMS_EPS = 1e-6

SUBLANES = 8
LANES = 128
MXU_DEPTH = 256
MIX_ROWS = 512
POOL_HIST_ROWS = 128
ROUTE_TM = 512
PEER_TM = 512
PEER_TE = 2048
PEER_PIECE = 1024
GATE_GROUP = 1
VMEM_LIMIT = 60 * 1024 * 1024

NEG_INF = float("-inf")
INV_SQRT2 = 0.7071067811865476


def _rms(x, g):
    return x * lax.rsqrt(jnp.mean(x * x, axis=-1, keepdims=True) + RMS_EPS) * g


def _gelu(x):
    return 0.5 * x * (1.0 + lax.erf(x * INV_SQRT2))


def _dot(a, b):
    return jnp.dot(a, b, preferred_element_type=F32)


def _words(x):
    return pltpu.bitcast(x, jnp.uint32)


def _halves(w):
    return pltpu.bitcast(w, BF16)


def _disc_kernel(are_ref, aim_ref, ldt_ref, bre_ref, bim_ref,
                 abr_ref, abi_ref, btr_ref, bti_ref):
    lam_re = are_ref[...]
    lam_im = aim_ref[...]
    dt = jnp.exp(ldt_ref[...])
    decay = jnp.exp(lam_re * dt)
    abar_re = decay * jnp.cos(lam_im * dt)
    abar_im = decay * jnp.sin(lam_im * dt)
    inv_den = 1.0 / (lam_re * lam_re + lam_im * lam_im)
    num_re = abar_re - 1.0
    zoh_re = (num_re * lam_re + abar_im * lam_im) * inv_den
    zoh_im = (abar_im * lam_re - num_re * lam_im) * inv_den
    b_re = bre_ref[...]
    b_im = bim_ref[...]
    abr_ref[...] = abar_re
    abi_ref[...] = abar_im
    btr_ref[...] = zoh_re * b_re - zoh_im * b_im
    bti_ref[...] = zoh_re * b_im + zoh_im * b_re


def _discretise(a_re, a_im, log_dt, b_re, b_im):
    nl = a_re.shape[0]
    rows = nl * SSM_GROUPS * SSM_GROUP
    shp = (nl, SSM_GROUPS, SSM_GROUP, SSM_STATE)

    def rep(a):
        return jnp.broadcast_to(a[:, :, None, :], shp).reshape(rows, SSM_STATE)

    ldt = jnp.broadcast_to(log_dt[:, :, None, None], shp).reshape(rows, SSM_STATE)
    bre = jnp.transpose(b_re, (0, 1, 3, 2)).reshape(rows, SSM_STATE)
    bim = jnp.transpose(b_im, (0, 1, 3, 2)).reshape(rows, SSM_STATE)
    out = jax.ShapeDtypeStruct((rows, SSM_STATE), F32)
    abr, abi, btr, bti = pl.pallas_call(
        _disc_kernel, out_shape=(out, out, out, out), name="s5_discretise",
    )(rep(a_re), rep(a_im), ldt, bre, bim)
    abr = abr.reshape(shp)[:, :, 0, :].reshape(nl, STATE_W)
    abi = abi.reshape(shp)[:, :, 0, :].reshape(nl, STATE_W)
    return abr, abi, btr.reshape(shp), bti.reshape(shp)


def _mixer_kernel(x_ref, gmix_ref, win_ref, bmat_ref, are_ref, aim_ref,
                  cre_ref, cim_ref, dskip_ref, wglu_ref, bglu_ref, wpool_ref,
                  pscale_ref, gssm_ref, gpool_ref, wout_ref, o_ref,
                  st_ref, sre_ref, sim_ref, ext_ref, *, batch_major_in):
    c = pl.program_id(0)
    rows = o_ref.shape[0]
    steps = rows // SUBLANES

    @pl.when(c == 0)
    def _():
        sre_ref[...] = jnp.zeros_like(sre_ref)
        sim_ref[...] = jnp.zeros_like(sim_ref)
        ext_ref[0:POOL_HIST_ROWS, :] = jnp.zeros((POOL_HIST_ROWS, POOL_WIDTH), F32)

    if batch_major_in:
        xr = pltpu.einshape("btd->(tb)d", x_ref[...])
    else:
        xr = x_ref[...]
    hn = _rms(xr, gmix_ref[...])
    proj = _dot(hn.astype(BF16), win_ref[...])
    u_ssm = proj[:, :SSM_WIDTH]
    u_pool = proj[:, SSM_WIDTH:]

    u_b = u_ssm.astype(BF16)
    for part in range(2):
        for hf in range(2):
            lo = part * STATE_W + hf * HALF_STATE
            st_ref[:, lo:lo + HALF_STATE] = _dot(
                u_b[:, hf * HALF_SSM:(hf + 1) * HALF_SSM], bmat_ref[2 * part + hf])
    a_re = jnp.broadcast_to(are_ref[...], (SUBLANES, STATE_W))
    a_im = jnp.broadcast_to(aim_ref[...], (SUBLANES, STATE_W))

    def step(t, carry):
        s_re, s_im = carry
        r = pl.multiple_of(t * SUBLANES, SUBLANES)
        in_re = st_ref[pl.ds(r, SUBLANES), 0:STATE_W]
        in_im = st_ref[pl.ds(r, SUBLANES), STATE_W:2 * STATE_W]
        n_re = a_re * s_re - a_im * s_im + in_re
        n_im = a_re * s_im + a_im * s_re + in_im
        st_ref[pl.ds(r, SUBLANES), 0:STATE_W] = n_re
        st_ref[pl.ds(r, SUBLANES), STATE_W:2 * STATE_W] = n_im
        return n_re, n_im

    s_re, s_im = lax.fori_loop(0, steps, step, (sre_ref[...], sim_ref[...]))
    sre_ref[...] = s_re
    sim_ref[...] = s_im

    y = []
    for hf in range(2):
        lo = hf * HALF_STATE
        y.append(_dot(st_ref[:, lo:lo + HALF_STATE].astype(BF16), cre_ref[hf])
                 - _dot(st_ref[:, STATE_W + lo:STATE_W + lo + HALF_STATE].astype(BF16),
                        cim_ref[hf]))
    y = jnp.concatenate(y, axis=1) + dskip_ref[...] * u_ssm
    y = _gelu(y)
    y = y * jax.nn.sigmoid(_dot(y.astype(BF16), wglu_ref[...]) + bglu_ref[...])
    ssm_n = _rms(y, gssm_ref[...])

    ext_ref[POOL_HIST_ROWS:, :] = u_pool
    t_idx = c * steps + jnp.right_shift(
        lax.broadcasted_iota(jnp.int32, (rows, POOL_GROUP_WIDTH), 0), 3)
    pooled = []
    for gi, win in enumerate(POOL_WINDOWS):
        lo = gi * POOL_GROUP_WIDTH
        hi = lo + POOL_GROUP_WIDTH
        acc = ext_ref[POOL_HIST_ROWS:, lo:hi]
        for k in range(1, win):
            off = POOL_HIST_ROWS - SUBLANES * k
            acc = acc + ext_ref[off:off + rows, lo:hi]
        count = jnp.minimum(t_idx + 1, win).astype(F32)
        pooled.append(acc / count - ext_ref[POOL_HIST_ROWS:, lo:hi])
    ext_ref[0:POOL_HIST_ROWS, :] = ext_ref[rows:rows + POOL_HIST_ROWS, :]
    pooled = jnp.concatenate(pooled, axis=1)
    y_pool = _dot(pooled.astype(BF16), wpool_ref[...]) * pscale_ref[...]
    pool_n = _rms(y_pool, gpool_ref[...])

    res = (_dot(ssm_n.astype(BF16), wout_ref[0:SSM_WIDTH, :])
           + _dot(pool_n.astype(BF16), wout_ref[SSM_WIDTH:, :]))
    o_ref[...] = xr + res


def _const_spec(shape):
    zeros = (0,) * len(shape)
    return pl.BlockSpec(shape, lambda *_: zeros, pipeline_mode=pl.Buffered(1))


def _mixer(x, gmix, win, bmat, are, aim, cre, cim, dskip, wglu, bglu, wpool,
           pscale, gssm, gpool, wout):
    batch_major_in = x.ndim == 3
    n = x.shape[0] * x.shape[1] if batch_major_in else x.shape[0]
    consts = (gmix, win, bmat, are, aim, cre, cim, dskip, wglu, bglu, wpool,
              pscale, gssm, gpool, wout)
    row_spec = pl.BlockSpec((MIX_ROWS, D_MODEL), lambda c: (c, 0))
    x_spec = (pl.BlockSpec((SUBLANES, MIX_ROWS // SUBLANES, D_MODEL), lambda c: (0, c, 0))
              if batch_major_in else row_spec)
    return pl.pallas_call(
        functools.partial(_mixer_kernel, batch_major_in=batch_major_in),
        grid=(n // MIX_ROWS,),
        in_specs=[x_spec] + [_const_spec(a.shape) for a in consts],
        out_specs=row_spec,
        out_shape=jax.ShapeDtypeStruct((n, D_MODEL), F32),
        scratch_shapes=[
            pltpu.VMEM((MIX_ROWS, 2 * STATE_W), F32),
            pltpu.VMEM((SUBLANES, STATE_W), F32),
            pltpu.VMEM((SUBLANES, STATE_W), F32),
            pltpu.VMEM((POOL_HIST_ROWS + MIX_ROWS, POOL_WIDTH), F32),
        ],
        compiler_params=pltpu.CompilerParams(
            dimension_semantics=("arbitrary",), vmem_limit_bytes=VMEM_LIMIT),
        name="mixer",
    )(x, *consts)


def _sort16_pairs():
    n, pairs, p = 16, [], 1
    while p < n:
        k = p
        while k >= 1:
            for j in range(k % p, n - k, 2 * k):
                for i in range(min(k, n - j - k)):
                    if (i + j) // (2 * p) == (i + j + k) // (2 * p):
                        pairs.append((i + j, i + j + k))
            k //= 2
        p *= 2
    return pairs


_SORT16 = _sort16_pairs()
N_TOP = PEER_TOPK + 1


def _top_sorted(s, out_ref):
    v = [s[SUBLANES * k:SUBLANES * (k + 1), :] for k in range(16)]
    for i, j in _SORT16:
        hi = jnp.maximum(v[i], v[j])
        lo = jnp.minimum(v[i], v[j])
        v[i], v[j] = hi, lo
    for i in range(N_TOP):
        head = v[0]
        m = jnp.max(head, axis=0, keepdims=True)
        out_ref[i:i + 1, :] = m
        if i + 1 < N_TOP:
            pop = head == m
            depth = N_TOP - i
            v = [jnp.where(pop, v[k + 1] if k + 1 < len(v) else NEG_INF, v[k])
                 for k in range(depth - 1)]


def _route_kernel(x_ref, gffn_ref, wqt_ref, k1_ref, k2_ref,
                  hb_ref, cnt_ref, e1_ref, rank_ref, w2_ref,
                  qt_ref, l1_ref, l2_ref):
    tm = x_ref.shape[0]
    h = _rms(x_ref[...], gffn_ref[...])
    hb = h.astype(BF16)
    hb_ref[...] = _words(hb)
    qt_ref[...] = lax.dot_general(wqt_ref[...], hb, (((1,), (1,)), ((), ())),
                                  preferred_element_type=F32)
    row = lax.broadcasted_iota(jnp.int32, (SUBLANES, tm), 0)

    def head(hd, carry):
        base = pl.multiple_of(hd * 2 * PEER_HALF, 2 * PEER_HALF)
        q1 = qt_ref[pl.ds(base, PEER_HALF), :].astype(BF16)
        q2 = qt_ref[pl.ds(base + PEER_HALF, PEER_HALF), :].astype(BF16)
        s1 = _dot(k1_ref[...], q1)
        s2 = _dot(k2_ref[...], q2)
        _top_sorted(s1, l1_ref)
        _top_sorted(s2, l2_ref)
        m1 = l1_ref[0:1, :]
        m2 = l2_ref[0:1, :]
        a = l1_ref[1:9, :]
        b = l2_ref[1:9, :]
        cands = [
            m1 + l2_ref[0:8, :],
            m1 + l2_ref[8:16, :],
            m2 + a,
            m2 + l1_ref[9:17, :],
            jnp.where(row < 7, l1_ref[1:2, :] + b, NEG_INF),
            jnp.where(row < 4, l1_ref[2:3, :] + b, NEG_INF),
            jnp.where(row < 3, l1_ref[3:4, :] + b, NEG_INF),
            jnp.where(row < 2, l1_ref[4:5, :] + b, NEG_INF),
            jnp.where((row >= 4) & (row < 7), l2_ref[1:2, :] + a, NEG_INF),
            jnp.where(row == 7, m1 + l2_ref[9:17, :], NEG_INF),
        ]
        tops = []
        for i in range(N_TOP):
            m = cands[0]
            for cnd in cands[1:]:
                m = jnp.maximum(m, cnd)
            m = jnp.max(m, axis=0, keepdims=True)
            tops.append(m)
            if i + 1 < N_TOP:
                cands = [jnp.where(cnd == m, NEG_INF, cnd) for cnd in cands]
        tau = 0.5 * (tops[PEER_TOPK - 1] + tops[PEER_TOPK])
        z = jnp.zeros_like(tau)
        for i in range(PEER_TOPK):
            z = z + jnp.exp(tops[i] - tops[0])
        theta = tau - s1
        cnt = jnp.zeros_like(s1)
        rank = jnp.zeros_like(s2)
        for j in range(PEER_TOPK):
            v2j = l2_ref[j:j + 1, :]
            cnt = jnp.where(v2j >= theta, j + 1.0, cnt)
            rank = jnp.where(v2j > s2, j + 1.0, rank)
        e1 = jnp.exp(s1 - m1)
        w2 = jnp.exp(s2 - m2) * (0.5 / z)
        outs = ((cnt_ref, cnt), (e1_ref, e1),
                (rank_ref, _words(rank.astype(BF16))), (w2_ref, _words(w2.astype(BF16))))
        for ref, val in outs:
            for lt in range(tm // LANES):
                ref[hd, lt] = val[:, lt * LANES:(lt + 1) * LANES]
        return carry

    lax.fori_loop(0, PEER_HEADS, head, 0)


def _route(x, gffn, wqt, k1, k2):
    n = x.shape[0]
    tm = ROUTE_TM
    def key_spec(rows):
        return pl.BlockSpec((PEER_HEADS, tm // LANES, rows, LANES), lambda i: (0, i, 0, 0))

    def key_shape(rows, dtype):
        return jax.ShapeDtypeStruct((PEER_HEADS, n // LANES, rows, LANES), dtype)

    return pl.pallas_call(
        _route_kernel,
        grid=(n // tm,),
        in_specs=[pl.BlockSpec((tm, D_MODEL), lambda i: (i, 0)),
                  _const_spec(gffn.shape), _const_spec(wqt.shape),
                  _const_spec(k1.shape), _const_spec(k2.shape)],
        out_specs=[pl.BlockSpec((tm // 2, D_MODEL), lambda i: (i, 0)),
                   key_spec(PEER_N_KEYS), key_spec(PEER_N_KEYS),
                   key_spec(PEER_N_KEYS // 2), key_spec(PEER_N_KEYS // 2)],
        out_shape=[jax.ShapeDtypeStruct((n // 2, D_MODEL), jnp.uint32),
                   key_shape(PEER_N_KEYS, F32), key_shape(PEER_N_KEYS, F32),
                   key_shape(PEER_N_KEYS // 2, jnp.uint32),
                   key_shape(PEER_N_KEYS // 2, jnp.uint32)],
        scratch_shapes=[
            pltpu.VMEM((PEER_HEADS * 2 * PEER_HALF, tm), F32),
            pltpu.VMEM((24, tm), F32),
            pltpu.VMEM((24, tm), F32),
        ],
        compiler_params=pltpu.CompilerParams(
            dimension_semantics=("arbitrary",), vmem_limit_bytes=VMEM_LIMIT),
        name="peer_route",
    )(x, gffn, wqt, k1, k2)


def _packed_row(row):
    tile = jnp.broadcast_to(row, (2 * SUBLANES, LANES)).astype(BF16)
    return jnp.tile(tile, (PEER_N_KEYS // tile.shape[0], 1))


def _peer_kernel(x_ref, hb_ref, cnt_ref, e1_ref, rank_ref, w2_ref, u_ref, vt_ref,
                 gfin_ref, o_ref, s_ref, a_ref, acc_ref, *, final_norm, e_tiles, n_tiles):
    g = pl.program_id(0)
    te = 2 * u_ref.shape[0]
    tm = 2 * hb_ref.shape[0]
    blocks = te // PEER_N_KEYS
    half_piece = PEER_PIECE // 2
    half_keys = PEER_N_KEYS // 2
    cur = g % 2
    prev = 1 - cur

    @pl.when(g == 0)
    def _():
        s_ref[...] = jnp.zeros_like(s_ref)
        a_ref[...] = jnp.zeros_like(a_ref)
        acc_ref[...] = jnp.zeros_like(acc_ref)

    tile2 = jnp.clip(g - 1, 0, n_tiles - 1)
    tile3 = jnp.clip(g - 2, 0, n_tiles - 1)
    i1_base = (tile2 % e_tiles) * blocks

    def piece(r, carry):
        rows = pl.ds(pl.multiple_of(r * PEER_PIECE, PEER_PIECE), PEER_PIECE)
        wrows = pl.ds(pl.multiple_of(r * half_piece, half_piece), half_piece)
        acc_ref[...] += _dot(_halves(vt_ref[r]), _halves(a_ref[cur, wrows, :]))
        for sub in range(0, PEER_PIECE // PEER_N_KEYS, GATE_GROUP):
            ibs = [r * (PEER_PIECE // PEER_N_KEYS) + sub + k for k in range(GATE_GROUP)]
            for lt in range(tm // LANES):
                cols = slice(lt * LANES, (lt + 1) * LANES)
                gates = [None] * GATE_GROUP
                for hd in range(PEER_HEADS):
                    rank = _halves(rank_ref[hd, lt])
                    w2 = _halves(w2_ref[hd, lt])
                    for k, ib in enumerate(ibs):
                        i1 = i1_base + ib
                        cnt = _packed_row(cnt_ref[hd, lt, pl.ds(i1, 1), :])
                        e1 = _packed_row(e1_ref[hd, lt, pl.ds(i1, 1), :])
                        term = jnp.where(rank < cnt, w2 * e1, jnp.zeros((), BF16))
                        gates[k] = term if gates[k] is None else gates[k] + term
                for k, ib in enumerate(ibs):
                    brow = pl.ds(pl.multiple_of(ib * PEER_N_KEYS, PEER_N_KEYS), PEER_N_KEYS)
                    bwrow = pl.ds(pl.multiple_of(ib * half_keys, half_keys), half_keys)
                    s = s_ref[prev, brow, cols]
                    act = s * (1.0 + lax.erf(s * INV_SQRT2))
                    a_ref[prev, bwrow, cols] = _words(act.astype(BF16) * gates[k])
        s_ref[cur, rows, :] = lax.dot_general(
            _halves(u_ref[wrows, :]), _halves(hb_ref[...]), (((1,), (1,)), ((), ())),
            preferred_element_type=F32)
        return carry

    lax.fori_loop(0, te // PEER_PIECE, piece, 0)

    @pl.when((g >= 2) & (tile3 % e_tiles == e_tiles - 1))
    def _():
        out = x_ref[...] + acc_ref[...].T
        if final_norm:
            out = _rms(out, gfin_ref[...])
            o_ref[...] = pltpu.einshape("(tb)d->btd", out, b=SUBLANES)
        else:
            o_ref[...] = out
        acc_ref[...] = jnp.zeros_like(acc_ref)


def _peer(x, hb_w, cnt, e1, rank_w, w2_w, u_w, vt_w, gfin, final_norm):
    n = x.shape[0]
    tm, te = PEER_TM, PEER_TE
    e_tiles = PEER_N_EXPERTS // te
    n_tiles = (n // tm) * e_tiles
    last = n_tiles - 1

    def t1(g):
        return jnp.minimum(g, last)

    def t2(g):
        return jnp.clip(g - 1, 0, last)

    def t3(g):
        return jnp.clip(g - 2, 0, last)

    def key_spec(rows):
        return pl.BlockSpec((PEER_HEADS, tm // LANES, rows, LANES),
                            lambda g: (0, t2(g) // e_tiles, 0, 0),
                            pipeline_mode=pl.Buffered(1))

    out_row_spec = pl.BlockSpec((tm, D_MODEL), lambda g: (t3(g) // e_tiles, 0))
    return pl.pallas_call(
        functools.partial(_peer_kernel, final_norm=final_norm, e_tiles=e_tiles,
                          n_tiles=n_tiles),
        grid=(n_tiles + 2,),
        in_specs=[pl.BlockSpec((tm, D_MODEL), lambda g: (t3(g) // e_tiles, 0),
                               pipeline_mode=pl.Buffered(1)),
                  pl.BlockSpec((tm // 2, D_MODEL), lambda g: (t1(g) // e_tiles, 0)),
                  key_spec(PEER_N_KEYS), key_spec(PEER_N_KEYS),
                  key_spec(PEER_N_KEYS // 2), key_spec(PEER_N_KEYS // 2),
                  pl.BlockSpec((te // 2, D_MODEL), lambda g: (t1(g) % e_tiles, 0)),
                  pl.BlockSpec((te // PEER_PIECE, D_MODEL // 2, PEER_PIECE),
                               lambda g: (t3(g) % e_tiles, 0, 0)),
                  pl.BlockSpec(gfin.shape, lambda g: (0, 0))],
        out_specs=(pl.BlockSpec((SUBLANES, tm // SUBLANES, D_MODEL),
                                lambda g: (0, t3(g) // e_tiles, 0))
                   if final_norm else out_row_spec),
        out_shape=(jax.ShapeDtypeStruct((SUBLANES, n // SUBLANES, D_MODEL), F32)
                   if final_norm else jax.ShapeDtypeStruct((n, D_MODEL), F32)),
        scratch_shapes=[
            pltpu.VMEM((2, te, tm), F32),
            pltpu.VMEM((2, te // 2, tm), jnp.uint32),
            pltpu.VMEM((D_MODEL, tm), F32),
        ],
        compiler_params=pltpu.CompilerParams(
            dimension_semantics=("arbitrary",), vmem_limit_bytes=VMEM_LIMIT),
        name="peer_dense",
    )(x, hb_w, cnt, e1, rank_w, w2_w, u_w, vt_w, gfin)


def _tables_kernel(u_ref, v_ref, uw_ref, vw_ref):
    uw_ref[...] = _words(u_ref[0].astype(BF16))
    vw_ref[0] = _words(v_ref[0].T.astype(BF16))


def _expert_tables(u, v, layer):
    _, e, d = u.shape
    rows = PEER_PIECE
    table_spec = pl.BlockSpec((1, rows, d), lambda i: (layer, i, 0))
    return pl.pallas_call(
        _tables_kernel,
        grid=(e // rows,),
        in_specs=[table_spec, table_spec],
        out_specs=[pl.BlockSpec((rows // 2, d), lambda i: (i, 0)),
                   pl.BlockSpec((1, d // 2, rows), lambda i: (i, 0, 0))],
        out_shape=[jax.ShapeDtypeStruct((e // 2, d), jnp.uint32),
                   jax.ShapeDtypeStruct((e // rows, d // 2, rows), jnp.uint32)],
        compiler_params=pltpu.CompilerParams(
            dimension_semantics=("arbitrary",), vmem_limit_bytes=VMEM_LIMIT),
        name="expert_tables",
    )(u, v)


def _block_diag(blocks):
    g, r, c = blocks.shape
    eye = jnp.eye(g, dtype=blocks.dtype)
    return (blocks[:, :, None, :] * eye[:, None, :, None]).reshape(g * r, g * c)


def kernel(x, norm_mix, w_in, a_re, a_im, log_dt, b_re, b_im, c_re, c_im, d_skip, w_glu, b_glu, w_pool, pool_scale, g_out_ssm, g_out_pool, w_out, norm_ffn, w_q, k1, k2, u_experts, v_experts, norm_final):
    bsz, seq, dm = x.shape
    assert (bsz, dm) == (SUBLANES, D_MODEL) and seq % (MIX_ROWS // SUBLANES) == 0
    depth = w_in.shape[0]
    n = bsz * seq
    row = lambda a: a.reshape(1, -1).astype(F32)

    abr, abi, btr, bti = _discretise(a_re, a_im, log_dt, b_re, b_im)
    xt = x
    gfin = row(norm_final)

    for i in range(depth):
        hg = SSM_GROUPS // 2
        bmat = jnp.stack([_block_diag(b[lo:lo + hg]) for b in (btr[i], bti[i])
                          for lo in (0, hg)]).astype(BF16)
        cre, cim = (jnp.stack([_block_diag(jnp.transpose(c[lo:lo + hg], (0, 2, 1)))
                               for lo in (0, hg)]).astype(BF16)
                    for c in (c_re[i], c_im[i]))
        xt = _mixer(
            xt, row(norm_mix[i]), w_in[i].astype(BF16), bmat,
            abr[i].reshape(1, STATE_W), abi[i].reshape(1, STATE_W), cre, cim,
            row(d_skip[i]), w_glu[i].astype(BF16), row(b_glu[i]),
            _block_diag(w_pool[i]).astype(BF16), row(pool_scale[i]),
            row(g_out_ssm[i]), row(g_out_pool[i]), w_out[i].astype(BF16))
        hb, cnt, e1, rank, w2 = _route(
            xt, row(norm_ffn[i]), jnp.transpose(w_q[i]).astype(BF16),
            k1[i].astype(BF16), k2[i].astype(BF16))
        u_w, vt_w = _expert_tables(u_experts, v_experts, i)
        xt = _peer(xt, hb, cnt, e1, rank, w2, u_w, vt_w, gfin,
                   final_norm=(i == depth - 1))
    return xt
```

```python
import functools
import math

import jax
import jax.numpy as jnp
from jax import lax
from jax.experimental import pallas as pl
from jax.experimental.pallas import tpu as pltpu

F32 = jnp.float32
BF16 = jnp.bfloat16

D_MODEL = 1024
SSM_WIDTH = 512
POOL_WIDTH = 512
SSM_GROUP = 16
SSM_GROUPS = 32
SSM_STATE = 64
STATE_W = SSM_GROUPS * SSM_STATE
HALF_SSM = SSM_WIDTH // 2
HALF_STATE = STATE_W // 2
POOL_WINDOWS = (2, 4, 8, 16)
POOL_GROUP_WIDTH = 128
PEER_HEADS = 8
PEER_N_KEYS = 128
PEER_N_EXPERTS = PEER_N_KEYS * PEER_N_KEYS
PEER_HALF = 128
PEER_TOPK = 16
RMS_EPS = 1e-6

SUBLANES = 8
LANES = 128
MXU_DEPTH = 256
MIX_ROWS = 512
POOL_HIST_ROWS = 128
ROUTE_TM = 512
PEER_TM = 512
PEER_TE = 2048
PEER_PIECE = 1024
GATE_GROUP = 1
VMEM_LIMIT = 60 * 1024 * 1024

NEG_INF = float("-inf")
INV_SQRT2 = 0.7071067811865476


def _rms(x, g):
    return x * lax.rsqrt(jnp.mean(x * x, axis=-1, keepdims=True) + RMS_EPS) * g


def _gelu(x):
    return 0.5 * x * (1.0 + lax.erf(x * INV_SQRT2))


def _dot(a, b):
    return jnp.dot(a, b, preferred_element_type=F32)


def _words(x):
    return pltpu.bitcast(x, jnp.uint32)


def _halves(w):
    return pltpu.bitcast(w, BF16)


def _disc_kernel(are_ref, aim_ref, ldt_ref, bre_ref, bim_ref,
                 abr_ref, abi_ref, btr_ref, bti_ref):
    lam_re = are_ref[...]
    lam_im = aim_ref[...]
    dt = jnp.exp(ldt_ref[...])
    decay = jnp.exp(lam_re * dt)
    abar_re = decay * jnp.cos(lam_im * dt)
    abar_im = decay * jnp.sin(lam_im * dt)
    inv_den = 1.0 / (lam_re * lam_re + lam_im * lam_im)
    num_re = abar_re - 1.0
    zoh_re = (num_re * lam_re + abar_im * lam_im) * inv_den
    zoh_im = (abar_im * lam_re - num_re * lam_im) * inv_den
    b_re = bre_ref[...]
    b_im = bim_ref[...]
    abr_ref[...] = abar_re
    abi_ref[...] = abar_im
    btr_ref[...] = zoh_re * b_re - zoh_im * b_im
    bti_ref[...] = zoh_re * b_im + zoh_im * b_re


def _discretise(a_re, a_im, log_dt, b_re, b_im):
    nl = a_re.shape[0]
    rows = nl * SSM_GROUPS * SSM_GROUP
    shp = (nl, SSM_GROUPS, SSM_GROUP, SSM_STATE)

    def rep(a):
        return jnp.broadcast_to(a[:, :, None, :], shp).reshape(rows, SSM_STATE)

    ldt = jnp.broadcast_to(log_dt[:, :, None, None], shp).reshape(rows, SSM_STATE)
    bre = jnp.transpose(b_re, (0, 1, 3, 2)).reshape(rows, SSM_STATE)
    bim = jnp.transpose(b_im, (0, 1, 3, 2)).reshape(rows, SSM_STATE)
    out = jax.ShapeDtypeStruct((rows, SSM_STATE), F32)
    abr, abi, btr, bti = pl.pallas_call(
        _disc_kernel, out_shape=(out, out, out, out), name="s5_discretise",
    )(rep(a_re), rep(a_im), ldt, bre, bim)
    abr = abr.reshape(shp)[:, :, 0, :].reshape(nl, STATE_W)
    abi = abi.reshape(shp)[:, :, 0, :].reshape(nl, STATE_W)
    return abr, abi, btr.reshape(shp), bti.reshape(shp)


def _mixer_kernel(x_ref, gmix_ref, win_ref, bmat_ref, are_ref, aim_ref,
                  cre_ref, cim_ref, dskip_ref, wglu_ref, bglu_ref, wpool_ref,
                  pscale_ref, gssm_ref, gpool_ref, wout_ref, o_ref,
                  st_ref, sre_ref, sim_ref, ext_ref, *, batch_major_in):
    c = pl.program_id(0)
    rows = o_ref.shape[0]
    steps = rows // SUBLANES

    @pl.when(c == 0)
    def _():
        sre_ref[...] = jnp.zeros_like(sre_ref)
        sim_ref[...] = jnp.zeros_like(sim_ref)
        ext_ref[0:POOL_HIST_ROWS, :] = jnp.zeros((POOL_HIST_ROWS, POOL_WIDTH), F32)

    if batch_major_in:
        xr = pltpu.einshape("btd->(tb)d", x_ref[...])
    else:
        xr = x_ref[...]
    hn = _rms(xr, gmix_ref[...])
    proj = _dot(hn.astype(BF16), win_ref[...])
    u_ssm = proj[:, :SSM_WIDTH]
    u_pool = proj[:, SSM_WIDTH:]

    u_b = u_ssm.astype(BF16)
    for part in range(2):
        for hf in range(2):
            lo = part * STATE_W + hf * HALF_STATE
            st_ref[:, lo:lo + HALF_STATE] = _dot(
                u_b[:, hf * HALF_SSM:(hf + 1) * HALF_SSM], bmat_ref[2 * part + hf])
    a_re = jnp.broadcast_to(are_ref[...], (SUBLANES, STATE_W))
    a_im = jnp.broadcast_to(aim_ref[...], (SUBLANES, STATE_W))

    def step(t, carry):
        s_re, s_im = carry
        r = pl.multiple_of(t * SUBLANES, SUBLANES)
        in_re = st_ref[pl.ds(r, SUBLANES), 0:STATE_W]
        in_im = st_ref[pl.ds(r, SUBLANES), STATE_W:2 * STATE_W]
        n_re = a_re * s_re - a_im * s_im + in_re
        n_im = a_re * s_im + a_im * s_re + in_im
        st_ref[pl.ds(r, SUBLANES), 0:STATE_W] = n_re
        st_ref[pl.ds(r, SUBLANES), STATE_W:2 * STATE_W] = n_im
        return n_re, n_im

    s_re, s_im = lax.fori_loop(0, steps, step, (sre_ref[...], sim_ref[...]))
    sre_ref[...] = s_re
    sim_ref[...] = s_im

    y = []
    for hf in range(2):
        lo = hf * HALF_STATE
        y.append(_dot(st_ref[:, lo:lo + HALF_STATE].astype(BF16), cre_ref[hf])
                 - _dot(st_ref[:, STATE_W + lo:STATE_W + lo + HALF_STATE].astype(BF16),
                        cim_ref[hf]))
    y = jnp.concatenate(y, axis=1) + dskip_ref[...] * u_ssm
    y = _gelu(y)
    y = y * jax.nn.sigmoid(_dot(y.astype(BF16), wglu_ref[...]) + bglu_ref[...])
    ssm_n = _rms(y, gssm_ref[...])

    ext_ref[POOL_HIST_ROWS:, :] = u_pool
    t_idx = c * steps + jnp.right_shift(
        lax.broadcasted_iota(jnp.int32, (rows, POOL_GROUP_WIDTH), 0), 3)
    pooled = []
    for gi, win in enumerate(POOL_WINDOWS):
        lo = gi * POOL_GROUP_WIDTH
        hi = lo + POOL_GROUP_WIDTH
        acc = ext_ref[POOL_HIST_ROWS:, lo:hi]
        for k in range(1, win):
            off = POOL_HIST_ROWS - SUBLANES * k
            acc = acc + ext_ref[off:off + rows, lo:hi]
        count = jnp.minimum(t_idx + 1, win).astype(F32)
        pooled.append(acc / count - ext_ref[POOL_HIST_ROWS:, lo:hi])
    ext_ref[0:POOL_HIST_ROWS, :] = ext_ref[rows:rows + POOL_HIST_ROWS, :]
    pooled = jnp.concatenate(pooled, axis=1)
    y_pool = _dot(pooled.astype(BF16), wpool_ref[...]) * pscale_ref[...]
    pool_n = _rms(y_pool, gpool_ref[...])

    res = (_dot(ssm_n.astype(BF16), wout_ref[0:SSM_WIDTH, :])
           + _dot(pool_n.astype(BF16), wout_ref[SSM_WIDTH:, :]))
    o_ref[...] = xr + res


def _const_spec(shape):
    zeros = (0,) * len(shape)
    return pl.BlockSpec(shape, lambda *_: zeros, pipeline_mode=pl.Buffered(1))


def _mixer(x, gmix, win, bmat, are, aim, cre, cim, dskip, wglu, bglu, wpool,
           pscale, gssm, gpool, wout):
    batch_major_in = x.ndim == 3
    n = x.shape[0] * x.shape[1] if batch_major_in else x.shape[0]
    consts = (gmix, win, bmat, are, aim, cre, cim, dskip, wglu, bglu, wpool,
              pscale, gssm, gpool, wout)
    row_spec = pl.BlockSpec((MIX_ROWS, D_MODEL), lambda c: (c, 0))
    x_spec = (pl.BlockSpec((SUBLANES, MIX_ROWS // SUBLANES, D_MODEL), lambda c: (0, c, 0))
              if batch_major_in else row_spec)
    return pl.pallas_call(
        functools.partial(_mixer_kernel, batch_major_in=batch_major_in),
        grid=(n // MIX_ROWS,),
        in_specs=[x_spec] + [_const_spec(a.shape) for a in consts],
        out_specs=row_spec,
        out_shape=jax.ShapeDtypeStruct((n, D_MODEL), F32),
        scratch_shapes=[
            pltpu.VMEM((MIX_ROWS, 2 * STATE_W), F32),
            pltpu.VMEM((SUBLANES, STATE_W), F32),
            pltpu.VMEM((SUBLANES, STATE_W), F32),
            pltpu.VMEM((POOL_HIST_ROWS + MIX_ROWS, POOL_WIDTH), F32),
        ],
        compiler_params=pltpu.CompilerParams(
            dimension_semantics=("arbitrary",), vmem_limit_bytes=VMEM_LIMIT),
        name="mixer",
    )(x, *consts)


def _sort16_pairs():
    n, pairs, p = 16, [], 1
    while p < n:
        k = p
        while k >= 1:
            for j in range(k % p, n - k, 2 * k):
                for i in range(min(k, n - j - k)):
                    if (i + j) // (2 * p) == (i + j + k) // (2 * p):
                        pairs.append((i + j, i + j + k))
            k //= 2
        p *= 2
    return pairs


_SORT16 = _sort16_pairs()
N_TOP = PEER_TOPK + 1


def _top_sorted(s, out_ref):
    v = [s[SUBLANES * k:SUBLANES * (k + 1), :] for k in range(16)]
    for i, j in _SORT16:
        hi = jnp.maximum(v[i], v[j])
        lo = jnp.minimum(v[i], v[j])
        v[i], v[j] = hi, lo
    for i in range(N_TOP):
        head = v[0]
        m = jnp.max(head, axis=0, keepdims=True)
        out_ref[i:i + 1, :] = m
        if i + 1 < N_TOP:
            pop = head == m
            depth = N_TOP - i
            v = [jnp.where(pop, v[k + 1] if k + 1 < len(v) else NEG_INF, v[k])
                 for k in range(depth - 1)]


def _route_kernel(x_ref, gffn_ref, wqt_ref, k1_ref, k2_ref,
                  hb_ref, cnt_ref, e1_ref, rank_ref, w2_ref,
                  qt_ref, l1_ref, l2_ref):
    tm = x_ref.shape[0]
    h = _rms(x_ref[...], gffn_ref[...])
    hbt = h.T.astype(BF16)
    hb_ref[...] = _words(hbt)
    qt_ref[...] = _dot(wqt_ref[...], hbt)
    row = lax.broadcasted_iota(jnp.int32, (SUBLANES, tm), 0)

    def head(hd, carry):
        base = pl.multiple_of(hd * 2 * PEER_HALF, 2 * PEER_HALF)
        q1 = qt_ref[pl.ds(base, PEER_HALF), :].astype(BF16)
        q2 = qt_ref[pl.ds(base + PEER_HALF, PEER_HALF), :].astype(BF16)
        s1 = _dot(k1_ref[...], q1)
        s2 = _dot(k2_ref[...], q2)
        _top_sorted(s1, l1_ref)
        _top_sorted(s2, l2_ref)
        m1 = l1_ref[0:1, :]
        m2 = l2_ref[0:1, :]
        a = l1_ref[1:9, :]
        b = l2_ref[1:9, :]
        cands = [
            m1 + l2_ref[0:8, :],
            m1 + l2_ref[8:16, :],
            m2 + a,
            m2 + l1_ref[9:17, :],
            jnp.where(row < 7, l1_ref[1:2, :] + b, NEG_INF),
            jnp.where(row < 4, l1_ref[2:3, :] + b, NEG_INF),
            jnp.where(row < 3, l1_ref[3:4, :] + b, NEG_INF),
            jnp.where(row < 2, l1_ref[4:5, :] + b, NEG_INF),
            jnp.where((row >= 4) & (row < 7), l2_ref[1:2, :] + a, NEG_INF),
            jnp.where(row == 7, m1 + l2_ref[9:17, :], NEG_INF),
        ]
        tops = []
        for i in range(N_TOP):
            m = cands[0]
            for cnd in cands[1:]:
                m = jnp.maximum(m, cnd)
            m = jnp.max(m, axis=0, keepdims=True)
            tops.append(m)
            if i + 1 < N_TOP:
                cands = [jnp.where(cnd == m, NEG_INF, cnd) for cnd in cands]
        tau = 0.5 * (tops[PEER_TOPK - 1] + tops[PEER_TOPK])
        z = jnp.zeros_like(tau)
        for i in range(PEER_TOPK):
            z = z + jnp.exp(tops[i] - tops[0])
        theta = tau - s1
        cnt = jnp.zeros_like(s1)
        rank = jnp.zeros_like(s2)
        for j in range(PEER_TOPK):
            v2j = l2_ref[j:j + 1, :]
            cnt = jnp.where(v2j >= theta, j + 1.0, cnt)
            rank = jnp.where(v2j > s2, j + 1.0, rank)
        e1 = jnp.exp(s1 - m1)
        w2 = jnp.exp(s2 - m2) * (0.5 / z)
        outs = ((cnt_ref, cnt), (e1_ref, e1),
                (rank_ref, _words(rank.astype(BF16))), (w2_ref, _words(w2.astype(BF16))))
        for ref, val in outs:
            for lt in range(tm // LANES):
                ref[hd, lt] = val[:, lt * LANES:(lt + 1) * LANES]
        return carry

    lax.fori_loop(0, PEER_HEADS, head, 0)


def _route(x, gffn, wqt, k1, k2):
    n = x.shape[0]
    tm = ROUTE_TM
    def key_spec(rows):
        return pl.BlockSpec((PEER_HEADS, tm // LANES, rows, LANES), lambda i: (0, i, 0, 0))

    def key_shape(rows, dtype):
        return jax.ShapeDtypeStruct((PEER_HEADS, n // LANES, rows, LANES), dtype)

    return pl.pallas_call(
        _route_kernel,
        grid=(n // tm,),
        in_specs=[pl.BlockSpec((tm, D_MODEL), lambda i: (i, 0)),
                  _const_spec(gffn.shape), _const_spec(wqt.shape),
                  _const_spec(k1.shape), _const_spec(k2.shape)],
        out_specs=[pl.BlockSpec((D_MODEL // 2, tm), lambda i: (0, i)),
                   key_spec(PEER_N_KEYS), key_spec(PEER_N_KEYS),
                   key_spec(PEER_N_KEYS // 2), key_spec(PEER_N_KEYS // 2)],
        out_shape=[jax.ShapeDtypeStruct((D_MODEL // 2, n), jnp.uint32),
                   key_shape(PEER_N_KEYS, F32), key_shape(PEER_N_KEYS, F32),
                   key_shape(PEER_N_KEYS // 2, jnp.uint32),
                   key_shape(PEER_N_KEYS // 2, jnp.uint32)],
        scratch_shapes=[
            pltpu.VMEM((PEER_HEADS * 2 * PEER_HALF, tm), F32),
            pltpu.VMEM((24, tm), F32),
            pltpu.VMEM((24, tm), F32),
        ],
        compiler_params=pltpu.CompilerParams(
            dimension_semantics=("arbitrary",), vmem_limit_bytes=VMEM_LIMIT),
        name="peer_route",
    )(x, gffn, wqt, k1, k2)


def _packed_row(row):
    tile = jnp.broadcast_to(row, (2 * SUBLANES, LANES)).astype(BF16)
    return jnp.tile(tile, (PEER_N_KEYS // tile.shape[0], 1))


def _peer_kernel(x_ref, hb_ref, cnt_ref, e1_ref, rank_ref, w2_ref, u_ref, vt_ref,
                 gfin_ref, o_ref, s_ref, a_ref, acc_ref, *, final_norm, e_tiles, n_tiles):
    g = pl.program_id(0)
    te = 2 * u_ref.shape[0]
    tm = hb_ref.shape[1]
    blocks = te // PEER_N_KEYS
    half_piece = PEER_PIECE // 2
    half_keys = PEER_N_KEYS // 2
    cur = g % 2
    prev = 1 - cur

    @pl.when(g == 0)
    def _():
        s_ref[...] = jnp.zeros_like(s_ref)
        a_ref[...] = jnp.zeros_like(a_ref)
        acc_ref[...] = jnp.zeros_like(acc_ref)

    tile2 = jnp.clip(g - 1, 0, n_tiles - 1)
    tile3 = jnp.clip(g - 2, 0, n_tiles - 1)
    i1_base = (tile2 % e_tiles) * blocks

    def piece(r, carry):
        rows = pl.ds(pl.multiple_of(r * PEER_PIECE, PEER_PIECE), PEER_PIECE)
        wrows = pl.ds(pl.multiple_of(r * half_piece, half_piece), half_piece)
        acc_ref[...] += _dot(_halves(vt_ref[r]), _halves(a_ref[cur, wrows, :]))
        for sub in range(0, PEER_PIECE // PEER_N_KEYS, GATE_GROUP):
            ibs = [r * (PEER_PIECE // PEER_N_KEYS) + sub + k for k in range(GATE_GROUP)]
            for lt in range(tm // LANES):
                cols = slice(lt * LANES, (lt + 1) * LANES)
                gates = [None] * GATE_GROUP
                for hd in range(PEER_HEADS):
                    rank = _halves(rank_ref[hd, lt])
                    w2 = _halves(w2_ref[hd, lt])
                    for k, ib in enumerate(ibs):
                        i1 = i1_base + ib
                        cnt = _packed_row(cnt_ref[hd, lt, pl.ds(i1, 1), :])
                        e1 = _packed_row(e1_ref[hd, lt, pl.ds(i1, 1), :])
                        term = jnp.where(rank < cnt, w2 * e1, jnp.zeros((), BF16))
                        gates[k] = term if gates[k] is None else gates[k] + term
                for k, ib in enumerate(ibs):
                    brow = pl.ds(pl.multiple_of(ib * PEER_N_KEYS, PEER_N_KEYS), PEER_N_KEYS)
                    bwrow = pl.ds(pl.multiple_of(ib * half_keys, half_keys), half_keys)
                    s = s_ref[prev, brow, cols]
                    act = s * (1.0 + lax.erf(s * INV_SQRT2))
                    a_ref[prev, bwrow, cols] = _words(act.astype(BF16) * gates[k])
        s_ref[cur, rows, :] = _dot(_halves(u_ref[wrows, :]), _halves(hb_ref[...]))
        return carry

    lax.fori_loop(0, te // PEER_PIECE, piece, 0)

    @pl.when((g >= 2) & (tile3 % e_tiles == e_tiles - 1))
    def _():
        out = x_ref[...] + acc_ref[...].T
        if final_norm:
            out = _rms(out, gfin_ref[...])
            o_ref[...] = pltpu.einshape("(tb)d->btd", out, b=SUBLANES)
        else:
            o_ref[...] = out
        acc_ref[...] = jnp.zeros_like(acc_ref)


def _peer(x, hb_w, cnt, e1, rank_w, w2_w, u_w, vt_w, gfin, final_norm):
    n = x.shape[0]
    tm, te = PEER_TM, PEER_TE
    e_tiles = PEER_N_EXPERTS // te
    n_tiles = (n // tm) * e_tiles
    last = n_tiles - 1

    def t1(g):
        return jnp.minimum(g, last)

    def t2(g):
        return jnp.clip(g - 1, 0, last)

    def t3(g):
        return jnp.clip(g - 2, 0, last)

    def key_spec(rows):
        return pl.BlockSpec((PEER_HEADS, tm // LANES, rows, LANES),
                            lambda g: (0, t2(g) // e_tiles, 0, 0),
                            pipeline_mode=pl.Buffered(1))

    out_row_spec = pl.BlockSpec((tm, D_MODEL), lambda g: (t3(g) // e_tiles, 0))
    return pl.pallas_call(
        functools.partial(_peer_kernel, final_norm=final_norm, e_tiles=e_tiles,
                          n_tiles=n_tiles),
        grid=(n_tiles + 2,),
        in_specs=[pl.BlockSpec((tm, D_MODEL), lambda g: (t3(g) // e_tiles, 0),
                               pipeline_mode=pl.Buffered(1)),
                  pl.BlockSpec((D_MODEL // 2, tm), lambda g: (0, t1(g) // e_tiles)),
                  key_spec(PEER_N_KEYS), key_spec(PEER_N_KEYS),
                  key_spec(PEER_N_KEYS // 2), key_spec(PEER_N_KEYS // 2),
                  pl.BlockSpec((te // 2, D_MODEL), lambda g: (t1(g) % e_tiles, 0)),
                  pl.BlockSpec((te // PEER_PIECE, D_MODEL // 2, PEER_PIECE),
                               lambda g: (t3(g) % e_tiles, 0, 0)),
                  pl.BlockSpec(gfin.shape, lambda g: (0, 0))],
        out_specs=(pl.BlockSpec((SUBLANES, tm // SUBLANES, D_MODEL),
                                lambda g: (0, t3(g) // e_tiles, 0))
                   if final_norm else out_row_spec),
        out_shape=(jax.ShapeDtypeStruct((SUBLANES, n // SUBLANES, D_MODEL), F32)
                   if final_norm else jax.ShapeDtypeStruct((n, D_MODEL), F32)),
        scratch_shapes=[
            pltpu.VMEM((2, te, tm), F32),
            pltpu.VMEM((2, te // 2, tm), jnp.uint32),
            pltpu.VMEM((D_MODEL, tm), F32),
        ],
        compiler_params=pltpu.CompilerParams(
            dimension_semantics=("arbitrary",), vmem_limit_bytes=VMEM_LIMIT),
        name="peer_dense",
    )(x, hb_w, cnt, e1, rank_w, w2_w, u_w, vt_w, gfin)


def _tables_kernel(u_ref, v_ref, uw_ref, vw_ref):
    uw_ref[...] = _words(u_ref[0].astype(BF16))
    vw_ref[0] = _words(v_ref[0].T.astype(BF16))


def _expert_tables(u, v, layer):
    _, e, d = u.shape
    rows = PEER_PIECE
    table_spec = pl.BlockSpec((1, rows, d), lambda i: (layer, i, 0))
    return pl.pallas_call(
        _tables_kernel,
        grid=(e // rows,),
        in_specs=[table_spec, table_spec],
        out_specs=[pl.BlockSpec((rows // 2, d), lambda i: (i, 0)),
                   pl.BlockSpec((1, d // 2, rows), lambda i: (i, 0, 0))],
        out_shape=[jax.ShapeDtypeStruct((e // 2, d), jnp.uint32),
                   jax.ShapeDtypeStruct((e // rows, d // 2, rows), jnp.uint32)],
        compiler_params=pltpu.CompilerParams(
            dimension_semantics=("arbitrary",), vmem_limit_bytes=VMEM_LIMIT),
        name="expert_tables",
    )(u, v)


def _block_diag(blocks):
    g, r, c = blocks.shape
    eye = jnp.eye(g, dtype=blocks.dtype)
    return (blocks[:, :, None, :] * eye[:, None, :, None]).reshape(g * r, g * c)


def kernel(x, norm_mix, w_in, a_re, a_im, log_dt, b_re, b_im, c_re, c_im, d_skip, w_glu, b_glu, w_pool, pool_scale, g_out_ssm, g_out_pool, w_out, norm_ffn, w_q, k1, k2, u_experts, v_experts, norm_final):
    bsz, seq, dm = x.shape
    assert (bsz, dm) == (SUBLANES, D_MODEL) and seq % (MIX_ROWS // SUBLANES) == 0
    depth = w_in.shape[0]
    n = bsz * seq
    row = lambda a: a.reshape(1, -1).astype(F32)

    abr, abi, btr, bti = _discretise(a_re, a_im, log_dt, b_re, b_im)
    xt = x
    gfin = row(norm_final)

    for i in range(depth):
        hg = SSM_GROUPS // 2
        bmat = jnp.stack([_block_diag(b[lo:lo + hg]) for b in (btr[i], bti[i])
                          for lo in (0, hg)]).astype(BF16)
        cre, cim = (jnp.stack([_block_diag(jnp.transpose(c[lo:lo + hg], (0, 2, 1)))
                               for lo in (0, hg)]).astype(BF16)
                    for c in (c_re[i], c_im[i]))
        xt = _mixer(
            xt, row(norm_mix[i]), w_in[i].astype(BF16), bmat,
            abr[i].reshape(1, STATE_W), abi[i].reshape(1, STATE_W), cre, cim,
            row(d_skip[i]), w_glu[i].astype(BF16), row(b_glu[i]),
            _block_diag(w_pool[i]).astype(BF16), row(pool_scale[i]),
            row(g_out_ssm[i]), row(g_out_pool[i]), w_out[i].astype(BF16))
        hb, cnt, e1, rank, w2 = _route(
            xt, row(norm_ffn[i]), jnp.transpose(w_q[i]).astype(BF16),
            k1[i].astype(BF16), k2[i].astype(BF16))
        u_w, vt_w = _expert_tables(u_experts, v_experts, i)
        xt = _peer(xt, hb, cnt, e1, rank, w2, u_w, vt_w, gfin,
                   final_norm=(i == depth - 1))
    return xt
```

```python
import functools
import itertools
import math

import jax
import jax.numpy as jnp
from jax import lax
from jax.experimental import pallas as pl
from jax.experimental.pallas import tpu as pltpu

F32 = jnp.float32
BF16 = jnp.bfloat16

D_MODEL = 1024
SSM_WIDTH = 512
POOL_WIDTH = 512
SSM_GROUP = 16
SSM_GROUPS = 32
SSM_STATE = 64
STATE_W = SSM_GROUPS * SSM_STATE
HALF_SSM = SSM_WIDTH // 2
HALF_STATE = STATE_W // 2
POOL_WINDOWS = (2, 4, 8, 16)
POOL_GROUP_WIDTH = 128
PEER_HEADS = 8
PEER_N_KEYS = 128
PEER_N_EXPERTS = PEER_N_KEYS * PEER_N_KEYS
PEER_HALF = 128
PEER_TOPK = 16
RMS_EPS = 1e-6

SUBLANES = 8
LANES = 128
MXU_DEPTH = 256
MIX_ROWS = 512
POOL_HIST_ROWS = 128
ROUTE_TM = 512
PEER_TM = 512
PEER_TE = 2048
PEER_PIECE = 1024
MXU_CHUNKS = 1
GATE_ROWS = 64
GATE_GROUP = 2
VMEM_LIMIT = 60 * 1024 * 1024

NEG_INF = float("-inf")
INV_SQRT2 = 0.7071067811865476


def _rms(x, g):
    return x * lax.rsqrt(jnp.mean(x * x, axis=-1, keepdims=True) + RMS_EPS) * g


def _gelu(x):
    return 0.5 * x * (1.0 + lax.erf(x * INV_SQRT2))


def _dot(a, b):
    return jnp.dot(a, b, preferred_element_type=F32)


def _words(x):
    return pltpu.bitcast(x, jnp.uint32)


def _halves(w):
    return pltpu.bitcast(w, BF16)


def _disc_kernel(are_ref, aim_ref, ldt_ref, bre_ref, bim_ref,
                 abr_ref, abi_ref, btr_ref, bti_ref):
    lam_re = are_ref[...]
    lam_im = aim_ref[...]
    dt = jnp.exp(ldt_ref[...])
    decay = jnp.exp(lam_re * dt)
    abar_re = decay * jnp.cos(lam_im * dt)
    abar_im = decay * jnp.sin(lam_im * dt)
    inv_den = 1.0 / (lam_re * lam_re + lam_im * lam_im)
    num_re = abar_re - 1.0
    zoh_re = (num_re * lam_re + abar_im * lam_im) * inv_den
    zoh_im = (abar_im * lam_re - num_re * lam_im) * inv_den
    b_re = bre_ref[...]
    b_im = bim_ref[...]
    abr_ref[...] = abar_re
    abi_ref[...] = abar_im
    btr_ref[...] = zoh_re * b_re - zoh_im * b_im
    bti_ref[...] = zoh_re * b_im + zoh_im * b_re


def _discretise(a_re, a_im, log_dt, b_re, b_im):
    nl = a_re.shape[0]
    rows = nl * SSM_GROUPS * SSM_GROUP
    shp = (nl, SSM_GROUPS, SSM_GROUP, SSM_STATE)

    def rep(a):
        return jnp.broadcast_to(a[:, :, None, :], shp).reshape(rows, SSM_STATE)

    ldt = jnp.broadcast_to(log_dt[:, :, None, None], shp).reshape(rows, SSM_STATE)
    bre = jnp.transpose(b_re, (0, 1, 3, 2)).reshape(rows, SSM_STATE)
    bim = jnp.transpose(b_im, (0, 1, 3, 2)).reshape(rows, SSM_STATE)
    out = jax.ShapeDtypeStruct((rows, SSM_STATE), F32)
    abr, abi, btr, bti = pl.pallas_call(
        _disc_kernel, out_shape=(out, out, out, out), name="s5_discretise",
    )(rep(a_re), rep(a_im), ldt, bre, bim)
    abr = abr.reshape(shp)[:, :, 0, :].reshape(nl, STATE_W)
    abi = abi.reshape(shp)[:, :, 0, :].reshape(nl, STATE_W)
    return abr, abi, btr.reshape(shp), bti.reshape(shp)


def _mixer_kernel(x_ref, gmix_ref, win_ref, bmat_ref, are_ref, aim_ref,
                  cre_ref, cim_ref, dskip_ref, wglu_ref, bglu_ref, wpool_ref,
                  pscale_ref, gssm_ref, gpool_ref, wout_ref, o_ref,
                  st_ref, sre_ref, sim_ref, ext_ref, *, batch_major_in):
    c = pl.program_id(0)
    rows = o_ref.shape[0]
    steps = rows // SUBLANES

    @pl.when(c == 0)
    def _():
        sre_ref[...] = jnp.zeros_like(sre_ref)
        sim_ref[...] = jnp.zeros_like(sim_ref)
        ext_ref[0:POOL_HIST_ROWS, :] = jnp.zeros((POOL_HIST_ROWS, POOL_WIDTH), F32)

    if batch_major_in:
        xr = pltpu.einshape("btd->(tb)d", x_ref[...])
    else:
        xr = x_ref[...]
    hn = _rms(xr, gmix_ref[...])
    proj = _dot(hn.astype(BF16), win_ref[...])
    u_ssm = proj[:, :SSM_WIDTH]
    u_pool = proj[:, SSM_WIDTH:]

    u_b = u_ssm.astype(BF16)
    for part in range(2):
        for hf in range(2):
            lo = part * STATE_W + hf * HALF_STATE
            st_ref[:, lo:lo + HALF_STATE] = _dot(
                u_b[:, hf * HALF_SSM:(hf + 1) * HALF_SSM], bmat_ref[2 * part + hf])
    a_re = jnp.broadcast_to(are_ref[...], (SUBLANES, STATE_W))
    a_im = jnp.broadcast_to(aim_ref[...], (SUBLANES, STATE_W))

    def step(t, carry):
        s_re, s_im = carry
        r = pl.multiple_of(t * SUBLANES, SUBLANES)
        in_re = st_ref[pl.ds(r, SUBLANES), 0:STATE_W]
        in_im = st_ref[pl.ds(r, SUBLANES), STATE_W:2 * STATE_W]
        n_re = a_re * s_re - a_im * s_im + in_re
        n_im = a_re * s_im + a_im * s_re + in_im
        st_ref[pl.ds(r, SUBLANES), 0:STATE_W] = n_re
        st_ref[pl.ds(r, SUBLANES), STATE_W:2 * STATE_W] = n_im
        return n_re, n_im

    s_re, s_im = lax.fori_loop(0, steps, step, (sre_ref[...], sim_ref[...]))
    sre_ref[...] = s_re
    sim_ref[...] = s_im

    y = []
    for hf in range(2):
        lo = hf * HALF_STATE
        y.append(_dot(st_ref[:, lo:lo + HALF_STATE].astype(BF16), cre_ref[hf])
                 - _dot(st_ref[:, STATE_W + lo:STATE_W + lo + HALF_STATE].astype(BF16),
                        cim_ref[hf]))
    y = jnp.concatenate(y, axis=1) + dskip_ref[...] * u_ssm
    y = _gelu(y)
    y = y * jax.nn.sigmoid(_dot(y.astype(BF16), wglu_ref[...]) + bglu_ref[...])
    ssm_n = _rms(y, gssm_ref[...])

    ext_ref[POOL_HIST_ROWS:, :] = u_pool
    t_idx = c * steps + jnp.right_shift(
        lax.broadcasted_iota(jnp.int32, (rows, POOL_GROUP_WIDTH), 0), 3)
    pooled = []
    for gi, win in enumerate(POOL_WINDOWS):
        lo = gi * POOL_GROUP_WIDTH
        hi = lo + POOL_GROUP_WIDTH
        acc = ext_ref[POOL_HIST_ROWS:, lo:hi]
        for k in range(1, win):
            off = POOL_HIST_ROWS - SUBLANES * k
            acc = acc + ext_ref[off:off + rows, lo:hi]
        count = jnp.minimum(t_idx + 1, win).astype(F32)
        pooled.append(acc / count - ext_ref[POOL_HIST_ROWS:, lo:hi])
    ext_ref[0:POOL_HIST_ROWS, :] = ext_ref[rows:rows + POOL_HIST_ROWS, :]
    pooled = jnp.concatenate(pooled, axis=1)
    y_pool = _dot(pooled.astype(BF16), wpool_ref[...]) * pscale_ref[...]
    pool_n = _rms(y_pool, gpool_ref[...])

    res = (_dot(ssm_n.astype(BF16), wout_ref[0:SSM_WIDTH, :])
           + _dot(pool_n.astype(BF16), wout_ref[SSM_WIDTH:, :]))
    o_ref[...] = xr + res


def _const_spec(shape):
    zeros = (0,) * len(shape)
    return pl.BlockSpec(shape, lambda *_: zeros, pipeline_mode=pl.Buffered(1))


def _mixer(x, gmix, win, bmat, are, aim, cre, cim, dskip, wglu, bglu, wpool,
           pscale, gssm, gpool, wout):
    batch_major_in = x.ndim == 3
    n = x.shape[0] * x.shape[1] if batch_major_in else x.shape[0]
    consts = (gmix, win, bmat, are, aim, cre, cim, dskip, wglu, bglu, wpool,
              pscale, gssm, gpool, wout)
    row_spec = pl.BlockSpec((MIX_ROWS, D_MODEL), lambda c: (c, 0))
    x_spec = (pl.BlockSpec((SUBLANES, MIX_ROWS // SUBLANES, D_MODEL), lambda c: (0, c, 0))
              if batch_major_in else row_spec)
    return pl.pallas_call(
        functools.partial(_mixer_kernel, batch_major_in=batch_major_in),
        grid=(n // MIX_ROWS,),
        in_specs=[x_spec] + [_const_spec(a.shape) for a in consts],
        out_specs=row_spec,
        out_shape=jax.ShapeDtypeStruct((n, D_MODEL), F32),
        scratch_shapes=[
            pltpu.VMEM((MIX_ROWS, 2 * STATE_W), F32),
            pltpu.VMEM((SUBLANES, STATE_W), F32),
            pltpu.VMEM((SUBLANES, STATE_W), F32),
            pltpu.VMEM((POOL_HIST_ROWS + MIX_ROWS, POOL_WIDTH), F32),
        ],
        compiler_params=pltpu.CompilerParams(
            dimension_semantics=("arbitrary",), vmem_limit_bytes=VMEM_LIMIT),
        name="mixer",
    )(x, *consts)


def _sort16_pairs():
    n, pairs, p = 16, [], 1
    while p < n:
        k = p
        while k >= 1:
            for j in range(k % p, n - k, 2 * k):
                for i in range(min(k, n - j - k)):
                    if (i + j) // (2 * p) == (i + j + k) // (2 * p):
                        pairs.append((i + j, i + j + k))
            k //= 2
        p *= 2
    return pairs


_SORT16 = _sort16_pairs()
N_TOP = PEER_TOPK + 1


def _top_sorted(s, out_ref):
    v = [s[SUBLANES * k:SUBLANES * (k + 1), :] for k in range(16)]
    for i, j in _SORT16:
        hi = jnp.maximum(v[i], v[j])
        lo = jnp.minimum(v[i], v[j])
        v[i], v[j] = hi, lo
    for i in range(N_TOP):
        head = v[0]
        m = jnp.max(head, axis=0, keepdims=True)
        out_ref[i:i + 1, :] = m
        if i + 1 < N_TOP:
            pop = head == m
            depth = N_TOP - i
            v = [jnp.where(pop, v[k + 1] if k + 1 < len(v) else NEG_INF, v[k])
                 for k in range(depth - 1)]


def _route_kernel(x_ref, gffn_ref, wqt_ref, k1_ref, k2_ref,
                  hb_ref, cnt_ref, e1_ref, rank_ref, w2_ref,
                  qt_ref, l1_ref, l2_ref):
    tm = x_ref.shape[0]
    h = _rms(x_ref[...], gffn_ref[...])
    hbt = h.T.astype(BF16)
    hb_ref[...] = _words(hbt)
    qt_ref[...] = _dot(wqt_ref[...], hbt)
    row = lax.broadcasted_iota(jnp.int32, (SUBLANES, tm), 0)

    def head(hd, carry):
        base = pl.multiple_of(hd * 2 * PEER_HALF, 2 * PEER_HALF)
        q1 = qt_ref[pl.ds(base, PEER_HALF), :].astype(BF16)
        q2 = qt_ref[pl.ds(base + PEER_HALF, PEER_HALF), :].astype(BF16)
        s1 = _dot(k1_ref[...], q1)
        s2 = _dot(k2_ref[...], q2)
        _top_sorted(s1, l1_ref)
        _top_sorted(s2, l2_ref)
        m1 = l1_ref[0:1, :]
        m2 = l2_ref[0:1, :]
        a = l1_ref[1:9, :]
        b = l2_ref[1:9, :]
        cands = [
            m1 + l2_ref[0:8, :],
            m1 + l2_ref[8:16, :],
            m2 + a,
            m2 + l1_ref[9:17, :],
            jnp.where(row < 7, l1_ref[1:2, :] + b, NEG_INF),
            jnp.where(row < 4, l1_ref[2:3, :] + b, NEG_INF),
            jnp.where(row < 3, l1_ref[3:4, :] + b, NEG_INF),
            jnp.where(row < 2, l1_ref[4:5, :] + b, NEG_INF),
            jnp.where((row >= 4) & (row < 7), l2_ref[1:2, :] + a, NEG_INF),
            jnp.where(row == 7, m1 + l2_ref[9:17, :], NEG_INF),
        ]
        tops = []
        for i in range(N_TOP):
            m = cands[0]
            for cnd in cands[1:]:
                m = jnp.maximum(m, cnd)
            m = jnp.max(m, axis=0, keepdims=True)
            tops.append(m)
            if i + 1 < N_TOP:
                cands = [jnp.where(cnd == m, NEG_INF, cnd) for cnd in cands]
        tau = 0.5 * (tops[PEER_TOPK - 1] + tops[PEER_TOPK])
        z = jnp.zeros_like(tau)
        for i in range(PEER_TOPK):
            z = z + jnp.exp(tops[i] - tops[0])
        theta = tau - s1
        cnt = jnp.zeros_like(s1)
        rank = jnp.zeros_like(s2)
        for j in range(PEER_TOPK):
            v2j = l2_ref[j:j + 1, :]
            cnt = jnp.where(v2j >= theta, j + 1.0, cnt)
            rank = jnp.where(v2j > s2, j + 1.0, rank)
        e1 = jnp.exp(s1 - m1)
        w2 = jnp.exp(s2 - m2) * (0.5 / z)
        outs = ((cnt_ref, cnt), (e1_ref, e1),
                (rank_ref, _words(rank.astype(BF16))), (w2_ref, _words(w2.astype(BF16))))
        for ref, val in outs:
            for lt in range(tm // LANES):
                ref[hd, lt] = val[:, lt * LANES:(lt + 1) * LANES]
        return carry

    lax.fori_loop(0, PEER_HEADS, head, 0)


def _route(x, gffn, wqt, k1, k2):
    n = x.shape[0]
    tm = ROUTE_TM
    def key_spec(rows):
        return pl.BlockSpec((PEER_HEADS, tm // LANES, rows, LANES), lambda i: (0, i, 0, 0))

    def key_shape(rows, dtype):
        return jax.ShapeDtypeStruct((PEER_HEADS, n // LANES, rows, LANES), dtype)

    return pl.pallas_call(
        _route_kernel,
        grid=(n // tm,),
        in_specs=[pl.BlockSpec((tm, D_MODEL), lambda i: (i, 0)),
                  _const_spec(gffn.shape), _const_spec(wqt.shape),
                  _const_spec(k1.shape), _const_spec(k2.shape)],
        out_specs=[pl.BlockSpec((D_MODEL // 2, tm), lambda i: (0, i)),
                   key_spec(PEER_N_KEYS), key_spec(PEER_N_KEYS),
                   key_spec(PEER_N_KEYS // 2), key_spec(PEER_N_KEYS // 2)],
        out_shape=[jax.ShapeDtypeStruct((D_MODEL // 2, n), jnp.uint32),
                   key_shape(PEER_N_KEYS, F32), key_shape(PEER_N_KEYS, F32),
                   key_shape(PEER_N_KEYS // 2, jnp.uint32),
                   key_shape(PEER_N_KEYS // 2, jnp.uint32)],
        scratch_shapes=[
            pltpu.VMEM((PEER_HEADS * 2 * PEER_HALF, tm), F32),
            pltpu.VMEM((24, tm), F32),
            pltpu.VMEM((24, tm), F32),
        ],
        compiler_params=pltpu.CompilerParams(
            dimension_semantics=("arbitrary",), vmem_limit_bytes=VMEM_LIMIT),
        name="peer_route",
    )(x, gffn, wqt, k1, k2)


def _packed_row(row, rows):
    tile = jnp.broadcast_to(row, (2 * SUBLANES, LANES)).astype(BF16)
    return jnp.tile(tile, (rows // tile.shape[0], 1))


def _peer_kernel(x_ref, hb_ref, cnt_ref, e1_ref, rank_ref, w2_ref, u_ref, vt_ref,
                 gfin_ref, o_ref, s_ref, a_ref, acc_ref, *, final_norm, e_tiles, n_tiles):
    g = pl.program_id(0)
    te = 2 * u_ref.shape[0]
    tm = hb_ref.shape[1]
    blocks = te // PEER_N_KEYS
    half_piece = PEER_PIECE // 2
    half_keys = PEER_N_KEYS // 2
    cur = g % 2
    prev = 1 - cur

    @pl.when(g == 0)
    def _():
        s_ref[...] = jnp.zeros_like(s_ref)
        a_ref[...] = jnp.zeros_like(a_ref)
        acc_ref[...] = jnp.zeros_like(acc_ref)

    tile2 = jnp.clip(g - 1, 0, n_tiles - 1)
    tile3 = jnp.clip(g - 2, 0, n_tiles - 1)
    i1_base = (tile2 % e_tiles) * blocks

    blocks_per_chunk = PEER_PIECE // PEER_N_KEYS // MXU_CHUNKS
    score_rows = PEER_PIECE // MXU_CHUNKS
    value_rows = D_MODEL // MXU_CHUNKS

    def gate_blocks(r, first):
        for sub in range(first, first + blocks_per_chunk, GATE_GROUP):
            ibs = [r * (PEER_PIECE // PEER_N_KEYS) + sub + k for k in range(GATE_GROUP)]
            for lt, part in itertools.product(range(tm // LANES),
                                              range(PEER_N_KEYS // GATE_ROWS)):
                cols = slice(lt * LANES, (lt + 1) * LANES)
                krows = slice(part * GATE_ROWS // 2, (part + 1) * GATE_ROWS // 2)
                gates = [None] * GATE_GROUP
                for hd in range(PEER_HEADS):
                    rank = _halves(rank_ref[hd, lt, krows, :])
                    w2 = _halves(w2_ref[hd, lt, krows, :])
                    for k, ib in enumerate(ibs):
                        i1 = i1_base + ib
                        cnt = _packed_row(cnt_ref[hd, lt, pl.ds(i1, 1), :], GATE_ROWS)
                        e1 = _packed_row(e1_ref[hd, lt, pl.ds(i1, 1), :], GATE_ROWS)
                        term = jnp.where(rank < cnt, w2 * e1, jnp.zeros((), BF16))
                        gates[k] = term if gates[k] is None else gates[k] + term
                for k, ib in enumerate(ibs):
                    lo = ib * PEER_N_KEYS + part * GATE_ROWS
                    brow = pl.ds(pl.multiple_of(lo, GATE_ROWS), GATE_ROWS)
                    bwrow = pl.ds(pl.multiple_of(lo // 2, GATE_ROWS // 2), GATE_ROWS // 2)
                    s = s_ref[prev, brow, cols]
                    act = s * (1.0 + lax.erf(s * INV_SQRT2))
                    a_ref[prev, bwrow, cols] = _words(act.astype(BF16) * gates[k])

    def piece(r, carry):
        wrows = pl.ds(pl.multiple_of(r * half_piece, half_piece), half_piece)
        for c in range(MXU_CHUNKS):
            vrows = slice(c * value_rows, (c + 1) * value_rows)
            vwrows = slice(c * value_rows // 2, (c + 1) * value_rows // 2)
            acc_ref[vrows, :] += _dot(_halves(vt_ref[r, vwrows, :]),
                                      _halves(a_ref[cur, wrows, :]))
            gate_blocks(r, c * blocks_per_chunk)
            lo = r * PEER_PIECE + c * score_rows
            srows = pl.ds(pl.multiple_of(lo, score_rows), score_rows)
            swrows = pl.ds(pl.multiple_of(lo // 2, score_rows // 2), score_rows // 2)
            s_ref[cur, srows, :] = _dot(_halves(u_ref[swrows, :]), _halves(hb_ref[...]))
        return carry

    lax.fori_loop(0, te // PEER_PIECE, piece, 0)

    @pl.when((g >= 2) & (tile3 % e_tiles == e_tiles - 1))
    def _():
        out = x_ref[...] + acc_ref[...].T
        if final_norm:
            out = _rms(out, gfin_ref[...])
            o_ref[...] = pltpu.einshape("(tb)d->btd", out, b=SUBLANES)
        else:
            o_ref[...] = out
        acc_ref[...] = jnp.zeros_like(acc_ref)


def _peer(x, hb_w, cnt, e1, rank_w, w2_w, u_w, vt_w, gfin, final_norm):
    n = x.shape[0]
    tm, te = PEER_TM, PEER_TE
    e_tiles = PEER_N_EXPERTS // te
    n_tiles = (n // tm) * e_tiles
    last = n_tiles - 1

    def t1(g):
        return jnp.minimum(g, last)

    def t2(g):
        return jnp.clip(g - 1, 0, last)

    def t3(g):
        return jnp.clip(g - 2, 0, last)

    def key_spec(rows):
        return pl.BlockSpec((PEER_HEADS, tm // LANES, rows, LANES),
                            lambda g: (0, t2(g) // e_tiles, 0, 0),
                            pipeline_mode=pl.Buffered(1))

    out_row_spec = pl.BlockSpec((tm, D_MODEL), lambda g: (t3(g) // e_tiles, 0))
    return pl.pallas_call(
        functools.partial(_peer_kernel, final_norm=final_norm, e_tiles=e_tiles,
                          n_tiles=n_tiles),
        grid=(n_tiles + 2,),
        in_specs=[pl.BlockSpec((tm, D_MODEL), lambda g: (t3(g) // e_tiles, 0),
                               pipeline_mode=pl.Buffered(1)),
                  pl.BlockSpec((D_MODEL // 2, tm), lambda g: (0, t1(g) // e_tiles)),
                  key_spec(PEER_N_KEYS), key_spec(PEER_N_KEYS),
                  key_spec(PEER_N_KEYS // 2), key_spec(PEER_N_KEYS // 2),
                  pl.BlockSpec((te // 2, D_MODEL), lambda g: (t1(g) % e_tiles, 0)),
                  pl.BlockSpec((te // PEER_PIECE, D_MODEL // 2, PEER_PIECE),
                               lambda g: (t3(g) % e_tiles, 0, 0)),
                  pl.BlockSpec(gfin.shape, lambda g: (0, 0))],
        out_specs=(pl.BlockSpec((SUBLANES, tm // SUBLANES, D_MODEL),
                                lambda g: (0, t3(g) // e_tiles, 0))
                   if final_norm else out_row_spec),
        out_shape=(jax.ShapeDtypeStruct((SUBLANES, n // SUBLANES, D_MODEL), F32)
                   if final_norm else jax.ShapeDtypeStruct((n, D_MODEL), F32)),
        scratch_shapes=[
            pltpu.VMEM((2, te, tm), F32),
            pltpu.VMEM((2, te // 2, tm), jnp.uint32),
            pltpu.VMEM((D_MODEL, tm), F32),
        ],
        compiler_params=pltpu.CompilerParams(
            dimension_semantics=("arbitrary",), vmem_limit_bytes=VMEM_LIMIT),
        name="peer_dense",
    )(x, hb_w, cnt, e1, rank_w, w2_w, u_w, vt_w, gfin)


def _tables_kernel(u_ref, v_ref, uw_ref, vw_ref):
    uw_ref[...] = _words(u_ref[0].astype(BF16))
    vw_ref[0] = _words(v_ref[0].T.astype(BF16))


def _expert_tables(u, v, layer):
    _, e, d = u.shape
    rows = PEER_PIECE
    table_spec = pl.BlockSpec((1, rows, d), lambda i: (layer, i, 0))
    return pl.pallas_call(
        _tables_kernel,
        grid=(e // rows,),
        in_specs=[table_spec, table_spec],
        out_specs=[pl.BlockSpec((rows // 2, d), lambda i: (i, 0)),
                   pl.BlockSpec((1, d // 2, rows), lambda i: (i, 0, 0))],
        out_shape=[jax.ShapeDtypeStruct((e // 2, d), jnp.uint32),
                   jax.ShapeDtypeStruct((e // rows, d // 2, rows), jnp.uint32)],
        compiler_params=pltpu.CompilerParams(
            dimension_semantics=("arbitrary",), vmem_limit_bytes=VMEM_LIMIT),
        name="expert_tables",
    )(u, v)


def _block_diag(blocks):
    g, r, c = blocks.shape
    eye = jnp.eye(g, dtype=blocks.dtype)
    return (blocks[:, :, None, :] * eye[:, None, :, None]).reshape(g * r, g * c)


def kernel(x, norm_mix, w_in, a_re, a_im, log_dt, b_re, b_im, c_re, c_im, d_skip, w_glu, b_glu, w_pool, pool_scale, g_out_ssm, g_out_pool, w_out, norm_ffn, w_q, k1, k2, u_experts, v_experts, norm_final):
    bsz, seq, dm = x.shape
    assert (bsz, dm) == (SUBLANES, D_MODEL) and seq % (MIX_ROWS // SUBLANES) == 0
    depth = w_in.shape[0]
    n = bsz * seq
    row = lambda a: a.reshape(1, -1).astype(F32)

    abr, abi, btr, bti = _discretise(a_re, a_im, log_dt, b_re, b_im)
    xt = x
    gfin = row(norm_final)

    for i in range(depth):
        hg = SSM_GROUPS // 2
        bmat = jnp.stack([_block_diag(b[lo:lo + hg]) for b in (btr[i], bti[i])
                          for lo in (0, hg)]).astype(BF16)
        cre, cim = (jnp.stack([_block_diag(jnp.transpose(c[lo:lo + hg], (0, 2, 1)))
                               for lo in (0, hg)]).astype(BF16)
                    for c in (c_re[i], c_im[i]))
        xt = _mixer(
            xt, row(norm_mix[i]), w_in[i].astype(BF16), bmat,
            abr[i].reshape(1, STATE_W), abi[i].reshape(1, STATE_W), cre, cim,
            row(d_skip[i]), w_glu[i].astype(BF16), row(b_glu[i]),
            _block_diag(w_pool[i]).astype(BF16), row(pool_scale[i]),
            row(g_out_ssm[i]), row(g_out_pool[i]), w_out[i].astype(BF16))
        hb, cnt, e1, rank, w2 = _route(
            xt, row(norm_ffn[i]), jnp.transpose(w_q[i]).astype(BF16),
            k1[i].astype(BF16), k2[i].astype(BF16))
        u_w, vt_w = _expert_tables(u_experts, v_experts, i)
        xt = _peer(xt, hb, cnt, e1, rank, w2, u_w, vt_w, gfin,
                   final_norm=(i == depth - 1))
    return xt
```

```python
import functools
import itertools

import jax
import jax.numpy as jnp
from jax import lax
from jax.experimental import pallas as pl
from jax.experimental.pallas import tpu as pltpu

F32 = jnp.float32
BF16 = jnp.bfloat16

D_MODEL = 1024
SSM_WIDTH = 512
POOL_WIDTH = 512
SSM_GROUP = 16
SSM_GROUPS = 32
SSM_STATE = 64
STATE_W = SSM_GROUPS * SSM_STATE
HALF_SSM = SSM_WIDTH // 2
HALF_STATE = STATE_W // 2
POOL_WINDOWS = (2, 4, 8, 16)
POOL_GROUP_WIDTH = 128
PEER_HEADS = 8
PEER_N_KEYS = 128
PEER_N_EXPERTS = PEER_N_KEYS * PEER_N_KEYS
PEER_HALF = 128
PEER_TOPK = 16
RMS_EPS = 1e-6

SUBLANES = 8
LANES = 128
MXU_DEPTH = 256
MIX_ROWS = 512
POOL_HIST_ROWS = 128
ROUTE_TM = 512
PEER_TM = 512
PEER_TE = 2048
PEER_PIECE = 1024
GATE_ROWS = 64
GATE_GROUP = 2
VMEM_LIMIT = 60 * 1024 * 1024

NEG_INF = float("-inf")
INV_SQRT2 = 0.7071067811865476


def _rms(x, g):
    return x * lax.rsqrt(jnp.mean(x * x, axis=-1, keepdims=True) + RMS_EPS) * g


def _gelu(x):
    return 0.5 * x * (1.0 + lax.erf(x * INV_SQRT2))


def _dot(a, b):
    return jnp.dot(a, b, preferred_element_type=F32)


def _words(x):
    return pltpu.bitcast(x, jnp.uint32)


def _halves(w):
    return pltpu.bitcast(w, BF16)


def _disc_kernel(are_ref, aim_ref, ldt_ref, bre_ref, bim_ref,
                 abr_ref, abi_ref, btr_ref, bti_ref):
    lam_re = are_ref[...]
    lam_im = aim_ref[...]
    dt = jnp.exp(ldt_ref[...])
    decay = jnp.exp(lam_re * dt)
    abar_re = decay * jnp.cos(lam_im * dt)
    abar_im = decay * jnp.sin(lam_im * dt)
    inv_den = 1.0 / (lam_re * lam_re + lam_im * lam_im)
    num_re = abar_re - 1.0
    zoh_re = (num_re * lam_re + abar_im * lam_im) * inv_den
    zoh_im = (abar_im * lam_re - num_re * lam_im) * inv_den
    b_re = bre_ref[...]
    b_im = bim_ref[...]
    abr_ref[...] = abar_re
    abi_ref[...] = abar_im
    btr_ref[...] = zoh_re * b_re - zoh_im * b_im
    bti_ref[...] = zoh_re * b_im + zoh_im * b_re


def _discretise(a_re, a_im, log_dt, b_re, b_im):
    nl = a_re.shape[0]
    rows = nl * SSM_GROUPS * SSM_GROUP
    shp = (nl, SSM_GROUPS, SSM_GROUP, SSM_STATE)

    def rep(a):
        return jnp.broadcast_to(a[:, :, None, :], shp).reshape(rows, SSM_STATE)

    ldt = jnp.broadcast_to(log_dt[:, :, None, None], shp).reshape(rows, SSM_STATE)
    bre = jnp.transpose(b_re, (0, 1, 3, 2)).reshape(rows, SSM_STATE)
    bim = jnp.transpose(b_im, (0, 1, 3, 2)).reshape(rows, SSM_STATE)
    out = jax.ShapeDtypeStruct((rows, SSM_STATE), F32)
    abr, abi, btr, bti = pl.pallas_call(
        _disc_kernel, out_shape=(out, out, out, out), name="s5_discretise",
    )(rep(a_re), rep(a_im), ldt, bre, bim)
    abr = abr.reshape(shp)[:, :, 0, :].reshape(nl, STATE_W)
    abi = abi.reshape(shp)[:, :, 0, :].reshape(nl, STATE_W)
    return abr, abi, btr.reshape(shp), bti.reshape(shp)


def _mixer_kernel(x_ref, gmix_ref, win_ref, bmat_ref, are_ref, aim_ref,
                  cre_ref, cim_ref, dskip_ref, wglu_ref, bglu_ref, wpool_ref,
                  pscale_ref, gssm_ref, gpool_ref, wout_ref, o_ref,
                  st_ref, sre_ref, sim_ref, ext_ref, *, batch_major_in):
    c = pl.program_id(0)
    rows = o_ref.shape[0]
    steps = rows // SUBLANES

    @pl.when(c == 0)
    def _():
        sre_ref[...] = jnp.zeros_like(sre_ref)
        sim_ref[...] = jnp.zeros_like(sim_ref)
        ext_ref[0:POOL_HIST_ROWS, :] = jnp.zeros((POOL_HIST_ROWS, POOL_WIDTH), F32)

    if batch_major_in:
        xr = pltpu.einshape("btd->(tb)d", x_ref[...])
    else:
        xr = x_ref[...]
    hn = _rms(xr, gmix_ref[...])
    proj = _dot(hn.astype(BF16), win_ref[...])
    u_ssm = proj[:, :SSM_WIDTH]
    u_pool = proj[:, SSM_WIDTH:]

    u_b = u_ssm.astype(BF16)
    for part in range(2):
        for hf in range(2):
            lo = part * STATE_W + hf * HALF_STATE
            st_ref[:, lo:lo + HALF_STATE] = _dot(
                u_b[:, hf * HALF_SSM:(hf + 1) * HALF_SSM], bmat_ref[2 * part + hf])
    a_re = jnp.broadcast_to(are_ref[...], (SUBLANES, STATE_W))
    a_im = jnp.broadcast_to(aim_ref[...], (SUBLANES, STATE_W))

    def step(t, carry):
        s_re, s_im = carry
        r = pl.multiple_of(t * SUBLANES, SUBLANES)
        in_re = st_ref[pl.ds(r, SUBLANES), 0:STATE_W]
        in_im = st_ref[pl.ds(r, SUBLANES), STATE_W:2 * STATE_W]
        n_re = a_re * s_re - a_im * s_im + in_re
        n_im = a_re * s_im + a_im * s_re + in_im
        st_ref[pl.ds(r, SUBLANES), 0:STATE_W] = n_re
        st_ref[pl.ds(r, SUBLANES), STATE_W:2 * STATE_W] = n_im
        return n_re, n_im

    s_re, s_im = lax.fori_loop(0, steps, step, (sre_ref[...], sim_ref[...]))
    sre_ref[...] = s_re
    sim_ref[...] = s_im

    y = []
    for hf in range(2):
        lo = hf * HALF_STATE
        y.append(_dot(st_ref[:, lo:lo + HALF_STATE].astype(BF16), cre_ref[hf])
                 - _dot(st_ref[:, STATE_W + lo:STATE_W + lo + HALF_STATE].astype(BF16),
                        cim_ref[hf]))
    y = jnp.concatenate(y, axis=1) + dskip_ref[...] * u_ssm
    y = _gelu(y)
    y = y * jax.nn.sigmoid(_dot(y.astype(BF16), wglu_ref[...]) + bglu_ref[...])
    ssm_n = _rms(y, gssm_ref[...])

    ext_ref[POOL_HIST_ROWS:, :] = u_pool
    t_idx = c * steps + jnp.right_shift(
        lax.broadcasted_iota(jnp.int32, (rows, POOL_GROUP_WIDTH), 0),
        SUBLANES.bit_length() - 1)
    pooled = []
    for gi, win in enumerate(POOL_WINDOWS):
        lo = gi * POOL_GROUP_WIDTH
        hi = lo + POOL_GROUP_WIDTH
        acc = ext_ref[POOL_HIST_ROWS:, lo:hi]
        for k in range(1, win):
            off = POOL_HIST_ROWS - SUBLANES * k
            acc = acc + ext_ref[off:off + rows, lo:hi]
        count = jnp.minimum(t_idx + 1, win).astype(F32)
        pooled.append(acc / count - ext_ref[POOL_HIST_ROWS:, lo:hi])
    ext_ref[0:POOL_HIST_ROWS, :] = ext_ref[rows:rows + POOL_HIST_ROWS, :]
    pooled = jnp.concatenate(pooled, axis=1)
    y_pool = _dot(pooled.astype(BF16), wpool_ref[...]) * pscale_ref[...]
    pool_n = _rms(y_pool, gpool_ref[...])

    res = (_dot(ssm_n.astype(BF16), wout_ref[0:SSM_WIDTH, :])
           + _dot(pool_n.astype(BF16), wout_ref[SSM_WIDTH:, :]))
    o_ref[...] = xr + res


def _const_spec(shape):
    zeros = (0,) * len(shape)
    return pl.BlockSpec(shape, lambda *_: zeros, pipeline_mode=pl.Buffered(1))


def _mixer(x, gmix, win, bmat, are, aim, cre, cim, dskip, wglu, bglu, wpool,
           pscale, gssm, gpool, wout):
    batch_major_in = x.ndim == 3
    n = x.shape[0] * x.shape[1] if batch_major_in else x.shape[0]
    consts = (gmix, win, bmat, are, aim, cre, cim, dskip, wglu, bglu, wpool,
              pscale, gssm, gpool, wout)
    row_spec = pl.BlockSpec((MIX_ROWS, D_MODEL), lambda c: (c, 0))
    x_spec = (pl.BlockSpec((SUBLANES, MIX_ROWS // SUBLANES, D_MODEL), lambda c: (0, c, 0))
              if batch_major_in else row_spec)
    return pl.pallas_call(
        functools.partial(_mixer_kernel, batch_major_in=batch_major_in),
        grid=(n // MIX_ROWS,),
        in_specs=[x_spec] + [_const_spec(a.shape) for a in consts],
        out_specs=row_spec,
        out_shape=jax.ShapeDtypeStruct((n, D_MODEL), F32),
        scratch_shapes=[
            pltpu.VMEM((MIX_ROWS, 2 * STATE_W), F32),
            pltpu.VMEM((SUBLANES, STATE_W), F32),
            pltpu.VMEM((SUBLANES, STATE_W), F32),
            pltpu.VMEM((POOL_HIST_ROWS + MIX_ROWS, POOL_WIDTH), F32),
        ],
        compiler_params=pltpu.CompilerParams(
            dimension_semantics=("arbitrary",), vmem_limit_bytes=VMEM_LIMIT),
        name="mixer",
    )(x, *consts)


def _sort16_pairs():
    n, pairs, p = 16, [], 1
    while p < n:
        k = p
        while k >= 1:
            for j in range(k % p, n - k, 2 * k):
                for i in range(min(k, n - j - k)):
                    if (i + j) // (2 * p) == (i + j + k) // (2 * p):
                        pairs.append((i + j, i + j + k))
            k //= 2
        p *= 2
    return pairs


_SORT16 = _sort16_pairs()
N_TOP = PEER_TOPK + 1
SORTED_ROWS = -(-N_TOP // SUBLANES) * SUBLANES


def _top_sorted(s, out_ref):
    v = [s[SUBLANES * k:SUBLANES * (k + 1), :] for k in range(16)]
    for i, j in _SORT16:
        hi = jnp.maximum(v[i], v[j])
        lo = jnp.minimum(v[i], v[j])
        v[i], v[j] = hi, lo
    for i in range(N_TOP):
        head = v[0]
        m = jnp.max(head, axis=0, keepdims=True)
        out_ref[i:i + 1, :] = m
        if i + 1 < N_TOP:
            pop = head == m
            depth = N_TOP - i
            v = [jnp.where(pop, v[k + 1] if k + 1 < len(v) else NEG_INF, v[k])
                 for k in range(depth - 1)]


def _route_kernel(x_ref, gffn_ref, wqt_ref, k1_ref, k2_ref,
                  hb_ref, cnt_ref, e1_ref, rank_ref, w2_ref,
                  qt_ref, l1_ref, l2_ref):
    tm = x_ref.shape[0]
    h = _rms(x_ref[...], gffn_ref[...])
    hbt = h.T.astype(BF16)
    hb_ref[...] = _words(hbt)
    qt_ref[...] = _dot(wqt_ref[...], hbt)
    row = lax.broadcasted_iota(jnp.int32, (SUBLANES, tm), 0)

    def head(hd, carry):
        base = pl.multiple_of(hd * 2 * PEER_HALF, 2 * PEER_HALF)
        q1 = qt_ref[pl.ds(base, PEER_HALF), :].astype(BF16)
        q2 = qt_ref[pl.ds(base + PEER_HALF, PEER_HALF), :].astype(BF16)
        s1 = _dot(k1_ref[...], q1)
        s2 = _dot(k2_ref[...], q2)
        _top_sorted(s1, l1_ref)
        _top_sorted(s2, l2_ref)
        m1 = l1_ref[0:1, :]
        m2 = l2_ref[0:1, :]
        a = l1_ref[1:9, :]
        b = l2_ref[1:9, :]
        cands = [
            m1 + l2_ref[0:8, :],
            m1 + l2_ref[8:16, :],
            m2 + a,
            m2 + l1_ref[9:17, :],
            jnp.where(row < 7, l1_ref[1:2, :] + b, NEG_INF),
            jnp.where(row < 4, l1_ref[2:3, :] + b, NEG_INF),
            jnp.where(row < 3, l1_ref[3:4, :] + b, NEG_INF),
            jnp.where(row < 2, l1_ref[4:5, :] + b, NEG_INF),
            jnp.where((row >= 4) & (row < 7), l2_ref[1:2, :] + a, NEG_INF),
            jnp.where(row == 7, m1 + l2_ref[9:17, :], NEG_INF),
        ]
        tops = []
        for i in range(N_TOP):
            m = cands[0]
            for cnd in cands[1:]:
                m = jnp.maximum(m, cnd)
            m = jnp.max(m, axis=0, keepdims=True)
            tops.append(m)
            if i + 1 < N_TOP:
                cands = [jnp.where(cnd == m, NEG_INF, cnd) for cnd in cands]
        tau = 0.5 * (tops[PEER_TOPK - 1] + tops[PEER_TOPK])
        z = jnp.zeros_like(tau)
        for i in range(PEER_TOPK):
            z = z + jnp.exp(tops[i] - tops[0])
        theta = tau - s1
        cnt = jnp.zeros_like(s1)
        rank = jnp.zeros_like(s2)
        for j in range(PEER_TOPK):
            v2j = l2_ref[j:j + 1, :]
            cnt = jnp.where(v2j >= theta, j + 1.0, cnt)
            rank = jnp.where(v2j > s2, j + 1.0, rank)
        e1 = jnp.exp(s1 - m1)
        w2 = jnp.exp(s2 - m2) * (0.5 / z)
        outs = ((cnt_ref, cnt), (e1_ref, e1),
                (rank_ref, _words(rank.astype(BF16))), (w2_ref, _words(w2.astype(BF16))))
        for ref, val in outs:
            for lt in range(tm // LANES):
                ref[hd, lt] = val[:, lt * LANES:(lt + 1) * LANES]
        return carry

    lax.fori_loop(0, PEER_HEADS, head, 0)


def _route(x, gffn, wqt, k1, k2):
    n = x.shape[0]
    tm = ROUTE_TM
    def key_spec(rows):
        return pl.BlockSpec((PEER_HEADS, tm // LANES, rows, LANES), lambda i: (0, i, 0, 0))

    def key_shape(rows, dtype):
        return jax.ShapeDtypeStruct((PEER_HEADS, n // LANES, rows, LANES), dtype)

    return pl.pallas_call(
        _route_kernel,
        grid=(n // tm,),
        in_specs=[pl.BlockSpec((tm, D_MODEL), lambda i: (i, 0)),
                  _const_spec(gffn.shape), _const_spec(wqt.shape),
                  _const_spec(k1.shape), _const_spec(k2.shape)],
        out_specs=[pl.BlockSpec((D_MODEL // 2, tm), lambda i: (0, i)),
                   key_spec(PEER_N_KEYS), key_spec(PEER_N_KEYS),
                   key_spec(PEER_N_KEYS // 2), key_spec(PEER_N_KEYS // 2)],
        out_shape=[jax.ShapeDtypeStruct((D_MODEL // 2, n), jnp.uint32),
                   key_shape(PEER_N_KEYS, F32), key_shape(PEER_N_KEYS, F32),
                   key_shape(PEER_N_KEYS // 2, jnp.uint32),
                   key_shape(PEER_N_KEYS // 2, jnp.uint32)],
        scratch_shapes=[
            pltpu.VMEM((PEER_HEADS * 2 * PEER_HALF, tm), F32),
            pltpu.VMEM((SORTED_ROWS, tm), F32),
            pltpu.VMEM((SORTED_ROWS, tm), F32),
        ],
        compiler_params=pltpu.CompilerParams(
            dimension_semantics=("arbitrary",), vmem_limit_bytes=VMEM_LIMIT),
        name="peer_route",
    )(x, gffn, wqt, k1, k2)


def _packed_row(row, rows):
    tile = jnp.broadcast_to(row, (2 * SUBLANES, LANES)).astype(BF16)
    return jnp.tile(tile, (rows // tile.shape[0], 1))


def _peer_kernel(x_ref, hb_ref, cnt_ref, e1_ref, rank_ref, w2_ref, u_ref, vt_ref,
                 gfin_ref, o_ref, s_ref, a_ref, acc_ref, *, final_norm, e_tiles, n_tiles):
    g = pl.program_id(0)
    te = 2 * u_ref.shape[0]
    tm = hb_ref.shape[1]
    blocks = te // PEER_N_KEYS
    half_piece = PEER_PIECE // 2
    cur = g % 2
    prev = 1 - cur

    @pl.when(g == 0)
    def _():
        s_ref[...] = jnp.zeros_like(s_ref)
        a_ref[...] = jnp.zeros_like(a_ref)
        acc_ref[...] = jnp.zeros_like(acc_ref)

    tile2 = jnp.clip(g - 1, 0, n_tiles - 1)
    tile3 = jnp.clip(g - 2, 0, n_tiles - 1)
    i1_base = (tile2 % e_tiles) * blocks

    def gate_blocks(r):
        for sub in range(0, PEER_PIECE // PEER_N_KEYS, GATE_GROUP):
            ibs = [r * (PEER_PIECE // PEER_N_KEYS) + sub + k for k in range(GATE_GROUP)]
            for lt, part in itertools.product(range(tm // LANES),
                                              range(PEER_N_KEYS // GATE_ROWS)):
                cols = slice(lt * LANES, (lt + 1) * LANES)
                krows = slice(part * GATE_ROWS // 2, (part + 1) * GATE_ROWS // 2)
                gates = [None] * GATE_GROUP
                for hd in range(PEER_HEADS):
                    rank = _halves(rank_ref[hd, lt, krows, :])
                    w2 = _halves(w2_ref[hd, lt, krows, :])
                    for k, ib in enumerate(ibs):
                        i1 = i1_base + ib
                        cnt = _packed_row(cnt_ref[hd, lt, pl.ds(i1, 1), :], GATE_ROWS)
                        e1 = _packed_row(e1_ref[hd, lt, pl.ds(i1, 1), :], GATE_ROWS)
                        term = jnp.where(rank < cnt, w2 * e1, jnp.zeros((), BF16))
                        gates[k] = term if gates[k] is None else gates[k] + term
                for k, ib in enumerate(ibs):
                    lo = ib * PEER_N_KEYS + part * GATE_ROWS
                    brow = pl.ds(pl.multiple_of(lo, GATE_ROWS), GATE_ROWS)
                    bwrow = pl.ds(pl.multiple_of(lo // 2, GATE_ROWS // 2), GATE_ROWS // 2)
                    s = s_ref[prev, brow, cols]
                    act = s * (1.0 + lax.erf(s * INV_SQRT2))
                    a_ref[prev, bwrow, cols] = _words(act.astype(BF16) * gates[k])

    def piece(r, carry):
        rows = pl.ds(pl.multiple_of(r * PEER_PIECE, PEER_PIECE), PEER_PIECE)
        wrows = pl.ds(pl.multiple_of(r * half_piece, half_piece), half_piece)
        acc_ref[...] += _dot(_halves(vt_ref[r]), _halves(a_ref[cur, wrows, :]))
        gate_blocks(r)
        s_ref[cur, rows, :] = _dot(_halves(u_ref[wrows, :]), _halves(hb_ref[...]))
        return carry

    lax.fori_loop(0, te // PEER_PIECE, piece, 0)

    @pl.when((g >= 2) & (tile3 % e_tiles == e_tiles - 1))
    def _():
        out = x_ref[...] + acc_ref[...].T
        if final_norm:
            out = _rms(out, gfin_ref[...])
            o_ref[...] = pltpu.einshape("(tb)d->btd", out, b=SUBLANES)
        else:
            o_ref[...] = out
        acc_ref[...] = jnp.zeros_like(acc_ref)


def _peer(x, hb_w, cnt, e1, rank_w, w2_w, u_w, vt_w, gfin, final_norm):
    n = x.shape[0]
    tm, te = PEER_TM, PEER_TE
    e_tiles = PEER_N_EXPERTS // te
    n_tiles = (n // tm) * e_tiles
    last = n_tiles - 1

    def t1(g):
        return jnp.minimum(g, last)

    def t2(g):
        return jnp.clip(g - 1, 0, last)

    def t3(g):
        return jnp.clip(g - 2, 0, last)

    def key_spec(rows):
        return pl.BlockSpec((PEER_HEADS, tm // LANES, rows, LANES),
                            lambda g: (0, t2(g) // e_tiles, 0, 0),
                            pipeline_mode=pl.Buffered(1))

    out_row_spec = pl.BlockSpec((tm, D_MODEL), lambda g: (t3(g) // e_tiles, 0))
    return pl.pallas_call(
        functools.partial(_peer_kernel, final_norm=final_norm, e_tiles=e_tiles,
                          n_tiles=n_tiles),
        grid=(n_tiles + 2,),
        in_specs=[pl.BlockSpec((tm, D_MODEL), lambda g: (t3(g) // e_tiles, 0),
                               pipeline_mode=pl.Buffered(1)),
                  pl.BlockSpec((D_MODEL // 2, tm), lambda g: (0, t1(g) // e_tiles)),
                  key_spec(PEER_N_KEYS), key_spec(PEER_N_KEYS),
                  key_spec(PEER_N_KEYS // 2), key_spec(PEER_N_KEYS // 2),
                  pl.BlockSpec((te // 2, D_MODEL), lambda g: (t1(g) % e_tiles, 0)),
                  pl.BlockSpec((te // PEER_PIECE, D_MODEL // 2, PEER_PIECE),
                               lambda g: (t3(g) % e_tiles, 0, 0)),
                  pl.BlockSpec(gfin.shape, lambda g: (0, 0))],
        out_specs=(pl.BlockSpec((SUBLANES, tm // SUBLANES, D_MODEL),
                                lambda g: (0, t3(g) // e_tiles, 0))
                   if final_norm else out_row_spec),
        out_shape=(jax.ShapeDtypeStruct((SUBLANES, n // SUBLANES, D_MODEL), F32)
                   if final_norm else jax.ShapeDtypeStruct((n, D_MODEL), F32)),
        scratch_shapes=[
            pltpu.VMEM((2, te, tm), F32),
            pltpu.VMEM((2, te // 2, tm), jnp.uint32),
            pltpu.VMEM((D_MODEL, tm), F32),
        ],
        compiler_params=pltpu.CompilerParams(
            dimension_semantics=("arbitrary",), vmem_limit_bytes=VMEM_LIMIT),
        name="peer_dense",
    )(x, hb_w, cnt, e1, rank_w, w2_w, u_w, vt_w, gfin)


def _tables_kernel(u_ref, v_ref, uw_ref, vw_ref):
    uw_ref[...] = _words(u_ref[0].astype(BF16))
    vw_ref[0] = _words(v_ref[0].T.astype(BF16))


def _expert_tables(u, v, layer):
    _, e, d = u.shape
    rows = PEER_PIECE
    table_spec = pl.BlockSpec((1, rows, d), lambda i: (layer, i, 0))
    return pl.pallas_call(
        _tables_kernel,
        grid=(e // rows,),
        in_specs=[table_spec, table_spec],
        out_specs=[pl.BlockSpec((rows // 2, d), lambda i: (i, 0)),
                   pl.BlockSpec((1, d // 2, rows), lambda i: (i, 0, 0))],
        out_shape=[jax.ShapeDtypeStruct((e // 2, d), jnp.uint32),
                   jax.ShapeDtypeStruct((e // rows, d // 2, rows), jnp.uint32)],
        compiler_params=pltpu.CompilerParams(
            dimension_semantics=("arbitrary",), vmem_limit_bytes=VMEM_LIMIT),
        name="expert_tables",
    )(u, v)


def _block_diag(blocks):
    g, r, c = blocks.shape
    eye = jnp.eye(g, dtype=blocks.dtype)
    return (blocks[:, :, None, :] * eye[:, None, :, None]).reshape(g * r, g * c)


def kernel(x, norm_mix, w_in, a_re, a_im, log_dt, b_re, b_im, c_re, c_im, d_skip, w_glu, b_glu, w_pool, pool_scale, g_out_ssm, g_out_pool, w_out, norm_ffn, w_q, k1, k2, u_experts, v_experts, norm_final):
    bsz, seq, dm = x.shape
    assert (bsz, dm) == (SUBLANES, D_MODEL) and seq % (MIX_ROWS // SUBLANES) == 0
    depth = w_in.shape[0]
    n = bsz * seq
    row = lambda a: a.reshape(1, -1).astype(F32)

    abr, abi, btr, bti = _discretise(a_re, a_im, log_dt, b_re, b_im)
    xt = x
    gfin = row(norm_final)

    for i in range(depth):
        hg = SSM_GROUPS // 2
        bmat = jnp.stack([_block_diag(b[lo:lo + hg]) for b in (btr[i], bti[i])
                          for lo in (0, hg)]).astype(BF16)
        cre, cim = (jnp.stack([_block_diag(jnp.transpose(c[lo:lo + hg], (0, 2, 1)))
                               for lo in (0, hg)]).astype(BF16)
                    for c in (c_re[i], c_im[i]))
        xt = _mixer(
            xt, row(norm_mix[i]), w_in[i].astype(BF16), bmat,
            abr[i].reshape(1, STATE_W), abi[i].reshape(1, STATE_W), cre, cim,
            row(d_skip[i]), w_glu[i].astype(BF16), row(b_glu[i]),
            _block_diag(w_pool[i]).astype(BF16), row(pool_scale[i]),
            row(g_out_ssm[i]), row(g_out_pool[i]), w_out[i].astype(BF16))
        hb, cnt, e1, rank, w2 = _route(
            xt, row(norm_ffn[i]), jnp.transpose(w_q[i]).astype(BF16),
            k1[i].astype(BF16), k2[i].astype(BF16))
        u_w, vt_w = _expert_tables(u_experts, v_experts, i)
        xt = _peer(xt, hb, cnt, e1, rank, w2, u_w, vt_w, gfin,
                   final_norm=(i == depth - 1))
    return xt
```

```python
import functools
import itertools

import jax
import jax.numpy as jnp
from jax import lax
from jax.experimental import pallas as pl
from jax.experimental.pallas import tpu as pltpu

F32 = jnp.float32
BF16 = jnp.bfloat16

D_MODEL = 1024
SSM_WIDTH = 512
POOL_WIDTH = 512
SSM_GROUP = 16
SSM_GROUPS = 32
SSM_STATE = 64
STATE_W = SSM_GROUPS * SSM_STATE
HALF_SSM = SSM_WIDTH // 2
HALF_STATE = STATE_W // 2
POOL_WINDOWS = (2, 4, 8, 16)
POOL_GROUP_WIDTH = 128
PEER_HEADS = 8
PEER_N_KEYS = 128
PEER_N_EXPERTS = PEER_N_KEYS * PEER_N_KEYS
PEER_HALF = 128
PEER_TOPK = 16
RMS_EPS = 1e-6

SUBLANES = 8
LANES = 128
MXU_DEPTH = 256
MIX_ROWS = 512
POOL_HIST_ROWS = 128
ROUTE_TM = 512
PEER_TM = 512
PEER_TE = 2048
PEER_PIECE = 1024
GATE_ROWS = 64
GATE_GROUP = 2
VMEM_LIMIT = 60 * 1024 * 1024

NEG_INF = float("-inf")
INV_SQRT2 = 0.7071067811865476


def _rms(x, g):
    return x * lax.rsqrt(jnp.mean(x * x, axis=-1, keepdims=True) + RMS_EPS) * g


def _gelu(x):
    return 0.5 * x * (1.0 + lax.erf(x * INV_SQRT2))


def _dot(a, b):
    return jnp.dot(a, b, preferred_element_type=F32)


def _words(x):
    return pltpu.bitcast(x, jnp.uint32)


def _halves(w):
    return pltpu.bitcast(w, BF16)


def _disc_kernel(are_ref, aim_ref, ldt_ref, bre_ref, bim_ref,
                 abr_ref, abi_ref, btr_ref, bti_ref):
    lam_re = are_ref[...]
    lam_im = aim_ref[...]
    dt = jnp.exp(ldt_ref[...])
    decay = jnp.exp(lam_re * dt)
    abar_re = decay * jnp.cos(lam_im * dt)
    abar_im = decay * jnp.sin(lam_im * dt)
    inv_den = 1.0 / (lam_re * lam_re + lam_im * lam_im)
    num_re = abar_re - 1.0
    zoh_re = (num_re * lam_re + abar_im * lam_im) * inv_den
    zoh_im = (abar_im * lam_re - num_re * lam_im) * inv_den
    b_re = bre_ref[...]
    b_im = bim_ref[...]
    abr_ref[...] = abar_re
    abi_ref[...] = abar_im
    btr_ref[...] = zoh_re * b_re - zoh_im * b_im
    bti_ref[...] = zoh_re * b_im + zoh_im * b_re


def _discretise(a_re, a_im, log_dt, b_re, b_im):
    nl = a_re.shape[0]
    rows = nl * SSM_GROUPS * SSM_GROUP
    shp = (nl, SSM_GROUPS, SSM_GROUP, SSM_STATE)

    def rep(a):
        return jnp.broadcast_to(a[:, :, None, :], shp).reshape(rows, SSM_STATE)

    ldt = jnp.broadcast_to(log_dt[:, :, None, None], shp).reshape(rows, SSM_STATE)
    bre = jnp.transpose(b_re, (0, 1, 3, 2)).reshape(rows, SSM_STATE)
    bim = jnp.transpose(b_im, (0, 1, 3, 2)).reshape(rows, SSM_STATE)
    out = jax.ShapeDtypeStruct((rows, SSM_STATE), F32)
    abr, abi, btr, bti = pl.pallas_call(
        _disc_kernel, out_shape=(out, out, out, out), name="s5_discretise",
    )(rep(a_re), rep(a_im), ldt, bre, bim)
    abr = abr.reshape(shp)[:, :, 0, :].reshape(nl, STATE_W)
    abi = abi.reshape(shp)[:, :, 0, :].reshape(nl, STATE_W)
    return abr, abi, btr.reshape(shp), bti.reshape(shp)


def _mixer_kernel(x_ref, gmix_ref, win_ref, bmat_ref, are_ref, aim_ref,
                  cre_ref, cim_ref, dskip_ref, wglu_ref, bglu_ref, wpool_ref,
                  pscale_ref, gssm_ref, gpool_ref, wout_ref, o_ref,
                  st_ref, sre_ref, sim_ref, ext_ref, *, batch_major_in):
    c = pl.program_id(0)
    rows = o_ref.shape[0]
    steps = rows // SUBLANES

    @pl.when(c == 0)
    def _():
        sre_ref[...] = jnp.zeros_like(sre_ref)
        sim_ref[...] = jnp.zeros_like(sim_ref)
        ext_ref[0:POOL_HIST_ROWS, :] = jnp.zeros((POOL_HIST_ROWS, POOL_WIDTH), F32)

    if batch_major_in:
        xr = pltpu.einshape("btd->(tb)d", x_ref[...])
    else:
        xr = x_ref[...]
    hn = _rms(xr, gmix_ref[...])
    proj = _dot(hn.astype(BF16), win_ref[...])
    u_ssm = proj[:, :SSM_WIDTH]
    u_pool = proj[:, SSM_WIDTH:]

    u_b = u_ssm.astype(BF16)
    for part in range(2):
        for hf in range(2):
            lo = part * STATE_W + hf * HALF_STATE
            st_ref[:, lo:lo + HALF_STATE] = _dot(
                u_b[:, hf * HALF_SSM:(hf + 1) * HALF_SSM], bmat_ref[2 * part + hf])
    a_re = jnp.broadcast_to(are_ref[...], (SUBLANES, STATE_W))
    a_im = jnp.broadcast_to(aim_ref[...], (SUBLANES, STATE_W))

    def step(t, carry):
        s_re, s_im = carry
        r = pl.multiple_of(t * SUBLANES, SUBLANES)
        in_re = st_ref[pl.ds(r, SUBLANES), 0:STATE_W]
        in_im = st_ref[pl.ds(r, SUBLANES), STATE_W:2 * STATE_W]
        n_re = a_re * s_re - a_im * s_im + in_re
        n_im = a_re * s_im + a_im * s_re + in_im
        st_ref[pl.ds(r, SUBLANES), 0:STATE_W] = n_re
        st_ref[pl.ds(r, SUBLANES), STATE_W:2 * STATE_W] = n_im
        return n_re, n_im

    s_re, s_im = lax.fori_loop(0, steps, step, (sre_ref[...], sim_ref[...]))
    sre_ref[...] = s_re
    sim_ref[...] = s_im

    y = []
    for hf in range(2):
        lo = hf * HALF_STATE
        y.append(_dot(st_ref[:, lo:lo + HALF_STATE].astype(BF16), cre_ref[hf])
                 - _dot(st_ref[:, STATE_W + lo:STATE_W + lo + HALF_STATE].astype(BF16),
                        cim_ref[hf]))
    y = jnp.concatenate(y, axis=1) + dskip_ref[...] * u_ssm
    y = _gelu(y)
    y = y * jax.nn.sigmoid(_dot(y.astype(BF16), wglu_ref[...]) + bglu_ref[...])
    ssm_n = _rms(y, gssm_ref[...])

    ext_ref[POOL_HIST_ROWS:, :] = u_pool
    t_idx = c * steps + jnp.right_shift(
        lax.broadcasted_iota(jnp.int32, (rows, POOL_GROUP_WIDTH), 0),
        SUBLANES.bit_length() - 1)
    pooled = []
    for gi, win in enumerate(POOL_WINDOWS):
        lo = gi * POOL_GROUP_WIDTH
        hi = lo + POOL_GROUP_WIDTH
        acc = ext_ref[POOL_HIST_ROWS:, lo:hi]
        for k in range(1, win):
            off = POOL_HIST_ROWS - SUBLANES * k
            acc = acc + ext_ref[off:off + rows, lo:hi]
        count = jnp.minimum(t_idx + 1, win).astype(F32)
        pooled.append(acc / count - ext_ref[POOL_HIST_ROWS:, lo:hi])
    ext_ref[0:POOL_HIST_ROWS, :] = ext_ref[rows:rows + POOL_HIST_ROWS, :]
    pooled = jnp.concatenate(pooled, axis=1)
    y_pool = _dot(pooled.astype(BF16), wpool_ref[...]) * pscale_ref[...]
    pool_n = _rms(y_pool, gpool_ref[...])

    res = (_dot(ssm_n.astype(BF16), wout_ref[0:SSM_WIDTH, :])
           + _dot(pool_n.astype(BF16), wout_ref[SSM_WIDTH:, :]))
    o_ref[...] = xr + res


def _const_spec(shape):
    zeros = (0,) * len(shape)
    return pl.BlockSpec(shape, lambda *_: zeros, pipeline_mode=pl.Buffered(1))


def _mixer(x, gmix, win, bmat, are, aim, cre, cim, dskip, wglu, bglu, wpool,
           pscale, gssm, gpool, wout):
    batch_major_in = x.ndim == 3
    n = x.shape[0] * x.shape[1] if batch_major_in else x.shape[0]
    consts = (gmix, win, bmat, are, aim, cre, cim, dskip, wglu, bglu, wpool,
              pscale, gssm, gpool, wout)
    row_spec = pl.BlockSpec((MIX_ROWS, D_MODEL), lambda c: (c, 0))
    x_spec = (pl.BlockSpec((SUBLANES, MIX_ROWS // SUBLANES, D_MODEL), lambda c: (0, c, 0))
              if batch_major_in else row_spec)
    return pl.pallas_call(
        functools.partial(_mixer_kernel, batch_major_in=batch_major_in),
        grid=(n // MIX_ROWS,),
        in_specs=[x_spec] + [_const_spec(a.shape) for a in consts],
        out_specs=row_spec,
        out_shape=jax.ShapeDtypeStruct((n, D_MODEL), F32),
        scratch_shapes=[
            pltpu.VMEM((MIX_ROWS, 2 * STATE_W), F32),
            pltpu.VMEM((SUBLANES, STATE_W), F32),
            pltpu.VMEM((SUBLANES, STATE_W), F32),
            pltpu.VMEM((POOL_HIST_ROWS + MIX_ROWS, POOL_WIDTH), F32),
        ],
        compiler_params=pltpu.CompilerParams(
            dimension_semantics=("arbitrary",), vmem_limit_bytes=VMEM_LIMIT),
        name="mixer",
    )(x, *consts)


def _sort16_pairs():
    n, pairs, p = 16, [], 1
    while p < n:
        k = p
        while k >= 1:
            for j in range(k % p, n - k, 2 * k):
                for i in range(min(k, n - j - k)):
                    if (i + j) // (2 * p) == (i + j + k) // (2 * p):
                        pairs.append((i + j, i + j + k))
            k //= 2
        p *= 2
    return pairs


_SORT16 = _sort16_pairs()
N_TOP = PEER_TOPK + 1
SORTED_ROWS = -(-N_TOP // SUBLANES) * SUBLANES


def _top_sorted(s, out_ref):
    v = [s[SUBLANES * k:SUBLANES * (k + 1), :] for k in range(16)]
    for i, j in _SORT16:
        hi = jnp.maximum(v[i], v[j])
        lo = jnp.minimum(v[i], v[j])
        v[i], v[j] = hi, lo
    for i in range(N_TOP):
        head = v[0]
        m = jnp.max(head, axis=0, keepdims=True)
        out_ref[i:i + 1, :] = m
        if i + 1 < N_TOP:
            pop = head == m
            depth = N_TOP - i
            v = [jnp.where(pop, v[k + 1] if k + 1 < len(v) else NEG_INF, v[k])
                 for k in range(depth - 1)]


def _route_kernel(x_ref, gffn_ref, wqt_ref, k1_ref, k2_ref,
                  hb_ref, cnt_ref, e1_ref, rank_ref, w2_ref,
                  qt_ref, l1_ref, l2_ref):
    tm = x_ref.shape[0]
    h = _rms(x_ref[...], gffn_ref[...])
    hbt = h.T.astype(BF16)
    hb_ref[...] = _words(hbt)
    qt_ref[...] = _dot(wqt_ref[...], hbt)
    row = lax.broadcasted_iota(jnp.int32, (SUBLANES, tm), 0)

    def head(hd, carry):
        base = pl.multiple_of(hd * 2 * PEER_HALF, 2 * PEER_HALF)
        q1 = qt_ref[pl.ds(base, PEER_HALF), :].astype(BF16)
        q2 = qt_ref[pl.ds(base + PEER_HALF, PEER_HALF), :].astype(BF16)
        s1 = _dot(k1_ref[...], q1)
        s2 = _dot(k2_ref[...], q2)
        _top_sorted(s1, l1_ref)
        _top_sorted(s2, l2_ref)
        m1 = l1_ref[0:1, :]
        m2 = l2_ref[0:1, :]
        a = l1_ref[1:9, :]
        b = l2_ref[1:9, :]
        cands = [
            m1 + l2_ref[0:8, :],
            m1 + l2_ref[8:16, :],
            m2 + a,
            m2 + l1_ref[9:17, :],
            jnp.where(row < 7, l1_ref[1:2, :] + b, NEG_INF),
            jnp.where(row < 4, l1_ref[2:3, :] + b, NEG_INF),
            jnp.where(row < 3, l1_ref[3:4, :] + b, NEG_INF),
            jnp.where(row < 2, l1_ref[4:5, :] + b, NEG_INF),
            jnp.where((row >= 4) & (row < 7), l2_ref[1:2, :] + a, NEG_INF),
            jnp.where(row == 7, m1 + l2_ref[9:17, :], NEG_INF),
        ]
        tops = []
        for i in range(N_TOP):
            m = cands[0]
            for cnd in cands[1:]:
                m = jnp.maximum(m, cnd)
            m = jnp.max(m, axis=0, keepdims=True)
            tops.append(m)
            if i + 1 < N_TOP:
                cands = [jnp.where(cnd == m, NEG_INF, cnd) for cnd in cands]
        tau = 0.5 * (tops[PEER_TOPK - 1] + tops[PEER_TOPK])
        z = jnp.zeros_like(tau)
        for i in range(PEER_TOPK):
            z = z + jnp.exp(tops[i] - tops[0])
        theta = tau - s1
        cnt = jnp.zeros_like(s1)
        rank = jnp.zeros_like(s2)
        for j in range(PEER_TOPK):
            v2j = l2_ref[j:j + 1, :]
            cnt = jnp.where(v2j >= theta, j + 1.0, cnt)
            rank = jnp.where(v2j > s2, j + 1.0, rank)
        e1 = jnp.exp(s1 - m1)
        w2 = jnp.exp(s2 - m2) * (0.5 / z)
        outs = ((cnt_ref, cnt), (e1_ref, e1),
                (rank_ref, _words(rank.astype(BF16))), (w2_ref, _words(w2.astype(BF16))))
        for ref, val in outs:
            for lt in range(tm // LANES):
                ref[hd, lt] = val[:, lt * LANES:(lt + 1) * LANES]
        return carry

    lax.fori_loop(0, PEER_HEADS, head, 0)


def _route(x, gffn, wqt, k1, k2):
    n = x.shape[0]
    tm = ROUTE_TM
    def key_spec(rows):
        return pl.BlockSpec((PEER_HEADS, tm // LANES, rows, LANES), lambda i: (0, i, 0, 0))

    def key_shape(rows, dtype):
        return jax.ShapeDtypeStruct((PEER_HEADS, n // LANES, rows, LANES), dtype)

    return pl.pallas_call(
        _route_kernel,
        grid=(n // tm,),
        in_specs=[pl.BlockSpec((tm, D_MODEL), lambda i: (i, 0)),
                  _const_spec(gffn.shape), _const_spec(wqt.shape),
                  _const_spec(k1.shape), _const_spec(k2.shape)],
        out_specs=[pl.BlockSpec((D_MODEL // 2, tm), lambda i: (0, i)),
                   key_spec(PEER_N_KEYS), key_spec(PEER_N_KEYS),
                   key_spec(PEER_N_KEYS // 2), key_spec(PEER_N_KEYS // 2)],
        out_shape=[jax.ShapeDtypeStruct((D_MODEL // 2, n), jnp.uint32),
                   key_shape(PEER_N_KEYS, F32), key_shape(PEER_N_KEYS, F32),
                   key_shape(PEER_N_KEYS // 2, jnp.uint32),
                   key_shape(PEER_N_KEYS // 2, jnp.uint32)],
        scratch_shapes=[
            pltpu.VMEM((PEER_HEADS * 2 * PEER_HALF, tm), F32),
            pltpu.VMEM((SORTED_ROWS, tm), F32),
            pltpu.VMEM((SORTED_ROWS, tm), F32),
        ],
        compiler_params=pltpu.CompilerParams(
            dimension_semantics=("arbitrary",), vmem_limit_bytes=VMEM_LIMIT),
        name="peer_route",
    )(x, gffn, wqt, k1, k2)


def _packed_row(row, rows):
    tile = jnp.broadcast_to(row, (2 * SUBLANES, LANES)).astype(BF16)
    return jnp.tile(tile, (rows // tile.shape[0], 1))


def _peer_kernel(x_ref, hb_ref, cnt_ref, e1_ref, rank_ref, w2_ref, u_ref, vt_ref,
                 gfin_ref, o_ref, s_ref, a_ref, acc_ref, *, final_norm, e_tiles, n_tiles):
    g = pl.program_id(0)
    te = 2 * u_ref.shape[0]
    tm = hb_ref.shape[1]
    blocks = te // PEER_N_KEYS
    half_piece = PEER_PIECE // 2
    cur = g % 2
    prev = 1 - cur

    @pl.when(g == 0)
    def _():
        s_ref[...] = jnp.zeros_like(s_ref)
        a_ref[...] = jnp.zeros_like(a_ref)
        acc_ref[...] = jnp.zeros_like(acc_ref)

    tile2 = jnp.clip(g - 1, 0, n_tiles - 1)
    tile3 = jnp.clip(g - 2, 0, n_tiles - 1)
    i1_base = (tile2 % e_tiles) * blocks

    def gate_blocks(r):
        for sub in range(0, PEER_PIECE // PEER_N_KEYS, GATE_GROUP):
            ibs = [r * (PEER_PIECE // PEER_N_KEYS) + sub + k for k in range(GATE_GROUP)]
            for lt, part in itertools.product(range(tm // LANES),
                                              range(PEER_N_KEYS // GATE_ROWS)):
                cols = slice(lt * LANES, (lt + 1) * LANES)
                krows = slice(part * GATE_ROWS // 2, (part + 1) * GATE_ROWS // 2)
                gates = [None] * GATE_GROUP
                for hd in range(PEER_HEADS):
                    rank = _halves(rank_ref[hd, lt, krows, :])
                    w2 = _halves(w2_ref[hd, lt, krows, :])
                    for k, ib in enumerate(ibs):
                        i1 = i1_base + ib
                        cnt = _packed_row(cnt_ref[hd, lt, pl.ds(i1, 1), :], GATE_ROWS)
                        e1 = _packed_row(e1_ref[hd, lt, pl.ds(i1, 1), :], GATE_ROWS)
                        term = jnp.where(rank < cnt, w2 * e1, jnp.zeros((), BF16))
                        gates[k] = term if gates[k] is None else gates[k] + term
                for k, ib in enumerate(ibs):
                    lo = ib * PEER_N_KEYS + part * GATE_ROWS
                    bwrow = pl.ds(pl.multiple_of(lo // 2, GATE_ROWS // 2), GATE_ROWS // 2)
                    s = _halves(s_ref[prev, bwrow, cols])
                    act = s * (1.0 + lax.erf(s * INV_SQRT2))
                    a_ref[prev, bwrow, cols] = _words(act * gates[k])

    def piece(r, carry):
        wrows = pl.ds(pl.multiple_of(r * half_piece, half_piece), half_piece)
        acc_ref[...] += _dot(_halves(vt_ref[r]), _halves(a_ref[cur, wrows, :]))
        gate_blocks(r)
        s_ref[cur, wrows, :] = _words(
            _dot(_halves(u_ref[wrows, :]), _halves(hb_ref[...])).astype(BF16))
        return carry

    lax.fori_loop(0, te // PEER_PIECE, piece, 0)

    @pl.when((g >= 2) & (tile3 % e_tiles == e_tiles - 1))
    def _():
        out = x_ref[...] + acc_ref[...].T
        if final_norm:
            out = _rms(out, gfin_ref[...])
            o_ref[...] = pltpu.einshape("(tb)d->btd", out, b=SUBLANES)
        else:
            o_ref[...] = out
        acc_ref[...] = jnp.zeros_like(acc_ref)


def _peer(x, hb_w, cnt, e1, rank_w, w2_w, u_w, vt_w, gfin, final_norm):
    n = x.shape[0]
    tm, te = PEER_TM, PEER_TE
    e_tiles = PEER_N_EXPERTS // te
    n_tiles = (n // tm) * e_tiles
    last = n_tiles - 1

    def t1(g):
        return jnp.minimum(g, last)

    def t2(g):
        return jnp.clip(g - 1, 0, last)

    def t3(g):
        return jnp.clip(g - 2, 0, last)

    def key_spec(rows):
        return pl.BlockSpec((PEER_HEADS, tm // LANES, rows, LANES),
                            lambda g: (0, t2(g) // e_tiles, 0, 0),
                            pipeline_mode=pl.Buffered(1))

    out_row_spec = pl.BlockSpec((tm, D_MODEL), lambda g: (t3(g) // e_tiles, 0))
    return pl.pallas_call(
        functools.partial(_peer_kernel, final_norm=final_norm, e_tiles=e_tiles,
                          n_tiles=n_tiles),
        grid=(n_tiles + 2,),
        in_specs=[pl.BlockSpec((tm, D_MODEL), lambda g: (t3(g) // e_tiles, 0),
                               pipeline_mode=pl.Buffered(1)),
                  pl.BlockSpec((D_MODEL // 2, tm), lambda g: (0, t1(g) // e_tiles)),
                  key_spec(PEER_N_KEYS), key_spec(PEER_N_KEYS),
                  key_spec(PEER_N_KEYS // 2), key_spec(PEER_N_KEYS // 2),
                  pl.BlockSpec((te // 2, D_MODEL), lambda g: (t1(g) % e_tiles, 0)),
                  pl.BlockSpec((te // PEER_PIECE, D_MODEL // 2, PEER_PIECE),
                               lambda g: (t3(g) % e_tiles, 0, 0)),
                  pl.BlockSpec(gfin.shape, lambda g: (0, 0))],
        out_specs=(pl.BlockSpec((SUBLANES, tm // SUBLANES, D_MODEL),
                                lambda g: (0, t3(g) // e_tiles, 0))
                   if final_norm else out_row_spec),
        out_shape=(jax.ShapeDtypeStruct((SUBLANES, n // SUBLANES, D_MODEL), F32)
                   if final_norm else jax.ShapeDtypeStruct((n, D_MODEL), F32)),
        scratch_shapes=[
            pltpu.VMEM((2, te // 2, tm), jnp.uint32),
            pltpu.VMEM((2, te // 2, tm), jnp.uint32),
            pltpu.VMEM((D_MODEL, tm), F32),
        ],
        compiler_params=pltpu.CompilerParams(
            dimension_semantics=("arbitrary",), vmem_limit_bytes=VMEM_LIMIT),
        name="peer_dense",
    )(x, hb_w, cnt, e1, rank_w, w2_w, u_w, vt_w, gfin)


def _tables_kernel(u_ref, v_ref, uw_ref, vw_ref):
    uw_ref[...] = _words(u_ref[0].astype(BF16))
    vw_ref[0] = _words(v_ref[0].T.astype(BF16))


def _expert_tables(u, v, layer):
    _, e, d = u.shape
    rows = PEER_PIECE
    table_spec = pl.BlockSpec((1, rows, d), lambda i: (layer, i, 0))
    return pl.pallas_call(
        _tables_kernel,
        grid=(e // rows,),
        in_specs=[table_spec, table_spec],
        out_specs=[pl.BlockSpec((rows // 2, d), lambda i: (i, 0)),
                   pl.BlockSpec((1, d // 2, rows), lambda i: (i, 0, 0))],
        out_shape=[jax.ShapeDtypeStruct((e // 2, d), jnp.uint32),
                   jax.ShapeDtypeStruct((e // rows, d // 2, rows), jnp.uint32)],
        compiler_params=pltpu.CompilerParams(
            dimension_semantics=("arbitrary",), vmem_limit_bytes=VMEM_LIMIT),
        name="expert_tables",
    )(u, v)


def _block_diag(blocks):
    g, r, c = blocks.shape
    eye = jnp.eye(g, dtype=blocks.dtype)
    return (blocks[:, :, None, :] * eye[:, None, :, None]).reshape(g * r, g * c)


def kernel(x, norm_mix, w_in, a_re, a_im, log_dt, b_re, b_im, c_re, c_im, d_skip, w_glu, b_glu, w_pool, pool_scale, g_out_ssm, g_out_pool, w_out, norm_ffn, w_q, k1, k2, u_experts, v_experts, norm_final):
    bsz, seq, dm = x.shape
    assert (bsz, dm) == (SUBLANES, D_MODEL) and seq % (MIX_ROWS // SUBLANES) == 0
    depth = w_in.shape[0]
    n = bsz * seq
    row = lambda a: a.reshape(1, -1).astype(F32)

    abr, abi, btr, bti = _discretise(a_re, a_im, log_dt, b_re, b_im)
    xt = x
    gfin = row(norm_final)

    for i in range(depth):
        hg = SSM_GROUPS // 2
        bmat = jnp.stack([_block_diag(b[lo:lo + hg]) for b in (btr[i], bti[i])
                          for lo in (0, hg)]).astype(BF16)
        cre, cim = (jnp.stack([_block_diag(jnp.transpose(c[lo:lo + hg], (0, 2, 1)))
                               for lo in (0, hg)]).astype(BF16)
                    for c in (c_re[i], c_im[i]))
        xt = _mixer(
            xt, row(norm_mix[i]), w_in[i].astype(BF16), bmat,
            abr[i].reshape(1, STATE_W), abi[i].reshape(1, STATE_W), cre, cim,
            row(d_skip[i]), w_glu[i].astype(BF16), row(b_glu[i]),
            _block_diag(w_pool[i]).astype(BF16), row(pool_scale[i]),
            row(g_out_ssm[i]), row(g_out_pool[i]), w_out[i].astype(BF16))
        hb, cnt, e1, rank, w2 = _route(
            xt, row(norm_ffn[i]), jnp.transpose(w_q[i]).astype(BF16),
            k1[i].astype(BF16), k2[i].astype(BF16))
        u_w, vt_w = _expert_tables(u_experts, v_experts, i)
        xt = _peer(xt, hb, cnt, e1, rank, w2, u_w, vt_w, gfin,
                   final_norm=(i == depth - 1))
    return xt
```

```python
import functools

import jax
import jax.numpy as jnp
from jax import lax
from jax.experimental import pallas as pl
from jax.experimental.pallas import tpu as pltpu

F32 = jnp.float32
BF16 = jnp.bfloat16

D_MODEL = 1024
SSM_WIDTH = 512
POOL_WIDTH = 512
SSM_GROUP = 16
SSM_GROUPS = 32
SSM_STATE = 64
STATE_W = SSM_GROUPS * SSM_STATE
HALF_SSM = SSM_WIDTH // 2
HALF_STATE = STATE_W // 2
POOL_WINDOWS = (2, 4, 8, 16)
POOL_GROUP_WIDTH = 128
PEER_HEADS = 8
PEER_N_KEYS = 128
PEER_N_EXPERTS = PEER_N_KEYS * PEER_N_KEYS
PEER_HALF = 128
PEER_TOPK = 16
RMS_EPS = 1e-6

SUBLANES = 8
LANES = 128
MXU_DEPTH = 256
MIX_ROWS = 512
POOL_HIST_ROWS = 128
ROUTE_TM = 512
PEER_TM = 512
PEER_TE = 2048
PEER_PIECE = 1024
GATE_ROWS = 64
VMEM_LIMIT = 60 * 1024 * 1024

NEG_INF = float("-inf")
INV_SQRT2 = 0.7071067811865476


def _rms(x, g):
    return x * lax.rsqrt(jnp.mean(x * x, axis=-1, keepdims=True) + RMS_EPS) * g


def _gelu(x):
    return 0.5 * x * (1.0 + lax.erf(x * INV_SQRT2))


def _dot(a, b):
    return jnp.dot(a, b, preferred_element_type=F32)


def _words(x):
    return pltpu.bitcast(x, jnp.uint32)


def _halves(w):
    return pltpu.bitcast(w, BF16)


def _disc_kernel(are_ref, aim_ref, ldt_ref, bre_ref, bim_ref,
                 abr_ref, abi_ref, btr_ref, bti_ref):
    lam_re = are_ref[...]
    lam_im = aim_ref[...]
    dt = jnp.exp(ldt_ref[...])
    decay = jnp.exp(lam_re * dt)
    abar_re = decay * jnp.cos(lam_im * dt)
    abar_im = decay * jnp.sin(lam_im * dt)
    inv_den = 1.0 / (lam_re * lam_re + lam_im * lam_im)
    num_re = abar_re - 1.0
    zoh_re = (num_re * lam_re + abar_im * lam_im) * inv_den
    zoh_im = (abar_im * lam_re - num_re * lam_im) * inv_den
    b_re = bre_ref[...]
    b_im = bim_ref[...]
    abr_ref[...] = abar_re
    abi_ref[...] = abar_im
    btr_ref[...] = zoh_re * b_re - zoh_im * b_im
    bti_ref[...] = zoh_re * b_im + zoh_im * b_re


def _discretise(a_re, a_im, log_dt, b_re, b_im):
    nl = a_re.shape[0]
    rows = nl * SSM_GROUPS * SSM_GROUP
    shp = (nl, SSM_GROUPS, SSM_GROUP, SSM_STATE)

    def rep(a):
        return jnp.broadcast_to(a[:, :, None, :], shp).reshape(rows, SSM_STATE)

    ldt = jnp.broadcast_to(log_dt[:, :, None, None], shp).reshape(rows, SSM_STATE)
    bre = jnp.transpose(b_re, (0, 1, 3, 2)).reshape(rows, SSM_STATE)
    bim = jnp.transpose(b_im, (0, 1, 3, 2)).reshape(rows, SSM_STATE)
    out = jax.ShapeDtypeStruct((rows, SSM_STATE), F32)
    abr, abi, btr, bti = pl.pallas_call(
        _disc_kernel, out_shape=(out, out, out, out), name="s5_discretise",
    )(rep(a_re), rep(a_im), ldt, bre, bim)
    abr = abr.reshape(shp)[:, :, 0, :].reshape(nl, STATE_W)
    abi = abi.reshape(shp)[:, :, 0, :].reshape(nl, STATE_W)
    return abr, abi, btr.reshape(shp), bti.reshape(shp)


def _mixer_kernel(x_ref, gmix_ref, win_ref, bmat_ref, are_ref, aim_ref,
                  cre_ref, cim_ref, dskip_ref, wglu_ref, bglu_ref, wpool_ref,
                  pscale_ref, gssm_ref, gpool_ref, wout_ref, o_ref,
                  st_ref, sre_ref, sim_ref, ext_ref, *, batch_major_in):
    c = pl.program_id(0)
    rows = o_ref.shape[0]
    steps = rows // SUBLANES

    @pl.when(c == 0)
    def _():
        sre_ref[...] = jnp.zeros_like(sre_ref)
        sim_ref[...] = jnp.zeros_like(sim_ref)
        ext_ref[0:POOL_HIST_ROWS, :] = jnp.zeros((POOL_HIST_ROWS, POOL_WIDTH), F32)

    if batch_major_in:
        xr = pltpu.einshape("btd->(tb)d", x_ref[...])
    else:
        xr = x_ref[...]
    hn = _rms(xr, gmix_ref[...])
    proj = _dot(hn.astype(BF16), win_ref[...])
    u_ssm = proj[:, :SSM_WIDTH]
    u_pool = proj[:, SSM_WIDTH:]

    u_b = u_ssm.astype(BF16)
    for part in range(2):
        for hf in range(2):
            lo = part * STATE_W + hf * HALF_STATE
            st_ref[:, lo:lo + HALF_STATE] = _dot(
                u_b[:, hf * HALF_SSM:(hf + 1) * HALF_SSM], bmat_ref[2 * part + hf])
    a_re = jnp.broadcast_to(are_ref[...], (SUBLANES, STATE_W))
    a_im = jnp.broadcast_to(aim_ref[...], (SUBLANES, STATE_W))

    def step(t, carry):
        s_re, s_im = carry
        r = pl.multiple_of(t * SUBLANES, SUBLANES)
        in_re = st_ref[pl.ds(r, SUBLANES), 0:STATE_W]
        in_im = st_ref[pl.ds(r, SUBLANES), STATE_W:2 * STATE_W]
        n_re = a_re * s_re - a_im * s_im + in_re
        n_im = a_re * s_im + a_im * s_re + in_im
        st_ref[pl.ds(r, SUBLANES), 0:STATE_W] = n_re
        st_ref[pl.ds(r, SUBLANES), STATE_W:2 * STATE_W] = n_im
        return n_re, n_im

    s_re, s_im = lax.fori_loop(0, steps, step, (sre_ref[...], sim_ref[...]))
    sre_ref[...] = s_re
    sim_ref[...] = s_im

    y = []
    for hf in range(2):
        lo = hf * HALF_STATE
        y.append(_dot(st_ref[:, lo:lo + HALF_STATE].astype(BF16), cre_ref[hf])
                 - _dot(st_ref[:, STATE_W + lo:STATE_W + lo + HALF_STATE].astype(BF16),
                        cim_ref[hf]))
    y = jnp.concatenate(y, axis=1) + dskip_ref[...] * u_ssm
    y = _gelu(y)
    y = y * jax.nn.sigmoid(_dot(y.astype(BF16), wglu_ref[...]) + bglu_ref[...])
    ssm_n = _rms(y, gssm_ref[...])

    ext_ref[POOL_HIST_ROWS:, :] = u_pool
    t_idx = c * steps + jnp.right_shift(
        lax.broadcasted_iota(jnp.int32, (rows, POOL_GROUP_WIDTH), 0),
        SUBLANES.bit_length() - 1)
    pooled = []
    for gi, win in enumerate(POOL_WINDOWS):
        lo = gi * POOL_GROUP_WIDTH
        hi = lo + POOL_GROUP_WIDTH
        acc = ext_ref[POOL_HIST_ROWS:, lo:hi]
        for k in range(1, win):
            off = POOL_HIST_ROWS - SUBLANES * k
            acc = acc + ext_ref[off:off + rows, lo:hi]
        count = jnp.minimum(t_idx + 1, win).astype(F32)
        pooled.append(acc / count - ext_ref[POOL_HIST_ROWS:, lo:hi])
    ext_ref[0:POOL_HIST_ROWS, :] = ext_ref[rows:rows + POOL_HIST_ROWS, :]
    pooled = jnp.concatenate(pooled, axis=1)
    y_pool = _dot(pooled.astype(BF16), wpool_ref[...]) * pscale_ref[...]
    pool_n = _rms(y_pool, gpool_ref[...])

    res = (_dot(ssm_n.astype(BF16), wout_ref[0:SSM_WIDTH, :])
           + _dot(pool_n.astype(BF16), wout_ref[SSM_WIDTH:, :]))
    o_ref[...] = xr + res


def _const_spec(shape):
    zeros = (0,) * len(shape)
    return pl.BlockSpec(shape, lambda *_: zeros, pipeline_mode=pl.Buffered(1))


def _mixer(x, gmix, win, bmat, are, aim, cre, cim, dskip, wglu, bglu, wpool,
           pscale, gssm, gpool, wout):
    batch_major_in = x.ndim == 3
    n = x.shape[0] * x.shape[1] if batch_major_in else x.shape[0]
    consts = (gmix, win, bmat, are, aim, cre, cim, dskip, wglu, bglu, wpool,
              pscale, gssm, gpool, wout)
    row_spec = pl.BlockSpec((MIX_ROWS, D_MODEL), lambda c: (c, 0))
    x_spec = (pl.BlockSpec((SUBLANES, MIX_ROWS // SUBLANES, D_MODEL), lambda c: (0, c, 0))
              if batch_major_in else row_spec)
    return pl.pallas_call(
        functools.partial(_mixer_kernel, batch_major_in=batch_major_in),
        grid=(n // MIX_ROWS,),
        in_specs=[x_spec] + [_const_spec(a.shape) for a in consts],
        out_specs=row_spec,
        out_shape=jax.ShapeDtypeStruct((n, D_MODEL), F32),
        scratch_shapes=[
            pltpu.VMEM((MIX_ROWS, 2 * STATE_W), F32),
            pltpu.VMEM((SUBLANES, STATE_W), F32),
            pltpu.VMEM((SUBLANES, STATE_W), F32),
            pltpu.VMEM((POOL_HIST_ROWS + MIX_ROWS, POOL_WIDTH), F32),
        ],
        compiler_params=pltpu.CompilerParams(
            dimension_semantics=("arbitrary",), vmem_limit_bytes=VMEM_LIMIT),
        name="mixer",
    )(x, *consts)


def _sort16_pairs():
    n, pairs, p = 16, [], 1
    while p < n:
        k = p
        while k >= 1:
            for j in range(k % p, n - k, 2 * k):
                for i in range(min(k, n - j - k)):
                    if (i + j) // (2 * p) == (i + j + k) // (2 * p):
                        pairs.append((i + j, i + j + k))
            k //= 2
        p *= 2
    return pairs


_SORT16 = _sort16_pairs()
N_TOP = PEER_TOPK + 1
SORTED_ROWS = -(-N_TOP // SUBLANES) * SUBLANES


def _top_sorted(s, out_ref):
    v = [s[SUBLANES * k:SUBLANES * (k + 1), :] for k in range(16)]
    for i, j in _SORT16:
        hi = jnp.maximum(v[i], v[j])
        lo = jnp.minimum(v[i], v[j])
        v[i], v[j] = hi, lo
    for i in range(N_TOP):
        head = v[0]
        m = jnp.max(head, axis=0, keepdims=True)
        out_ref[i:i + 1, :] = m
        if i + 1 < N_TOP:
            pop = head == m
            depth = N_TOP - i
            v = [jnp.where(pop, v[k + 1] if k + 1 < len(v) else NEG_INF, v[k])
                 for k in range(depth - 1)]


def _route_kernel(x_ref, gffn_ref, wqt_ref, k1_ref, k2_ref,
                  hb_ref, cnt_ref, e1_ref, rank_ref, w2_ref,
                  qt_ref, l1_ref, l2_ref):
    tm = x_ref.shape[0]
    h = _rms(x_ref[...], gffn_ref[...])
    hbt = h.T.astype(BF16)
    hb_ref[...] = _words(hbt)
    qt_ref[...] = _dot(wqt_ref[...], hbt)
    row = lax.broadcasted_iota(jnp.int32, (SUBLANES, tm), 0)

    def head(hd, carry):
        base = pl.multiple_of(hd * 2 * PEER_HALF, 2 * PEER_HALF)
        q1 = qt_ref[pl.ds(base, PEER_HALF), :].astype(BF16)
        q2 = qt_ref[pl.ds(base + PEER_HALF, PEER_HALF), :].astype(BF16)
        s1 = _dot(k1_ref[...], q1)
        s2 = _dot(k2_ref[...], q2)
        _top_sorted(s1, l1_ref)
        _top_sorted(s2, l2_ref)
        m1 = l1_ref[0:1, :]
        m2 = l2_ref[0:1, :]
        a = l1_ref[1:9, :]
        b = l2_ref[1:9, :]
        cands = [
            m1 + l2_ref[0:8, :],
            m1 + l2_ref[8:16, :],
            m2 + a,
            m2 + l1_ref[9:17, :],
            jnp.where(row < 7, l1_ref[1:2, :] + b, NEG_INF),
            jnp.where(row < 4, l1_ref[2:3, :] + b, NEG_INF),
            jnp.where(row < 3, l1_ref[3:4, :] + b, NEG_INF),
            jnp.where(row < 2, l1_ref[4:5, :] + b, NEG_INF),
            jnp.where((row >= 4) & (row < 7), l2_ref[1:2, :] + a, NEG_INF),
            jnp.where(row == 7, m1 + l2_ref[9:17, :], NEG_INF),
        ]
        tops = []
        for i in range(N_TOP):
            m = cands[0]
            for cnd in cands[1:]:
                m = jnp.maximum(m, cnd)
            m = jnp.max(m, axis=0, keepdims=True)
            tops.append(m)
            if i + 1 < N_TOP:
                cands = [jnp.where(cnd == m, NEG_INF, cnd) for cnd in cands]
        tau = 0.5 * (tops[PEER_TOPK - 1] + tops[PEER_TOPK])
        z = jnp.zeros_like(tau)
        for i in range(PEER_TOPK):
            z = z + jnp.exp(tops[i] - tops[0])
        theta = tau - s1
        cnt = jnp.zeros_like(s1)
        rank = jnp.zeros_like(s2)
        for j in range(PEER_TOPK):
            v2j = l2_ref[j:j + 1, :]
            cnt = jnp.where(v2j >= theta, j + 1.0, cnt)
            rank = jnp.where(v2j > s2, j + 1.0, rank)
        e1 = jnp.exp(s1 - m1)
        w2 = jnp.exp(s2 - m2) * (0.5 / z)
        outs = ((cnt_ref, cnt), (e1_ref, e1),
                (rank_ref, _words(rank.astype(BF16))), (w2_ref, _words(w2.astype(BF16))))
        for ref, val in outs:
            for lt in range(tm // LANES):
                ref[hd, lt] = val[:, lt * LANES:(lt + 1) * LANES]
        return carry

    lax.fori_loop(0, PEER_HEADS, head, 0)


def _route(x, gffn, wqt, k1, k2):
    n = x.shape[0]
    tm = ROUTE_TM
    def key_spec(rows):
        return pl.BlockSpec((PEER_HEADS, tm // LANES, rows, LANES), lambda i: (0, i, 0, 0))

    def key_shape(rows, dtype):
        return jax.ShapeDtypeStruct((PEER_HEADS, n // LANES, rows, LANES), dtype)

    return pl.pallas_call(
        _route_kernel,
        grid=(n // tm,),
        in_specs=[pl.BlockSpec((tm, D_MODEL), lambda i: (i, 0)),
                  _const_spec(gffn.shape), _const_spec(wqt.shape),
                  _const_spec(k1.shape), _const_spec(k2.shape)],
        out_specs=[pl.BlockSpec((D_MODEL // 2, tm), lambda i: (0, i)),
                   key_spec(PEER_N_KEYS), key_spec(PEER_N_KEYS),
                   key_spec(PEER_N_KEYS // 2), key_spec(PEER_N_KEYS // 2)],
        out_shape=[jax.ShapeDtypeStruct((D_MODEL // 2, n), jnp.uint32),
                   key_shape(PEER_N_KEYS, F32), key_shape(PEER_N_KEYS, F32),
                   key_shape(PEER_N_KEYS // 2, jnp.uint32),
                   key_shape(PEER_N_KEYS // 2, jnp.uint32)],
        scratch_shapes=[
            pltpu.VMEM((PEER_HEADS * 2 * PEER_HALF, tm), F32),
            pltpu.VMEM((SORTED_ROWS, tm), F32),
            pltpu.VMEM((SORTED_ROWS, tm), F32),
        ],
        compiler_params=pltpu.CompilerParams(
            dimension_semantics=("arbitrary",), vmem_limit_bytes=VMEM_LIMIT),
        name="peer_route",
    )(x, gffn, wqt, k1, k2)


def _packed_rows(row, rows):
    tile = jnp.broadcast_to(row, (2 * SUBLANES, LANES)).astype(BF16)
    return jnp.tile(tile, (rows // tile.shape[0], 1))


def _peer_kernel(x_ref, hb_ref, cnt_ref, e1_ref, rank_ref, w2_ref, u_ref, vt_ref,
                 gfin_ref, o_ref, s_ref, a_ref, acc_ref, *, final_norm, e_tiles, n_tiles):
    g = pl.program_id(0)
    te = 2 * u_ref.shape[0]
    tm = hb_ref.shape[1]
    blocks = te // PEER_N_KEYS
    half_piece = PEER_PIECE // 2
    cur = g % 2
    prev = 1 - cur

    @pl.when(g == 0)
    def _():
        s_ref[...] = jnp.zeros_like(s_ref)
        a_ref[...] = jnp.zeros_like(a_ref)
        acc_ref[...] = jnp.zeros_like(acc_ref)

    tile2 = jnp.clip(g - 1, 0, n_tiles - 1)
    tile3 = jnp.clip(g - 2, 0, n_tiles - 1)
    i1_base = (tile2 % e_tiles) * blocks

    def gate_blocks(r):
        for sub in range(PEER_PIECE // PEER_N_KEYS):
            ib = r * (PEER_PIECE // PEER_N_KEYS) + sub
            i1 = i1_base + ib
            for lt in range(tm // LANES):
                cols = slice(lt * LANES, (lt + 1) * LANES)
                for part in range(PEER_N_KEYS // GATE_ROWS):
                    krows = slice(part * GATE_ROWS // 2, (part + 1) * GATE_ROWS // 2)
                    gate = None
                    for hd in range(PEER_HEADS):
                        cnt = _packed_rows(cnt_ref[hd, lt, pl.ds(i1, 1), :], GATE_ROWS)
                        e1 = _packed_rows(e1_ref[hd, lt, pl.ds(i1, 1), :], GATE_ROWS)
                        term = jnp.where(_halves(rank_ref[hd, lt, krows, :]) < cnt,
                                         _halves(w2_ref[hd, lt, krows, :]) * e1,
                                         jnp.zeros((), BF16))
                        gate = term if gate is None else gate + term
                    lo = ib * PEER_N_KEYS + part * GATE_ROWS
                    bwrow = pl.ds(pl.multiple_of(lo // 2, GATE_ROWS // 2), GATE_ROWS // 2)
                    s = _halves(s_ref[prev, bwrow, cols])
                    act = s * (1.0 + lax.erf(s * INV_SQRT2))
                    a_ref[prev, bwrow, cols] = _words(act * gate)

    def piece(r, carry):
        wrows = pl.ds(pl.multiple_of(r * half_piece, half_piece), half_piece)
        acc_ref[...] += _dot(_halves(vt_ref[r]), _halves(a_ref[cur, wrows, :]))
        gate_blocks(r)
        s_ref[cur, wrows, :] = _words(
            _dot(_halves(u_ref[wrows, :]), _halves(hb_ref[...])).astype(BF16))
        return carry

    lax.fori_loop(0, te // PEER_PIECE, piece, 0)

    @pl.when((g >= 2) & (tile3 % e_tiles == e_tiles - 1))
    def _():
        out = x_ref[...] + acc_ref[...].T
        if final_norm:
            out = _rms(out, gfin_ref[...])
            o_ref[...] = pltpu.einshape("(tb)d->btd", out, b=SUBLANES)
        else:
            o_ref[...] = out
        acc_ref[...] = jnp.zeros_like(acc_ref)


def _peer(x, hb_w, cnt, e1, rank_w, w2_w, u_w, vt_w, gfin, final_norm):
    n = x.shape[0]
    tm, te = PEER_TM, PEER_TE
    e_tiles = PEER_N_EXPERTS // te
    n_tiles = (n // tm) * e_tiles
    last = n_tiles - 1

    def t1(g):
        return jnp.minimum(g, last)

    def t2(g):
        return jnp.clip(g - 1, 0, last)

    def t3(g):
        return jnp.clip(g - 2, 0, last)

    def key_spec(rows):
        return pl.BlockSpec((PEER_HEADS, tm // LANES, rows, LANES),
                            lambda g: (0, t2(g) // e_tiles, 0, 0),
                            pipeline_mode=pl.Buffered(1))

    out_row_spec = pl.BlockSpec((tm, D_MODEL), lambda g: (t3(g) // e_tiles, 0))
    return pl.pallas_call(
        functools.partial(_peer_kernel, final_norm=final_norm, e_tiles=e_tiles,
                          n_tiles=n_tiles),
        grid=(n_tiles + 2,),
        in_specs=[pl.BlockSpec((tm, D_MODEL), lambda g: (t3(g) // e_tiles, 0),
                               pipeline_mode=pl.Buffered(1)),
                  pl.BlockSpec((D_MODEL // 2, tm), lambda g: (0, t1(g) // e_tiles)),
                  key_spec(PEER_N_KEYS), key_spec(PEER_N_KEYS),
                  key_spec(PEER_N_KEYS // 2), key_spec(PEER_N_KEYS // 2),
                  pl.BlockSpec((te // 2, D_MODEL), lambda g: (t1(g) % e_tiles, 0)),
                  pl.BlockSpec((te // PEER_PIECE, D_MODEL // 2, PEER_PIECE),
                               lambda g: (t3(g) % e_tiles, 0, 0)),
                  pl.BlockSpec(gfin.shape, lambda g: (0, 0))],
        out_specs=(pl.BlockSpec((SUBLANES, tm // SUBLANES, D_MODEL),
                                lambda g: (0, t3(g) // e_tiles, 0))
                   if final_norm else out_row_spec),
        out_shape=(jax.ShapeDtypeStruct((SUBLANES, n // SUBLANES, D_MODEL), F32)
                   if final_norm else jax.ShapeDtypeStruct((n, D_MODEL), F32)),
        scratch_shapes=[
            pltpu.VMEM((2, te // 2, tm), jnp.uint32),
            pltpu.VMEM((2, te // 2, tm), jnp.uint32),
            pltpu.VMEM((D_MODEL, tm), F32),
        ],
        compiler_params=pltpu.CompilerParams(
            dimension_semantics=("arbitrary",), vmem_limit_bytes=VMEM_LIMIT),
        name="peer_dense",
    )(x, hb_w, cnt, e1, rank_w, w2_w, u_w, vt_w, gfin)


def _tables_kernel(u_ref, v_ref, uw_ref, vw_ref):
    uw_ref[...] = _words(u_ref[0].astype(BF16))
    vw_ref[0] = _words(v_ref[0].T.astype(BF16))


def _expert_tables(u, v, layer):
    _, e, d = u.shape
    rows = PEER_PIECE
    table_spec = pl.BlockSpec((1, rows, d), lambda i: (layer, i, 0))
    return pl.pallas_call(
        _tables_kernel,
        grid=(e // rows,),
        in_specs=[table_spec, table_spec],
        out_specs=[pl.BlockSpec((rows // 2, d), lambda i: (i, 0)),
                   pl.BlockSpec((1, d // 2, rows), lambda i: (i, 0, 0))],
        out_shape=[jax.ShapeDtypeStruct((e // 2, d), jnp.uint32),
                   jax.ShapeDtypeStruct((e // rows, d // 2, rows), jnp.uint32)],
        compiler_params=pltpu.CompilerParams(
            dimension_semantics=("arbitrary",), vmem_limit_bytes=VMEM_LIMIT),
        name="expert_tables",
    )(u, v)


def _block_diag(blocks):
    g, r, c = blocks.shape
    eye = jnp.eye(g, dtype=blocks.dtype)
    return (blocks[:, :, None, :] * eye[:, None, :, None]).reshape(g * r, g * c)


def kernel(x, norm_mix, w_in, a_re, a_im, log_dt, b_re, b_im, c_re, c_im, d_skip, w_glu, b_glu, w_pool, pool_scale, g_out_ssm, g_out_pool, w_out, norm_ffn, w_q, k1, k2, u_experts, v_experts, norm_final):
    bsz, seq, dm = x.shape
    assert (bsz, dm) == (SUBLANES, D_MODEL) and seq % (MIX_ROWS // SUBLANES) == 0
    depth = w_in.shape[0]
    n = bsz * seq
    row = lambda a: a.reshape(1, -1).astype(F32)

    abr, abi, btr, bti = _discretise(a_re, a_im, log_dt, b_re, b_im)
    xt = x
    gfin = row(norm_final)

    for i in range(depth):
        hg = SSM_GROUPS // 2
        bmat = jnp.stack([_block_diag(b[lo:lo + hg]) for b in (btr[i], bti[i])
                          for lo in (0, hg)]).astype(BF16)
        cre, cim = (jnp.stack([_block_diag(jnp.transpose(c[lo:lo + hg], (0, 2, 1)))
                               for lo in (0, hg)]).astype(BF16)
                    for c in (c_re[i], c_im[i]))
        xt = _mixer(
            xt, row(norm_mix[i]), w_in[i].astype(BF16), bmat,
            abr[i].reshape(1, STATE_W), abi[i].reshape(1, STATE_W), cre, cim,
            row(d_skip[i]), w_glu[i].astype(BF16), row(b_glu[i]),
            _block_diag(w_pool[i]).astype(BF16), row(pool_scale[i]),
            row(g_out_ssm[i]), row(g_out_pool[i]), w_out[i].astype(BF16))
        hb, cnt, e1, rank, w2 = _route(
            xt, row(norm_ffn[i]), jnp.transpose(w_q[i]).astype(BF16),
            k1[i].astype(BF16), k2[i].astype(BF16))
        u_w, vt_w = _expert_tables(u_experts, v_experts, i)
        xt = _peer(xt, hb, cnt, e1, rank, w2, u_w, vt_w, gfin,
                   final_norm=(i == depth - 1))
    return xt
```

```python
import functools

import jax
import jax.numpy as jnp
from jax import lax
from jax.experimental import pallas as pl
from jax.experimental.pallas import tpu as pltpu

F32 = jnp.float32
BF16 = jnp.bfloat16

D_MODEL = 1024
SSM_WIDTH = 512
POOL_WIDTH = 512
SSM_GROUP = 16
SSM_GROUPS = 32
SSM_STATE = 64
STATE_W = SSM_GROUPS * SSM_STATE
HALF_SSM = SSM_WIDTH // 2
HALF_STATE = STATE_W // 2
POOL_WINDOWS = (2, 4, 8, 16)
POOL_GROUP_WIDTH = 128
PEER_HEADS = 8
PEER_N_KEYS = 128
PEER_N_EXPERTS = PEER_N_KEYS * PEER_N_KEYS
PEER_HALF = 128
PEER_TOPK = 16
RMS_EPS = 1e-6

SUBLANES = 8
LANES = 128
MXU_DEPTH = 256
MIX_ROWS = 512
POOL_HIST_ROWS = 128
ROUTE_TM = 512
PEER_TM = 512
PEER_TE = 2048
PEER_PIECE = 1024
GATE_ROWS = 64
VMEM_LIMIT = 60 * 1024 * 1024

NEG_INF = float("-inf")
INV_SQRT2 = 0.7071067811865476


def _rms(x, g):
    return x * lax.rsqrt(jnp.mean(x * x, axis=-1, keepdims=True) + RMS_EPS) * g


def _gelu(x):
    return 0.5 * x * (1.0 + lax.erf(x * INV_SQRT2))


def _dot(a, b):
    return jnp.dot(a, b, preferred_element_type=F32)


def _words(x):
    return pltpu.bitcast(x, jnp.uint32)


def _halves(w):
    return pltpu.bitcast(w, BF16)


def _disc_kernel(are_ref, aim_ref, ldt_ref, bre_ref, bim_ref,
                 abr_ref, abi_ref, btr_ref, bti_ref):
    lam_re = are_ref[...]
    lam_im = aim_ref[...]
    dt = jnp.exp(ldt_ref[...])
    decay = jnp.exp(lam_re * dt)
    abar_re = decay * jnp.cos(lam_im * dt)
    abar_im = decay * jnp.sin(lam_im * dt)
    inv_den = 1.0 / (lam_re * lam_re + lam_im * lam_im)
    num_re = abar_re - 1.0
    zoh_re = (num_re * lam_re + abar_im * lam_im) * inv_den
    zoh_im = (abar_im * lam_re - num_re * lam_im) * inv_den
    b_re = bre_ref[...]
    b_im = bim_ref[...]
    abr_ref[...] = abar_re
    abi_ref[...] = abar_im
    btr_ref[...] = zoh_re * b_re - zoh_im * b_im
    bti_ref[...] = zoh_re * b_im + zoh_im * b_re


def _discretise(a_re, a_im, log_dt, b_re, b_im):
    nl = a_re.shape[0]
    rows = nl * SSM_GROUPS * SSM_GROUP
    shp = (nl, SSM_GROUPS, SSM_GROUP, SSM_STATE)

    def rep(a):
        return jnp.broadcast_to(a[:, :, None, :], shp).reshape(rows, SSM_STATE)

    ldt = jnp.broadcast_to(log_dt[:, :, None, None], shp).reshape(rows, SSM_STATE)
    bre = jnp.transpose(b_re, (0, 1, 3, 2)).reshape(rows, SSM_STATE)
    bim = jnp.transpose(b_im, (0, 1, 3, 2)).reshape(rows, SSM_STATE)
    out = jax.ShapeDtypeStruct((rows, SSM_STATE), F32)
    abr, abi, btr, bti = pl.pallas_call(
        _disc_kernel, out_shape=(out, out, out, out), name="s5_discretise",
    )(rep(a_re), rep(a_im), ldt, bre, bim)
    abr = abr.reshape(shp)[:, :, 0, :].reshape(nl, STATE_W)
    abi = abi.reshape(shp)[:, :, 0, :].reshape(nl, STATE_W)
    return abr, abi, btr.reshape(shp), bti.reshape(shp)


def _mixer_kernel(x_ref, gmix_ref, win_ref, bmat_ref, are_ref, aim_ref,
                  cre_ref, cim_ref, dskip_ref, wglu_ref, bglu_ref, wpool_ref,
                  pscale_ref, gssm_ref, gpool_ref, wout_ref, o_ref,
                  st_ref, sre_ref, sim_ref, ext_ref, *, batch_major_in):
    c = pl.program_id(0)
    rows = o_ref.shape[0]
    steps = rows // SUBLANES

    @pl.when(c == 0)
    def _():
        sre_ref[...] = jnp.zeros_like(sre_ref)
        sim_ref[...] = jnp.zeros_like(sim_ref)
        ext_ref[0:POOL_HIST_ROWS, :] = jnp.zeros((POOL_HIST_ROWS, POOL_WIDTH), F32)

    if batch_major_in:
        xr = pltpu.einshape("btd->(tb)d", x_ref[...])
    else:
        xr = x_ref[...]
    hn = _rms(xr, gmix_ref[...])
    proj = _dot(hn.astype(BF16), win_ref[...])
    u_ssm = proj[:, :SSM_WIDTH]
    u_pool = proj[:, SSM_WIDTH:]

    u_b = u_ssm.astype(BF16)
    for part in range(2):
        for hf in range(2):
            lo = part * STATE_W + hf * HALF_STATE
            st_ref[:, lo:lo + HALF_STATE] = _dot(
                u_b[:, hf * HALF_SSM:(hf + 1) * HALF_SSM], bmat_ref[2 * part + hf])
    a_re = jnp.broadcast_to(are_ref[...], (SUBLANES, STATE_W))
    a_im = jnp.broadcast_to(aim_ref[...], (SUBLANES, STATE_W))

    def step(t, carry):
        s_re, s_im = carry
        r = pl.multiple_of(t * SUBLANES, SUBLANES)
        in_re = st_ref[pl.ds(r, SUBLANES), 0:STATE_W]
        in_im = st_ref[pl.ds(r, SUBLANES), STATE_W:2 * STATE_W]
        n_re = a_re * s_re - a_im * s_im + in_re
        n_im = a_re * s_im + a_im * s_re + in_im
        st_ref[pl.ds(r, SUBLANES), 0:STATE_W] = n_re
        st_ref[pl.ds(r, SUBLANES), STATE_W:2 * STATE_W] = n_im
        return n_re, n_im

    s_re, s_im = lax.fori_loop(0, steps, step, (sre_ref[...], sim_ref[...]))
    sre_ref[...] = s_re
    sim_ref[...] = s_im

    y = []
    for hf in range(2):
        lo = hf * HALF_STATE
        y.append(_dot(st_ref[:, lo:lo + HALF_STATE].astype(BF16), cre_ref[hf])
                 - _dot(st_ref[:, STATE_W + lo:STATE_W + lo + HALF_STATE].astype(BF16),
                        cim_ref[hf]))
    y = jnp.concatenate(y, axis=1) + dskip_ref[...] * u_ssm
    y = _gelu(y)
    y = y * jax.nn.sigmoid(_dot(y.astype(BF16), wglu_ref[...]) + bglu_ref[...])
    ssm_n = _rms(y, gssm_ref[...])

    ext_ref[POOL_HIST_ROWS:, :] = u_pool
    t_idx = c * steps + jnp.right_shift(
        lax.broadcasted_iota(jnp.int32, (rows, POOL_GROUP_WIDTH), 0),
        SUBLANES.bit_length() - 1)
    pooled = []
    for gi, win in enumerate(POOL_WINDOWS):
        lo = gi * POOL_GROUP_WIDTH
        hi = lo + POOL_GROUP_WIDTH
        acc = ext_ref[POOL_HIST_ROWS:, lo:hi]
        for k in range(1, win):
            off = POOL_HIST_ROWS - SUBLANES * k
            acc = acc + ext_ref[off:off + rows, lo:hi]
        count = jnp.minimum(t_idx + 1, win).astype(F32)
        pooled.append(acc / count - ext_ref[POOL_HIST_ROWS:, lo:hi])
    ext_ref[0:POOL_HIST_ROWS, :] = ext_ref[rows:rows + POOL_HIST_ROWS, :]
    pooled = jnp.concatenate(pooled, axis=1)
    y_pool = _dot(pooled.astype(BF16), wpool_ref[...]) * pscale_ref[...]
    pool_n = _rms(y_pool, gpool_ref[...])

    res = (_dot(ssm_n.astype(BF16), wout_ref[0:SSM_WIDTH, :])
           + _dot(pool_n.astype(BF16), wout_ref[SSM_WIDTH:, :]))
    o_ref[...] = xr + res


def _const_spec(shape):
    zeros = (0,) * len(shape)
    return pl.BlockSpec(shape, lambda *_: zeros, pipeline_mode=pl.Buffered(1))


def _mixer(x, gmix, win, bmat, are, aim, cre, cim, dskip, wglu, bglu, wpool,
           pscale, gssm, gpool, wout):
    batch_major_in = x.ndim == 3
    n = x.shape[0] * x.shape[1] if batch_major_in else x.shape[0]
    consts = (gmix, win, bmat, are, aim, cre, cim, dskip, wglu, bglu, wpool,
              pscale, gssm, gpool, wout)
    row_spec = pl.BlockSpec((MIX_ROWS, D_MODEL), lambda c: (c, 0))
    x_spec = (pl.BlockSpec((SUBLANES, MIX_ROWS // SUBLANES, D_MODEL), lambda c: (0, c, 0))
              if batch_major_in else row_spec)
    return pl.pallas_call(
        functools.partial(_mixer_kernel, batch_major_in=batch_major_in),
        grid=(n // MIX_ROWS,),
        in_specs=[x_spec] + [_const_spec(a.shape) for a in consts],
        out_specs=row_spec,
        out_shape=jax.ShapeDtypeStruct((n, D_MODEL), F32),
        scratch_shapes=[
            pltpu.VMEM((MIX_ROWS, 2 * STATE_W), F32),
            pltpu.VMEM((SUBLANES, STATE_W), F32),
            pltpu.VMEM((SUBLANES, STATE_W), F32),
            pltpu.VMEM((POOL_HIST_ROWS + MIX_ROWS, POOL_WIDTH), F32),
        ],
        compiler_params=pltpu.CompilerParams(
            dimension_semantics=("arbitrary",), vmem_limit_bytes=VMEM_LIMIT),
        name="mixer",
    )(x, *consts)


def _sort16_pairs():
    n, pairs, p = 16, [], 1
    while p < n:
        k = p
        while k >= 1:
            for j in range(k % p, n - k, 2 * k):
                for i in range(min(k, n - j - k)):
                    if (i + j) // (2 * p) == (i + j + k) // (2 * p):
                        pairs.append((i + j, i + j + k))
            k //= 2
        p *= 2
    return pairs


_SORT16 = _sort16_pairs()
N_TOP = PEER_TOPK + 1
SORTED_ROWS = -(-N_TOP // SUBLANES) * SUBLANES


def _top_sorted(s, out_ref, cols):
    v = [s[SUBLANES * k:SUBLANES * (k + 1), :] for k in range(16)]
    for i, j in _SORT16:
        hi = jnp.maximum(v[i], v[j])
        lo = jnp.minimum(v[i], v[j])
        v[i], v[j] = hi, lo
    for i in range(N_TOP):
        head = v[0]
        m = jnp.max(head, axis=0, keepdims=True)
        out_ref[i:i + 1, cols] = m
        if i + 1 < N_TOP:
            pop = head == m
            depth = N_TOP - i
            v = [jnp.where(pop, v[k + 1] if k + 1 < len(v) else NEG_INF, v[k])
                 for k in range(depth - 1)]


def _route_kernel(x_ref, gffn_ref, wqt_ref, k1_ref, k2_ref,
                  hb_ref, cnt_ref, e1_ref, rank_ref, w2_ref,
                  qt_ref, l1_ref, l2_ref):
    tm = x_ref.shape[0]
    h = _rms(x_ref[...], gffn_ref[...])
    hbt = h.T.astype(BF16)
    hb_ref[...] = _words(hbt)
    qt_ref[...] = _dot(wqt_ref[...], hbt)
    row = lax.broadcasted_iota(jnp.int32, (SUBLANES, LANES), 0)

    def route_tile(hd, lt, s1, s2):
        cols = slice(lt * LANES, (lt + 1) * LANES)
        _top_sorted(s1, l1_ref, cols)
        _top_sorted(s2, l2_ref, cols)
        m1 = l1_ref[0:1, cols]
        m2 = l2_ref[0:1, cols]
        a = l1_ref[1:9, cols]
        b = l2_ref[1:9, cols]
        cands = [
            m1 + l2_ref[0:8, cols],
            m1 + l2_ref[8:16, cols],
            m2 + a,
            m2 + l1_ref[9:17, cols],
            jnp.where(row < 7, l1_ref[1:2, cols] + b, NEG_INF),
            jnp.where(row < 4, l1_ref[2:3, cols] + b, NEG_INF),
            jnp.where(row < 3, l1_ref[3:4, cols] + b, NEG_INF),
            jnp.where(row < 2, l1_ref[4:5, cols] + b, NEG_INF),
            jnp.where((row >= 4) & (row < 7), l2_ref[1:2, cols] + a, NEG_INF),
            jnp.where(row == 7, m1 + l2_ref[9:17, cols], NEG_INF),
        ]
        tops = []
        for i in range(N_TOP):
            m = cands[0]
            for cnd in cands[1:]:
                m = jnp.maximum(m, cnd)
            m = jnp.max(m, axis=0, keepdims=True)
            tops.append(m)
            if i + 1 < N_TOP:
                cands = [jnp.where(cnd == m, NEG_INF, cnd) for cnd in cands]
        tau = 0.5 * (tops[PEER_TOPK - 1] + tops[PEER_TOPK])
        z = jnp.zeros_like(tau)
        for i in range(PEER_TOPK):
            z = z + jnp.exp(tops[i] - tops[0])
        theta = tau - s1
        cnt = jnp.zeros_like(s1)
        rank = jnp.zeros_like(s2)
        for j in range(PEER_TOPK):
            v2j = l2_ref[j:j + 1, cols]
            cnt = jnp.where(v2j >= theta, j + 1.0, cnt)
            rank = jnp.where(v2j > s2, j + 1.0, rank)
        cnt_ref[hd, lt] = cnt
        e1_ref[hd, lt] = jnp.exp(s1 - m1)
        rank_ref[hd, lt] = _words(rank.astype(BF16))
        w2_ref[hd, lt] = _words((jnp.exp(s2 - m2) * (0.5 / z)).astype(BF16))

    def head(hd, carry):
        base = pl.multiple_of(hd * 2 * PEER_HALF, 2 * PEER_HALF)
        q1 = qt_ref[pl.ds(base, PEER_HALF), :].astype(BF16)
        q2 = qt_ref[pl.ds(base + PEER_HALF, PEER_HALF), :].astype(BF16)
        s1 = _dot(k1_ref[...], q1)
        s2 = _dot(k2_ref[...], q2)
        for lt in range(tm // LANES):
            cols = slice(lt * LANES, (lt + 1) * LANES)
            route_tile(hd, lt, s1[:, cols], s2[:, cols])
        return carry

    lax.fori_loop(0, PEER_HEADS, head, 0)


def _route(x, gffn, wqt, k1, k2):
    n = x.shape[0]
    tm = ROUTE_TM
    def key_spec(rows):
        return pl.BlockSpec((PEER_HEADS, tm // LANES, rows, LANES), lambda i: (0, i, 0, 0))

    def key_shape(rows, dtype):
        return jax.ShapeDtypeStruct((PEER_HEADS, n // LANES, rows, LANES), dtype)

    return pl.pallas_call(
        _route_kernel,
        grid=(n // tm,),
        in_specs=[pl.BlockSpec((tm, D_MODEL), lambda i: (i, 0)),
                  _const_spec(gffn.shape), _const_spec(wqt.shape),
                  _const_spec(k1.shape), _const_spec(k2.shape)],
        out_specs=[pl.BlockSpec((D_MODEL // 2, tm), lambda i: (0, i)),
                   key_spec(PEER_N_KEYS), key_spec(PEER_N_KEYS),
                   key_spec(PEER_N_KEYS // 2), key_spec(PEER_N_KEYS // 2)],
        out_shape=[jax.ShapeDtypeStruct((D_MODEL // 2, n), jnp.uint32),
                   key_shape(PEER_N_KEYS, F32), key_shape(PEER_N_KEYS, F32),
                   key_shape(PEER_N_KEYS // 2, jnp.uint32),
                   key_shape(PEER_N_KEYS // 2, jnp.uint32)],
        scratch_shapes=[
            pltpu.VMEM((PEER_HEADS * 2 * PEER_HALF, tm), F32),
            pltpu.VMEM((SORTED_ROWS, tm), F32),
            pltpu.VMEM((SORTED_ROWS, tm), F32),
        ],
        compiler_params=pltpu.CompilerParams(
            dimension_semantics=("arbitrary",), vmem_limit_bytes=VMEM_LIMIT),
        name="peer_route",
    )(x, gffn, wqt, k1, k2)


def _packed_rows(row, rows):
    tile = jnp.broadcast_to(row, (2 * SUBLANES, LANES)).astype(BF16)
    return jnp.tile(tile, (rows // tile.shape[0], 1))


def _peer_kernel(x_ref, hb_ref, cnt_ref, e1_ref, rank_ref, w2_ref, u_ref, vt_ref,
                 gfin_ref, o_ref, s_ref, a_ref, acc_ref, *, final_norm, e_tiles, n_tiles):
    g = pl.program_id(0)
    te = 2 * u_ref.shape[0]
    tm = hb_ref.shape[1]
    blocks = te // PEER_N_KEYS
    half_piece = PEER_PIECE // 2
    cur = g % 2
    prev = 1 - cur

    @pl.when(g == 0)
    def _():
        s_ref[...] = jnp.zeros_like(s_ref)
        a_ref[...] = jnp.zeros_like(a_ref)
        acc_ref[...] = jnp.zeros_like(acc_ref)

    tile2 = jnp.clip(g - 1, 0, n_tiles - 1)
    tile3 = jnp.clip(g - 2, 0, n_tiles - 1)
    i1_base = (tile2 % e_tiles) * blocks

    def gate_blocks(r):
        for sub in range(PEER_PIECE // PEER_N_KEYS):
            ib = r * (PEER_PIECE // PEER_N_KEYS) + sub
            i1 = i1_base + ib
            for lt in range(tm // LANES):
                cols = slice(lt * LANES, (lt + 1) * LANES)
                for part in range(PEER_N_KEYS // GATE_ROWS):
                    krows = slice(part * GATE_ROWS // 2, (part + 1) * GATE_ROWS // 2)
                    gate = None
                    for hd in range(PEER_HEADS):
                        cnt = _packed_rows(cnt_ref[hd, lt, pl.ds(i1, 1), :], GATE_ROWS)
                        e1 = _packed_rows(e1_ref[hd, lt, pl.ds(i1, 1), :], GATE_ROWS)
                        term = jnp.where(_halves(rank_ref[hd, lt, krows, :]) < cnt,
                                         _halves(w2_ref[hd, lt, krows, :]) * e1,
                                         jnp.zeros((), BF16))
                        gate = term if gate is None else gate + term
                    lo = ib * PEER_N_KEYS + part * GATE_ROWS
                    bwrow = pl.ds(pl.multiple_of(lo // 2, GATE_ROWS // 2), GATE_ROWS // 2)
                    s = _halves(s_ref[prev, bwrow, cols])
                    act = s * (1.0 + lax.erf(s * INV_SQRT2))
                    a_ref[prev, bwrow, cols] = _words(act * gate)

    def piece(r, carry):
        wrows = pl.ds(pl.multiple_of(r * half_piece, half_piece), half_piece)
        acc_ref[...] += _dot(_halves(vt_ref[r]), _halves(a_ref[cur, wrows, :]))
        gate_blocks(r)
        s_ref[cur, wrows, :] = _words(
            _dot(_halves(u_ref[wrows, :]), _halves(hb_ref[...])).astype(BF16))
        return carry

    lax.fori_loop(0, te // PEER_PIECE, piece, 0)

    @pl.when((g >= 2) & (tile3 % e_tiles == e_tiles - 1))
    def _():
        out = x_ref[...] + acc_ref[...].T
        if final_norm:
            out = _rms(out, gfin_ref[...])
            o_ref[...] = pltpu.einshape("(tb)d->btd", out, b=SUBLANES)
        else:
            o_ref[...] = out
        acc_ref[...] = jnp.zeros_like(acc_ref)


def _peer(x, hb_w, cnt, e1, rank_w, w2_w, u_w, vt_w, gfin, final_norm):
    n = x.shape[0]
    tm, te = PEER_TM, PEER_TE
    e_tiles = PEER_N_EXPERTS // te
    n_tiles = (n // tm) * e_tiles
    last = n_tiles - 1

    def t1(g):
        return jnp.minimum(g, last)

    def t2(g):
        return jnp.clip(g - 1, 0, last)

    def t3(g):
        return jnp.clip(g - 2, 0, last)

    def key_spec(rows):
        return pl.BlockSpec((PEER_HEADS, tm // LANES, rows, LANES),
                            lambda g: (0, t2(g) // e_tiles, 0, 0),
                            pipeline_mode=pl.Buffered(1))

    out_row_spec = pl.BlockSpec((tm, D_MODEL), lambda g: (t3(g) // e_tiles, 0))
    return pl.pallas_call(
        functools.partial(_peer_kernel, final_norm=final_norm, e_tiles=e_tiles,
                          n_tiles=n_tiles),
        grid=(n_tiles + 2,),
        in_specs=[pl.BlockSpec((tm, D_MODEL), lambda g: (t3(g) // e_tiles, 0),
                               pipeline_mode=pl.Buffered(1)),
                  pl.BlockSpec((D_MODEL // 2, tm), lambda g: (0, t1(g) // e_tiles)),
                  key_spec(PEER_N_KEYS), key_spec(PEER_N_KEYS),
                  key_spec(PEER_N_KEYS // 2), key_spec(PEER_N_KEYS // 2),
                  pl.BlockSpec((te // 2, D_MODEL), lambda g: (t1(g) % e_tiles, 0)),
                  pl.BlockSpec((te // PEER_PIECE, D_MODEL // 2, PEER_PIECE),
                               lambda g: (t3(g) % e_tiles, 0, 0)),
                  pl.BlockSpec(gfin.shape, lambda g: (0, 0))],
        out_specs=(pl.BlockSpec((SUBLANES, tm // SUBLANES, D_MODEL),
                                lambda g: (0, t3(g) // e_tiles, 0))
                   if final_norm else out_row_spec),
        out_shape=(jax.ShapeDtypeStruct((SUBLANES, n // SUBLANES, D_MODEL), F32)
                   if final_norm else jax.ShapeDtypeStruct((n, D_MODEL), F32)),
        scratch_shapes=[
            pltpu.VMEM((2, te // 2, tm), jnp.uint32),
            pltpu.VMEM((2, te // 2, tm), jnp.uint32),
            pltpu.VMEM((D_MODEL, tm), F32),
        ],
        compiler_params=pltpu.CompilerParams(
            dimension_semantics=("arbitrary",), vmem_limit_bytes=VMEM_LIMIT),
        name="peer_dense",
    )(x, hb_w, cnt, e1, rank_w, w2_w, u_w, vt_w, gfin)


def _tables_kernel(u_ref, v_ref, uw_ref, vw_ref):
    uw_ref[...] = _words(u_ref[0].astype(BF16))
    vw_ref[0] = _words(v_ref[0].T.astype(BF16))


def _expert_tables(u, v, layer):
    _, e, d = u.shape
    rows = PEER_PIECE
    table_spec = pl.BlockSpec((1, rows, d), lambda i: (layer, i, 0))
    return pl.pallas_call(
        _tables_kernel,
        grid=(e // rows,),
        in_specs=[table_spec, table_spec],
        out_specs=[pl.BlockSpec((rows // 2, d), lambda i: (i, 0)),
                   pl.BlockSpec((1, d // 2, rows), lambda i: (i, 0, 0))],
        out_shape=[jax.ShapeDtypeStruct((e // 2, d), jnp.uint32),
                   jax.ShapeDtypeStruct((e // rows, d // 2, rows), jnp.uint32)],
        compiler_params=pltpu.CompilerParams(
            dimension_semantics=("arbitrary",), vmem_limit_bytes=VMEM_LIMIT),
        name="expert_tables",
    )(u, v)


def _block_diag(blocks):
    g, r, c = blocks.shape
    eye = jnp.eye(g, dtype=blocks.dtype)
    return (blocks[:, :, None, :] * eye[:, None, :, None]).reshape(g * r, g * c)


def kernel(x, norm_mix, w_in, a_re, a_im, log_dt, b_re, b_im, c_re, c_im, d_skip, w_glu, b_glu, w_pool, pool_scale, g_out_ssm, g_out_pool, w_out, norm_ffn, w_q, k1, k2, u_experts, v_experts, norm_final):
    bsz, seq, dm = x.shape
    assert (bsz, dm) == (SUBLANES, D_MODEL) and seq % (MIX_ROWS // SUBLANES) == 0
    depth = w_in.shape[0]
    n = bsz * seq
    row = lambda a: a.reshape(1, -1).astype(F32)

    abr, abi, btr, bti = _discretise(a_re, a_im, log_dt, b_re, b_im)
    xt = x
    gfin = row(norm_final)

    for i in range(depth):
        hg = SSM_GROUPS // 2
        bmat = jnp.stack([_block_diag(b[lo:lo + hg]) for b in (btr[i], bti[i])
                          for lo in (0, hg)]).astype(BF16)
        cre, cim = (jnp.stack([_block_diag(jnp.transpose(c[lo:lo + hg], (0, 2, 1)))
                               for lo in (0, hg)]).astype(BF16)
                    for c in (c_re[i], c_im[i]))
        xt = _mixer(
            xt, row(norm_mix[i]), w_in[i].astype(BF16), bmat,
            abr[i].reshape(1, STATE_W), abi[i].reshape(1, STATE_W), cre, cim,
            row(d_skip[i]), w_glu[i].astype(BF16), row(b_glu[i]),
            _block_diag(w_pool[i]).astype(BF16), row(pool_scale[i]),
            row(g_out_ssm[i]), row(g_out_pool[i]), w_out[i].astype(BF16))
        hb, cnt, e1, rank, w2 = _route(
            xt, row(norm_ffn[i]), jnp.transpose(w_q[i]).astype(BF16),
            k1[i].astype(BF16), k2[i].astype(BF16))
        u_w, vt_w = _expert_tables(u_experts, v_experts, i)
        xt = _peer(xt, hb, cnt, e1, rank, w2, u_w, vt_w, gfin,
                   final_norm=(i == depth - 1))
    return xt
```

```python
import functools

import jax
import jax.numpy as jnp
from jax import lax
from jax.experimental import pallas as pl
from jax.experimental.pallas import tpu as pltpu

F32 = jnp.float32
BF16 = jnp.bfloat16

D_MODEL = 1024
SSM_WIDTH = 512
POOL_WIDTH = 512
SSM_GROUP = 16
SSM_GROUPS = 32
SSM_STATE = 64
STATE_W = SSM_GROUPS * SSM_STATE
HALF_SSM = SSM_WIDTH // 2
HALF_STATE = STATE_W // 2
POOL_WINDOWS = (2, 4, 8, 16)
POOL_GROUP_WIDTH = 128
PEER_HEADS = 8
PEER_N_KEYS = 128
PEER_N_EXPERTS = PEER_N_KEYS * PEER_N_KEYS
PEER_HALF = 128
PEER_TOPK = 16
RMS_EPS = 1e-6

SUBLANES = 8
LANES = 128
MXU_DEPTH = 256
MIX_ROWS = 512
POOL_HIST_ROWS = 128
ROUTE_TM = 512
PEER_TM = 512
PEER_TE = 2048
PEER_PIECE = 1024
GATE_ROWS = 64
VMEM_LIMIT = 60 * 1024 * 1024

NEG_INF = float("-inf")
INV_SQRT2 = 0.7071067811865476


def _rms(x, g):
    return x * lax.rsqrt(jnp.mean(x * x, axis=-1, keepdims=True) + RMS_EPS) * g


def _gelu(x):
    return 0.5 * x * (1.0 + lax.erf(x * INV_SQRT2))


def _dot(a, b):
    return jnp.dot(a, b, preferred_element_type=F32)


def _words(x):
    return pltpu.bitcast(x, jnp.uint32)


def _halves(w):
    return pltpu.bitcast(w, BF16)


def _disc_kernel(are_ref, aim_ref, ldt_ref, bre_ref, bim_ref,
                 abr_ref, abi_ref, btr_ref, bti_ref):
    lam_re = are_ref[...]
    lam_im = aim_ref[...]
    dt = jnp.exp(ldt_ref[...])
    decay = jnp.exp(lam_re * dt)
    abar_re = decay * jnp.cos(lam_im * dt)
    abar_im = decay * jnp.sin(lam_im * dt)
    inv_den = 1.0 / (lam_re * lam_re + lam_im * lam_im)
    num_re = abar_re - 1.0
    zoh_re = (num_re * lam_re + abar_im * lam_im) * inv_den
    zoh_im = (abar_im * lam_re - num_re * lam_im) * inv_den
    b_re = bre_ref[...]
    b_im = bim_ref[...]
    abr_ref[...] = abar_re
    abi_ref[...] = abar_im
    btr_ref[...] = zoh_re * b_re - zoh_im * b_im
    bti_ref[...] = zoh_re * b_im + zoh_im * b_re


def _discretise(a_re, a_im, log_dt, b_re, b_im):
    nl = a_re.shape[0]
    rows = nl * SSM_GROUPS * SSM_GROUP
    shp = (nl, SSM_GROUPS, SSM_GROUP, SSM_STATE)

    def rep(a):
        return jnp.broadcast_to(a[:, :, None, :], shp).reshape(rows, SSM_STATE)

    ldt = jnp.broadcast_to(log_dt[:, :, None, None], shp).reshape(rows, SSM_STATE)
    bre = jnp.transpose(b_re, (0, 1, 3, 2)).reshape(rows, SSM_STATE)
    bim = jnp.transpose(b_im, (0, 1, 3, 2)).reshape(rows, SSM_STATE)
    out = jax.ShapeDtypeStruct((rows, SSM_STATE), F32)
    abr, abi, btr, bti = pl.pallas_call(
        _disc_kernel, out_shape=(out, out, out, out), name="s5_discretise",
    )(rep(a_re), rep(a_im), ldt, bre, bim)
    abr = abr.reshape(shp)[:, :, 0, :].reshape(nl, STATE_W)
    abi = abi.reshape(shp)[:, :, 0, :].reshape(nl, STATE_W)
    return abr, abi, btr.reshape(shp), bti.reshape(shp)


def _mixer_kernel(x_ref, gmix_ref, win_ref, bmat_ref, are_ref, aim_ref,
                  cre_ref, cim_ref, dskip_ref, wglu_ref, bglu_ref, wpool_ref,
                  pscale_ref, gssm_ref, gpool_ref, wout_ref, o_ref,
                  st_ref, sre_ref, sim_ref, ext_ref, *, batch_major_in):
    c = pl.program_id(0)
    rows = o_ref.shape[0]
    steps = rows // SUBLANES

    @pl.when(c == 0)
    def _():
        sre_ref[...] = jnp.zeros_like(sre_ref)
        sim_ref[...] = jnp.zeros_like(sim_ref)
        ext_ref[0:POOL_HIST_ROWS, :] = jnp.zeros((POOL_HIST_ROWS, POOL_WIDTH), F32)

    if batch_major_in:
        xr = pltpu.einshape("btd->(tb)d", x_ref[...])
    else:
        xr = x_ref[...]
    hn = _rms(xr, gmix_ref[...])
    proj = _dot(hn.astype(BF16), win_ref[...])
    u_ssm = proj[:, :SSM_WIDTH]
    u_pool = proj[:, SSM_WIDTH:]

    u_b = u_ssm.astype(BF16)
    for part in range(2):
        for hf in range(2):
            lo = part * STATE_W + hf * HALF_STATE
            st_ref[:, lo:lo + HALF_STATE] = _dot(
                u_b[:, hf * HALF_SSM:(hf + 1) * HALF_SSM], bmat_ref[2 * part + hf])
    a_re = jnp.broadcast_to(are_ref[...], (SUBLANES, STATE_W))
    a_im = jnp.broadcast_to(aim_ref[...], (SUBLANES, STATE_W))

    def step(t, carry):
        s_re, s_im = carry
        r = pl.multiple_of(t * SUBLANES, SUBLANES)
        in_re = st_ref[pl.ds(r, SUBLANES), 0:STATE_W]
        in_im = st_ref[pl.ds(r, SUBLANES), STATE_W:2 * STATE_W]
        n_re = a_re * s_re - a_im * s_im + in_re
        n_im = a_re * s_im + a_im * s_re + in_im
        st_ref[pl.ds(r, SUBLANES), 0:STATE_W] = n_re
        st_ref[pl.ds(r, SUBLANES), STATE_W:2 * STATE_W] = n_im
        return n_re, n_im

    s_re, s_im = lax.fori_loop(0, steps, step, (sre_ref[...], sim_ref[...]))
    sre_ref[...] = s_re
    sim_ref[...] = s_im

    y = []
    for hf in range(2):
        lo = hf * HALF_STATE
        y.append(_dot(st_ref[:, lo:lo + HALF_STATE].astype(BF16), cre_ref[hf])
                 - _dot(st_ref[:, STATE_W + lo:STATE_W + lo + HALF_STATE].astype(BF16),
                        cim_ref[hf]))
    y = jnp.concatenate(y, axis=1) + dskip_ref[...] * u_ssm
    y = _gelu(y)
    y = y * jax.nn.sigmoid(_dot(y.astype(BF16), wglu_ref[...]) + bglu_ref[...])
    ssm_n = _rms(y, gssm_ref[...])

    ext_ref[POOL_HIST_ROWS:, :] = u_pool
    t_idx = c * steps + jnp.right_shift(
        lax.broadcasted_iota(jnp.int32, (rows, POOL_GROUP_WIDTH), 0),
        SUBLANES.bit_length() - 1)
    pooled = []
    for gi, win in enumerate(POOL_WINDOWS):
        lo = gi * POOL_GROUP_WIDTH
        hi = lo + POOL_GROUP_WIDTH
        acc = ext_ref[POOL_HIST_ROWS:, lo:hi]
        for k in range(1, win):
            off = POOL_HIST_ROWS - SUBLANES * k
            acc = acc + ext_ref[off:off + rows, lo:hi]
        count = jnp.minimum(t_idx + 1, win).astype(F32)
        pooled.append(acc / count - ext_ref[POOL_HIST_ROWS:, lo:hi])
    ext_ref[0:POOL_HIST_ROWS, :] = ext_ref[rows:rows + POOL_HIST_ROWS, :]
    pooled = jnp.concatenate(pooled, axis=1)
    y_pool = _dot(pooled.astype(BF16), wpool_ref[...]) * pscale_ref[...]
    pool_n = _rms(y_pool, gpool_ref[...])

    res = (_dot(ssm_n.astype(BF16), wout_ref[0:SSM_WIDTH, :])
           + _dot(pool_n.astype(BF16), wout_ref[SSM_WIDTH:, :]))
    o_ref[...] = xr + res


def _const_spec(shape):
    zeros = (0,) * len(shape)
    return pl.BlockSpec(shape, lambda *_: zeros, pipeline_mode=pl.Buffered(1))


def _mixer(x, gmix, win, bmat, are, aim, cre, cim, dskip, wglu, bglu, wpool,
           pscale, gssm, gpool, wout):
    batch_major_in = x.ndim == 3
    n = x.shape[0] * x.shape[1] if batch_major_in else x.shape[0]
    consts = (gmix, win, bmat, are, aim, cre, cim, dskip, wglu, bglu, wpool,
              pscale, gssm, gpool, wout)
    row_spec = pl.BlockSpec((MIX_ROWS, D_MODEL), lambda c: (c, 0))
    x_spec = (pl.BlockSpec((SUBLANES, MIX_ROWS // SUBLANES, D_MODEL), lambda c: (0, c, 0))
              if batch_major_in else row_spec)
    return pl.pallas_call(
        functools.partial(_mixer_kernel, batch_major_in=batch_major_in),
        grid=(n // MIX_ROWS,),
        in_specs=[x_spec] + [_const_spec(a.shape) for a in consts],
        out_specs=row_spec,
        out_shape=jax.ShapeDtypeStruct((n, D_MODEL), F32),
        scratch_shapes=[
            pltpu.VMEM((MIX_ROWS, 2 * STATE_W), F32),
            pltpu.VMEM((SUBLANES, STATE_W), F32),
            pltpu.VMEM((SUBLANES, STATE_W), F32),
            pltpu.VMEM((POOL_HIST_ROWS + MIX_ROWS, POOL_WIDTH), F32),
        ],
        compiler_params=pltpu.CompilerParams(
            dimension_semantics=("arbitrary",), vmem_limit_bytes=VMEM_LIMIT),
        name="mixer",
    )(x, *consts)


def _sort16_pairs():
    n, pairs, p = 16, [], 1
    while p < n:
        k = p
        while k >= 1:
            for j in range(k % p, n - k, 2 * k):
                for i in range(min(k, n - j - k)):
                    if (i + j) // (2 * p) == (i + j + k) // (2 * p):
                        pairs.append((i + j, i + j + k))
            k //= 2
        p *= 2
    return pairs


_SORT16 = _sort16_pairs()
N_TOP = PEER_TOPK + 1
SORTED_ROWS = -(-N_TOP // SUBLANES) * SUBLANES


def _top_sorted(s, out_ref, cols):
    v = [s[SUBLANES * k:SUBLANES * (k + 1), :] for k in range(16)]
    for i, j in _SORT16:
        hi = jnp.maximum(v[i], v[j])
        lo = jnp.minimum(v[i], v[j])
        v[i], v[j] = hi, lo
    for i in range(N_TOP):
        head = v[0]
        m = jnp.max(head, axis=0, keepdims=True)
        out_ref[i:i + 1, cols] = m
        if i + 1 < N_TOP:
            pop = head == m
            depth = N_TOP - i
            v = [jnp.where(pop, v[k + 1] if k + 1 < len(v) else NEG_INF, v[k])
                 for k in range(depth - 1)]


def _route_kernel(x_ref, gffn_ref, wqt_ref, k1_ref, k2_ref,
                  hb_ref, cnt_ref, e1_ref, rank_ref, w2_ref,
                  qt_ref, l1_ref, l2_ref):
    tm = x_ref.shape[0]
    h = _rms(x_ref[...], gffn_ref[...])
    hbt = h.T.astype(BF16)
    hb_ref[...] = _words(hbt)
    qt_ref[...] = _dot(wqt_ref[...], hbt)
    row = lax.broadcasted_iota(jnp.int32, (SUBLANES, LANES), 0)

    def route_tile(hd, lt, s1, s2):
        cols = slice(lt * LANES, (lt + 1) * LANES)
        _top_sorted(s1, l1_ref, cols)
        _top_sorted(s2, l2_ref, cols)
        m1 = l1_ref[0:1, cols]
        m2 = l2_ref[0:1, cols]
        a = l1_ref[1:9, cols]
        b = l2_ref[1:9, cols]
        v2_1 = l2_ref[1:2, cols] + a
        cands = [
            m1 + l2_ref[0:8, cols],
            m1 + l2_ref[8:16, cols],
            m2 + a,
            m2 + l1_ref[9:17, cols],
            jnp.where(row < 7, l1_ref[1:2, cols] + b, m1 + l2_ref[9:17, cols]),
            jnp.where(row < 4, l1_ref[2:3, cols] + b, jnp.where(row < 7, v2_1, NEG_INF)),
            jnp.where(row < 3, l1_ref[3:4, cols] + b,
                      jnp.where(row == 3, v2_1,
                                jnp.where(row == 4, l2_ref[2:3, cols] + l1_ref[0:8, cols],
                                          NEG_INF))),
        ]
        tops = []
        for i in range(N_TOP):
            m = cands[0]
            for cnd in cands[1:]:
                m = jnp.maximum(m, cnd)
            m = jnp.max(m, axis=0, keepdims=True)
            tops.append(m)
            if i + 1 < N_TOP:
                cands = [jnp.where(cnd == m, NEG_INF, cnd) for cnd in cands]
        tau = 0.5 * (tops[PEER_TOPK - 1] + tops[PEER_TOPK])
        z = jnp.zeros_like(tau)
        for i in range(PEER_TOPK):
            z = z + jnp.exp(tops[i] - tops[0])
        theta = tau - s1
        cnt = jnp.zeros_like(s1)
        rank = jnp.zeros_like(s2)
        for j in range(PEER_TOPK):
            v2j = l2_ref[j:j + 1, cols]
            cnt = jnp.where(v2j >= theta, j + 1.0, cnt)
            rank = jnp.where(v2j > s2, j + 1.0, rank)
        cnt_ref[hd, lt] = cnt
        e1_ref[hd, lt] = jnp.exp(s1 - m1)
        rank_ref[hd, lt] = _words(rank.astype(BF16))
        w2_ref[hd, lt] = _words((jnp.exp(s2 - m2) * (0.5 / z)).astype(BF16))

    def head(hd, carry):
        base = pl.multiple_of(hd * 2 * PEER_HALF, 2 * PEER_HALF)
        q1 = qt_ref[pl.ds(base, PEER_HALF), :].astype(BF16)
        q2 = qt_ref[pl.ds(base + PEER_HALF, PEER_HALF), :].astype(BF16)
        s1 = _dot(k1_ref[...], q1)
        s2 = _dot(k2_ref[...], q2)
        for lt in range(tm // LANES):
            cols = slice(lt * LANES, (lt + 1) * LANES)
            route_tile(hd, lt, s1[:, cols], s2[:, cols])
        return carry

    lax.fori_loop(0, PEER_HEADS, head, 0)


def _route(x, gffn, wqt, k1, k2):
    n = x.shape[0]
    tm = ROUTE_TM
    def key_spec(rows):
        return pl.BlockSpec((PEER_HEADS, tm // LANES, rows, LANES), lambda i: (0, i, 0, 0))

    def key_shape(rows, dtype):
        return jax.ShapeDtypeStruct((PEER_HEADS, n // LANES, rows, LANES), dtype)

    return pl.pallas_call(
        _route_kernel,
        grid=(n // tm,),
        in_specs=[pl.BlockSpec((tm, D_MODEL), lambda i: (i, 0)),
                  _const_spec(gffn.shape), _const_spec(wqt.shape),
                  _const_spec(k1.shape), _const_spec(k2.shape)],
        out_specs=[pl.BlockSpec((D_MODEL // 2, tm), lambda i: (0, i)),
                   key_spec(PEER_N_KEYS), key_spec(PEER_N_KEYS),
                   key_spec(PEER_N_KEYS // 2), key_spec(PEER_N_KEYS // 2)],
        out_shape=[jax.ShapeDtypeStruct((D_MODEL // 2, n), jnp.uint32),
                   key_shape(PEER_N_KEYS, F32), key_shape(PEER_N_KEYS, F32),
                   key_shape(PEER_N_KEYS // 2, jnp.uint32),
                   key_shape(PEER_N_KEYS // 2, jnp.uint32)],
        scratch_shapes=[
            pltpu.VMEM((PEER_HEADS * 2 * PEER_HALF, tm), F32),
            pltpu.VMEM((SORTED_ROWS, tm), F32),
            pltpu.VMEM((SORTED_ROWS, tm), F32),
        ],
        compiler_params=pltpu.CompilerParams(
            dimension_semantics=("arbitrary",), vmem_limit_bytes=VMEM_LIMIT),
        name="peer_route",
    )(x, gffn, wqt, k1, k2)


def _packed_rows(row, rows):
    tile = jnp.broadcast_to(row, (2 * SUBLANES, LANES)).astype(BF16)
    return jnp.tile(tile, (rows // tile.shape[0], 1))


def _peer_kernel(x_ref, hb_ref, cnt_ref, e1_ref, rank_ref, w2_ref, u_ref, vt_ref,
                 gfin_ref, o_ref, s_ref, a_ref, acc_ref, *, final_norm, e_tiles, n_tiles):
    g = pl.program_id(0)
    te = 2 * u_ref.shape[0]
    tm = hb_ref.shape[1]
    blocks = te // PEER_N_KEYS
    half_piece = PEER_PIECE // 2
    cur = g % 2
    prev = 1 - cur

    @pl.when(g == 0)
    def _():
        s_ref[...] = jnp.zeros_like(s_ref)
        a_ref[...] = jnp.zeros_like(a_ref)
        acc_ref[...] = jnp.zeros_like(acc_ref)

    tile2 = jnp.clip(g - 1, 0, n_tiles - 1)
    tile3 = jnp.clip(g - 2, 0, n_tiles - 1)
    i1_base = (tile2 % e_tiles) * blocks

    def gate_blocks(r):
        for sub in range(PEER_PIECE // PEER_N_KEYS):
            ib = r * (PEER_PIECE // PEER_N_KEYS) + sub
            i1 = i1_base + ib
            for lt in range(tm // LANES):
                cols = slice(lt * LANES, (lt + 1) * LANES)
                for part in range(PEER_N_KEYS // GATE_ROWS):
                    krows = slice(part * GATE_ROWS // 2, (part + 1) * GATE_ROWS // 2)
                    gate = None
                    for hd in range(PEER_HEADS):
                        cnt = _packed_rows(cnt_ref[hd, lt, pl.ds(i1, 1), :], GATE_ROWS)
                        e1 = _packed_rows(e1_ref[hd, lt, pl.ds(i1, 1), :], GATE_ROWS)
                        term = jnp.where(_halves(rank_ref[hd, lt, krows, :]) < cnt,
                                         _halves(w2_ref[hd, lt, krows, :]) * e1,
                                         jnp.zeros((), BF16))
                        gate = term if gate is None else gate + term
                    lo = ib * PEER_N_KEYS + part * GATE_ROWS
                    bwrow = pl.ds(pl.multiple_of(lo // 2, GATE_ROWS // 2), GATE_ROWS // 2)
                    s = _halves(s_ref[prev, bwrow, cols])
                    act = s * (1.0 + lax.erf(s * INV_SQRT2))
                    a_ref[prev, bwrow, cols] = _words(act * gate)

    def piece(r, carry):
        wrows = pl.ds(pl.multiple_of(r * half_piece, half_piece), half_piece)
        acc_ref[...] += _dot(_halves(vt_ref[r]), _halves(a_ref[cur, wrows, :]))
        gate_blocks(r)
        s_ref[cur, wrows, :] = _words(
            _dot(_halves(u_ref[wrows, :]), _halves(hb_ref[...])).astype(BF16))
        return carry

    lax.fori_loop(0, te // PEER_PIECE, piece, 0)

    @pl.when((g >= 2) & (tile3 % e_tiles == e_tiles - 1))
    def _():
        out = x_ref[...] + acc_ref[...].T
        if final_norm:
            out = _rms(out, gfin_ref[...])
            o_ref[...] = pltpu.einshape("(tb)d->btd", out, b=SUBLANES)
        else:
            o_ref[...] = out
        acc_ref[...] = jnp.zeros_like(acc_ref)


def _peer(x, hb_w, cnt, e1, rank_w, w2_w, u_w, vt_w, gfin, final_norm):
    n = x.shape[0]
    tm, te = PEER_TM, PEER_TE
    e_tiles = PEER_N_EXPERTS // te
    n_tiles = (n // tm) * e_tiles
    last = n_tiles - 1

    def t1(g):
        return jnp.minimum(g, last)

    def t2(g):
        return jnp.clip(g - 1, 0, last)

    def t3(g):
        return jnp.clip(g - 2, 0, last)

    def key_spec(rows):
        return pl.BlockSpec((PEER_HEADS, tm // LANES, rows, LANES),
                            lambda g: (0, t2(g) // e_tiles, 0, 0),
                            pipeline_mode=pl.Buffered(1))

    out_row_spec = pl.BlockSpec((tm, D_MODEL), lambda g: (t3(g) // e_tiles, 0))
    return pl.pallas_call(
        functools.partial(_peer_kernel, final_norm=final_norm, e_tiles=e_tiles,
                          n_tiles=n_tiles),
        grid=(n_tiles + 2,),
        in_specs=[pl.BlockSpec((tm, D_MODEL), lambda g: (t3(g) // e_tiles, 0),
                               pipeline_mode=pl.Buffered(1)),
                  pl.BlockSpec((D_MODEL // 2, tm), lambda g: (0, t1(g) // e_tiles)),
                  key_spec(PEER_N_KEYS), key_spec(PEER_N_KEYS),
                  key_spec(PEER_N_KEYS // 2), key_spec(PEER_N_KEYS // 2),
                  pl.BlockSpec((te // 2, D_MODEL), lambda g: (t1(g) % e_tiles, 0)),
                  pl.BlockSpec((te // PEER_PIECE, D_MODEL // 2, PEER_PIECE),
                               lambda g: (t3(g) % e_tiles, 0, 0)),
                  pl.BlockSpec(gfin.shape, lambda g: (0, 0))],
        out_specs=(pl.BlockSpec((SUBLANES, tm // SUBLANES, D_MODEL),
                                lambda g: (0, t3(g) // e_tiles, 0))
                   if final_norm else out_row_spec),
        out_shape=(jax.ShapeDtypeStruct((SUBLANES, n // SUBLANES, D_MODEL), F32)
                   if final_norm else jax.ShapeDtypeStruct((n, D_MODEL), F32)),
        scratch_shapes=[
            pltpu.VMEM((2, te // 2, tm), jnp.uint32),
            pltpu.VMEM((2, te // 2, tm), jnp.uint32),
            pltpu.VMEM((D_MODEL, tm), F32),
        ],
        compiler_params=pltpu.CompilerParams(
            dimension_semantics=("arbitrary",), vmem_limit_bytes=VMEM_LIMIT),
        name="peer_dense",
    )(x, hb_w, cnt, e1, rank_w, w2_w, u_w, vt_w, gfin)


def _tables_kernel(u_ref, v_ref, uw_ref, vw_ref):
    uw_ref[...] = _words(u_ref[0].astype(BF16))
    vw_ref[0] = _words(v_ref[0].T.astype(BF16))


def _expert_tables(u, v, layer):
    _, e, d = u.shape
    rows = PEER_PIECE
    table_spec = pl.BlockSpec((1, rows, d), lambda i: (layer, i, 0))
    return pl.pallas_call(
        _tables_kernel,
        grid=(e // rows,),
        in_specs=[table_spec, table_spec],
        out_specs=[pl.BlockSpec((rows // 2, d), lambda i: (i, 0)),
                   pl.BlockSpec((1, d // 2, rows), lambda i: (i, 0, 0))],
        out_shape=[jax.ShapeDtypeStruct((e // 2, d), jnp.uint32),
                   jax.ShapeDtypeStruct((e // rows, d // 2, rows), jnp.uint32)],
        compiler_params=pltpu.CompilerParams(
            dimension_semantics=("arbitrary",), vmem_limit_bytes=VMEM_LIMIT),
        name="expert_tables",
    )(u, v)


def _block_diag(blocks):
    g, r, c = blocks.shape
    eye = jnp.eye(g, dtype=blocks.dtype)
    return (blocks[:, :, None, :] * eye[:, None, :, None]).reshape(g * r, g * c)


def kernel(x, norm_mix, w_in, a_re, a_im, log_dt, b_re, b_im, c_re, c_im, d_skip, w_glu, b_glu, w_pool, pool_scale, g_out_ssm, g_out_pool, w_out, norm_ffn, w_q, k1, k2, u_experts, v_experts, norm_final):
    bsz, seq, dm = x.shape
    assert (bsz, dm) == (SUBLANES, D_MODEL) and seq % (MIX_ROWS // SUBLANES) == 0
    depth = w_in.shape[0]
    n = bsz * seq
    row = lambda a: a.reshape(1, -1).astype(F32)

    abr, abi, btr, bti = _discretise(a_re, a_im, log_dt, b_re, b_im)
    xt = x
    gfin = row(norm_final)

    for i in range(depth):
        hg = SSM_GROUPS // 2
        bmat = jnp.stack([_block_diag(b[lo:lo + hg]) for b in (btr[i], bti[i])
                          for lo in (0, hg)]).astype(BF16)
        cre, cim = (jnp.stack([_block_diag(jnp.transpose(c[lo:lo + hg], (0, 2, 1)))
                               for lo in (0, hg)]).astype(BF16)
                    for c in (c_re[i], c_im[i]))
        xt = _mixer(
            xt, row(norm_mix[i]), w_in[i].astype(BF16), bmat,
            abr[i].reshape(1, STATE_W), abi[i].reshape(1, STATE_W), cre, cim,
            row(d_skip[i]), w_glu[i].astype(BF16), row(b_glu[i]),
            _block_diag(w_pool[i]).astype(BF16), row(pool_scale[i]),
            row(g_out_ssm[i]), row(g_out_pool[i]), w_out[i].astype(BF16))
        hb, cnt, e1, rank, w2 = _route(
            xt, row(norm_ffn[i]), jnp.transpose(w_q[i]).astype(BF16),
            k1[i].astype(BF16), k2[i].astype(BF16))
        u_w, vt_w = _expert_tables(u_experts, v_experts, i)
        xt = _peer(xt, hb, cnt, e1, rank, w2, u_w, vt_w, gfin,
                   final_norm=(i == depth - 1))
    return xt
```

```python
import functools

import jax
import jax.numpy as jnp
from jax import lax
from jax.experimental import pallas as pl
from jax.experimental.pallas import tpu as pltpu

F32 = jnp.float32
BF16 = jnp.bfloat16

D_MODEL = 1024
SSM_WIDTH = 512
POOL_WIDTH = 512
SSM_GROUP = 16
SSM_GROUPS = 32
SSM_STATE = 64
STATE_W = SSM_GROUPS * SSM_STATE
HALF_SSM = SSM_WIDTH // 2
HALF_STATE = STATE_W // 2
POOL_WINDOWS = (2, 4, 8, 16)
POOL_GROUP_WIDTH = 128
PEER_HEADS = 8
PEER_N_KEYS = 128
PEER_N_EXPERTS = PEER_N_KEYS * PEER_N_KEYS
PEER_HALF = 128
PEER_TOPK = 16
RMS_EPS = 1e-6

SUBLANES = 8
LANES = 128
MXU_DEPTH = 256
MIX_ROWS = 512
POOL_HIST_ROWS = 128
ROUTE_TM = 512
PEER_TM = 512
PEER_TE = 2048
PEER_PIECE = 1024
GATE_ROWS = 64
VMEM_LIMIT = 60 * 1024 * 1024

NEG_INF = float("-inf")
INV_SQRT2 = 0.7071067811865476


def _rms(x, g):
    return x * lax.rsqrt(jnp.mean(x * x, axis=-1, keepdims=True) + RMS_EPS) * g


def _gelu(x):
    return 0.5 * x * (1.0 + lax.erf(x * INV_SQRT2))


def _dot(a, b):
    return jnp.dot(a, b, preferred_element_type=F32)


def _words(x):
    return pltpu.bitcast(x, jnp.uint32)


def _halves(w):
    return pltpu.bitcast(w, BF16)


def _disc_kernel(are_ref, aim_ref, ldt_ref, bre_ref, bim_ref,
                 abr_ref, abi_ref, btr_ref, bti_ref):
    lam_re = are_ref[...]
    lam_im = aim_ref[...]
    dt = jnp.exp(ldt_ref[...])
    decay = jnp.exp(lam_re * dt)
    abar_re = decay * jnp.cos(lam_im * dt)
    abar_im = decay * jnp.sin(lam_im * dt)
    inv_den = 1.0 / (lam_re * lam_re + lam_im * lam_im)
    num_re = abar_re - 1.0
    zoh_re = (num_re * lam_re + abar_im * lam_im) * inv_den
    zoh_im = (abar_im * lam_re - num_re * lam_im) * inv_den
    b_re = bre_ref[...]
    b_im = bim_ref[...]
    abr_ref[...] = abar_re
    abi_ref[...] = abar_im
    btr_ref[...] = zoh_re * b_re - zoh_im * b_im
    bti_ref[...] = zoh_re * b_im + zoh_im * b_re


def _discretise(a_re, a_im, log_dt, b_re, b_im):
    nl = a_re.shape[0]
    rows = nl * SSM_GROUPS * SSM_GROUP
    shp = (nl, SSM_GROUPS, SSM_GROUP, SSM_STATE)

    def rep(a):
        return jnp.broadcast_to(a[:, :, None, :], shp).reshape(rows, SSM_STATE)

    ldt = jnp.broadcast_to(log_dt[:, :, None, None], shp).reshape(rows, SSM_STATE)
    bre = jnp.transpose(b_re, (0, 1, 3, 2)).reshape(rows, SSM_STATE)
    bim = jnp.transpose(b_im, (0, 1, 3, 2)).reshape(rows, SSM_STATE)
    out = jax.ShapeDtypeStruct((rows, SSM_STATE), F32)
    abr, abi, btr, bti = pl.pallas_call(
        _disc_kernel, out_shape=(out, out, out, out), name="s5_discretise",
    )(rep(a_re), rep(a_im), ldt, bre, bim)
    abr = abr.reshape(shp)[:, :, 0, :].reshape(nl, STATE_W)
    abi = abi.reshape(shp)[:, :, 0, :].reshape(nl, STATE_W)
    return abr, abi, btr.reshape(shp), bti.reshape(shp)


def _mixer_kernel(x_ref, gmix_ref, win_ref, bmat_ref, are_ref, aim_ref,
                  cre_ref, cim_ref, dskip_ref, wglu_ref, bglu_ref, wpool_ref,
                  pscale_ref, gssm_ref, gpool_ref, wout_ref, o_ref,
                  st_ref, sre_ref, sim_ref, ext_ref, *, batch_major_in):
    c = pl.program_id(0)
    rows = o_ref.shape[0]
    steps = rows // SUBLANES

    @pl.when(c == 0)
    def _():
        sre_ref[...] = jnp.zeros_like(sre_ref)
        sim_ref[...] = jnp.zeros_like(sim_ref)
        ext_ref[0:POOL_HIST_ROWS, :] = jnp.zeros((POOL_HIST_ROWS, POOL_WIDTH), F32)

    if batch_major_in:
        xr = pltpu.einshape("btd->(tb)d", x_ref[...])
    else:
        xr = x_ref[...]
    hn = _rms(xr, gmix_ref[...])
    proj = _dot(hn.astype(BF16), win_ref[...])
    u_ssm = proj[:, :SSM_WIDTH]
    u_pool = proj[:, SSM_WIDTH:]

    u_b = u_ssm.astype(BF16)
    for part in range(2):
        for hf in range(2):
            lo = part * STATE_W + hf * HALF_STATE
            st_ref[:, lo:lo + HALF_STATE] = _dot(
                u_b[:, hf * HALF_SSM:(hf + 1) * HALF_SSM], bmat_ref[2 * part + hf])
    a_re = jnp.broadcast_to(are_ref[...], (SUBLANES, STATE_W))
    a_im = jnp.broadcast_to(aim_ref[...], (SUBLANES, STATE_W))

    def step(t, carry):
        s_re, s_im = carry
        r = pl.multiple_of(t * SUBLANES, SUBLANES)
        in_re = st_ref[pl.ds(r, SUBLANES), 0:STATE_W]
        in_im = st_ref[pl.ds(r, SUBLANES), STATE_W:2 * STATE_W]
        n_re = a_re * s_re - a_im * s_im + in_re
        n_im = a_re * s_im + a_im * s_re + in_im
        st_ref[pl.ds(r, SUBLANES), 0:STATE_W] = n_re
        st_ref[pl.ds(r, SUBLANES), STATE_W:2 * STATE_W] = n_im
        return n_re, n_im

    s_re, s_im = lax.fori_loop(0, steps, step, (sre_ref[...], sim_ref[...]))
    sre_ref[...] = s_re
    sim_ref[...] = s_im

    y = []
    for hf in range(2):
        lo = hf * HALF_STATE
        y.append(_dot(st_ref[:, lo:lo + HALF_STATE].astype(BF16), cre_ref[hf])
                 - _dot(st_ref[:, STATE_W + lo:STATE_W + lo + HALF_STATE].astype(BF16),
                        cim_ref[hf]))
    y = jnp.concatenate(y, axis=1) + dskip_ref[...] * u_ssm
    y = _gelu(y)
    y = y * jax.nn.sigmoid(_dot(y.astype(BF16), wglu_ref[...]) + bglu_ref[...])
    ssm_n = _rms(y, gssm_ref[...])

    ext_ref[POOL_HIST_ROWS:, :] = u_pool
    t_idx = c * steps + jnp.right_shift(
        lax.broadcasted_iota(jnp.int32, (rows, POOL_GROUP_WIDTH), 0),
        SUBLANES.bit_length() - 1)
    pooled = []
    for gi, win in enumerate(POOL_WINDOWS):
        lo = gi * POOL_GROUP_WIDTH
        hi = lo + POOL_GROUP_WIDTH
        acc = ext_ref[POOL_HIST_ROWS:, lo:hi]
        for k in range(1, win):
            off = POOL_HIST_ROWS - SUBLANES * k
            acc = acc + ext_ref[off:off + rows, lo:hi]
        count = jnp.minimum(t_idx + 1, win).astype(F32)
        pooled.append(acc / count - ext_ref[POOL_HIST_ROWS:, lo:hi])
    ext_ref[0:POOL_HIST_ROWS, :] = ext_ref[rows:rows + POOL_HIST_ROWS, :]
    pooled = jnp.concatenate(pooled, axis=1)
    y_pool = _dot(pooled.astype(BF16), wpool_ref[...]) * pscale_ref[...]
    pool_n = _rms(y_pool, gpool_ref[...])

    res = (_dot(ssm_n.astype(BF16), wout_ref[0:SSM_WIDTH, :])
           + _dot(pool_n.astype(BF16), wout_ref[SSM_WIDTH:, :]))
    o_ref[...] = xr + res


def _const_spec(shape):
    zeros = (0,) * len(shape)
    return pl.BlockSpec(shape, lambda *_: zeros, pipeline_mode=pl.Buffered(1))


def _mixer(x, gmix, win, bmat, are, aim, cre, cim, dskip, wglu, bglu, wpool,
           pscale, gssm, gpool, wout):
    batch_major_in = x.ndim == 3
    n = x.shape[0] * x.shape[1] if batch_major_in else x.shape[0]
    consts = (gmix, win, bmat, are, aim, cre, cim, dskip, wglu, bglu, wpool,
              pscale, gssm, gpool, wout)
    row_spec = pl.BlockSpec((MIX_ROWS, D_MODEL), lambda c: (c, 0))
    x_spec = (pl.BlockSpec((SUBLANES, MIX_ROWS // SUBLANES, D_MODEL), lambda c: (0, c, 0))
              if batch_major_in else row_spec)
    return pl.pallas_call(
        functools.partial(_mixer_kernel, batch_major_in=batch_major_in),
        grid=(n // MIX_ROWS,),
        in_specs=[x_spec] + [_const_spec(a.shape) for a in consts],
        out_specs=row_spec,
        out_shape=jax.ShapeDtypeStruct((n, D_MODEL), F32),
        scratch_shapes=[
            pltpu.VMEM((MIX_ROWS, 2 * STATE_W), F32),
            pltpu.VMEM((SUBLANES, STATE_W), F32),
            pltpu.VMEM((SUBLANES, STATE_W), F32),
            pltpu.VMEM((POOL_HIST_ROWS + MIX_ROWS, POOL_WIDTH), F32),
        ],
        compiler_params=pltpu.CompilerParams(
            dimension_semantics=("arbitrary",), vmem_limit_bytes=VMEM_LIMIT),
        name="mixer",
    )(x, *consts)


def _sort16_pairs():
    n, pairs, p = 16, [], 1
    while p < n:
        k = p
        while k >= 1:
            for j in range(k % p, n - k, 2 * k):
                for i in range(min(k, n - j - k)):
                    if (i + j) // (2 * p) == (i + j + k) // (2 * p):
                        pairs.append((i + j, i + j + k))
            k //= 2
        p *= 2
    return pairs


_SORT16 = _sort16_pairs()
N_TOP = PEER_TOPK + 1
SORTED_ROWS = -(-N_TOP // SUBLANES) * SUBLANES


def _top_sorted(s, out_ref, cols):
    v = [s[SUBLANES * k:SUBLANES * (k + 1), :] for k in range(16)]
    for i, j in _SORT16:
        hi = jnp.maximum(v[i], v[j])
        lo = jnp.minimum(v[i], v[j])
        v[i], v[j] = hi, lo
    for i in range(N_TOP):
        head = v[0]
        m = jnp.max(head, axis=0, keepdims=True)
        out_ref[i:i + 1, cols] = m
        if i + 1 < N_TOP:
            pop = head == m
            depth = N_TOP - i
            v = [jnp.where(pop, v[k + 1] if k + 1 < len(v) else NEG_INF, v[k])
                 for k in range(depth - 1)]


def _route_kernel(x_ref, gffn_ref, wqt_ref, k1_ref, k2_ref,
                  hb_ref, cnt_ref, e1_ref, rank_ref, w2_ref,
                  qt_ref, l1_ref, l2_ref):
    tm = x_ref.shape[0]
    h = _rms(x_ref[...], gffn_ref[...])
    hbt = h.T.astype(BF16)
    hb_ref[...] = _words(hbt)
    qt_ref[...] = _dot(wqt_ref[...], hbt)
    row = lax.broadcasted_iota(jnp.int32, (SUBLANES, LANES), 0)

    def route_tile(hd, lt, s1, s2):
        cols = slice(lt * LANES, (lt + 1) * LANES)
        _top_sorted(s1, l1_ref, cols)
        _top_sorted(s2, l2_ref, cols)
        m1 = l1_ref[0:1, cols]
        m2 = l2_ref[0:1, cols]
        a = l1_ref[1:9, cols]
        b = l2_ref[1:9, cols]
        v2_1 = l2_ref[1:2, cols] + a
        cands = [
            m1 + l2_ref[0:8, cols],
            m1 + l2_ref[8:16, cols],
            m2 + a,
            m2 + l1_ref[9:17, cols],
            jnp.where(row < 7, l1_ref[1:2, cols] + b, m1 + l2_ref[9:17, cols]),
            jnp.where(row < 4, l1_ref[2:3, cols] + b, jnp.where(row < 7, v2_1, NEG_INF)),
            jnp.where(row < 3, l1_ref[3:4, cols] + b,
                      jnp.where(row == 3, v2_1,
                                jnp.where(row == 4, l2_ref[2:3, cols] + l1_ref[0:8, cols],
                                          NEG_INF))),
        ]
        tops = []
        for i in range(N_TOP):
            m = cands[0]
            for cnd in cands[1:]:
                m = jnp.maximum(m, cnd)
            m = jnp.max(m, axis=0, keepdims=True)
            tops.append(m)
            if i + 1 < N_TOP:
                cands = [jnp.where(cnd == m, NEG_INF, cnd) for cnd in cands]
        tau = 0.5 * (tops[PEER_TOPK - 1] + tops[PEER_TOPK])
        z = jnp.zeros_like(tau)
        for i in range(PEER_TOPK):
            z = z + jnp.exp(tops[i] - tops[0])
        theta = tau - s1
        cnt = jnp.zeros_like(s1)
        rank = jnp.zeros_like(s2)
        half = PEER_TOPK // 2
        for j in range(half):
            v2j = l2_ref[j:j + 1, cols]
            cnt = jnp.where(v2j >= theta, j + 1.0, cnt)
            rank = jnp.where(v2j > s2, j + 1.0, rank)
        theta_top = tau - m1
        top_more = l2_ref[half:half + 1, cols] >= theta_top
        cnt = jnp.where((s1 == m1) & top_more, half + 1.0, cnt)
        rank = jnp.where((rank == half) & (s2 < theta_top), half + 1.0, rank)
        cnt_ref[hd, lt] = cnt
        e1_ref[hd, lt] = jnp.exp(s1 - m1)
        rank_ref[hd, lt] = _words(rank.astype(BF16))
        w2_ref[hd, lt] = _words((jnp.exp(s2 - m2) * (0.5 / z)).astype(BF16))

    def head(hd, carry):
        base = pl.multiple_of(hd * 2 * PEER_HALF, 2 * PEER_HALF)
        q1 = qt_ref[pl.ds(base, PEER_HALF), :].astype(BF16)
        q2 = qt_ref[pl.ds(base + PEER_HALF, PEER_HALF), :].astype(BF16)
        s1 = _dot(k1_ref[...], q1)
        s2 = _dot(k2_ref[...], q2)
        for lt in range(tm // LANES):
            cols = slice(lt * LANES, (lt + 1) * LANES)
            route_tile(hd, lt, s1[:, cols], s2[:, cols])
        return carry

    lax.fori_loop(0, PEER_HEADS, head, 0)


def _route(x, gffn, wqt, k1, k2):
    n = x.shape[0]
    tm = ROUTE_TM
    def key_spec(rows):
        return pl.BlockSpec((PEER_HEADS, tm // LANES, rows, LANES), lambda i: (0, i, 0, 0))

    def key_shape(rows, dtype):
        return jax.ShapeDtypeStruct((PEER_HEADS, n // LANES, rows, LANES), dtype)

    return pl.pallas_call(
        _route_kernel,
        grid=(n // tm,),
        in_specs=[pl.BlockSpec((tm, D_MODEL), lambda i: (i, 0)),
                  _const_spec(gffn.shape), _const_spec(wqt.shape),
                  _const_spec(k1.shape), _const_spec(k2.shape)],
        out_specs=[pl.BlockSpec((D_MODEL // 2, tm), lambda i: (0, i)),
                   key_spec(PEER_N_KEYS), key_spec(PEER_N_KEYS),
                   key_spec(PEER_N_KEYS // 2), key_spec(PEER_N_KEYS // 2)],
        out_shape=[jax.ShapeDtypeStruct((D_MODEL // 2, n), jnp.uint32),
                   key_shape(PEER_N_KEYS, F32), key_shape(PEER_N_KEYS, F32),
                   key_shape(PEER_N_KEYS // 2, jnp.uint32),
                   key_shape(PEER_N_KEYS // 2, jnp.uint32)],
        scratch_shapes=[
            pltpu.VMEM((PEER_HEADS * 2 * PEER_HALF, tm), F32),
            pltpu.VMEM((SORTED_ROWS, tm), F32),
            pltpu.VMEM((SORTED_ROWS, tm), F32),
        ],
        compiler_params=pltpu.CompilerParams(
            dimension_semantics=("arbitrary",), vmem_limit_bytes=VMEM_LIMIT),
        name="peer_route",
    )(x, gffn, wqt, k1, k2)


def _packed_rows(row, rows):
    tile = jnp.broadcast_to(row, (2 * SUBLANES, LANES)).astype(BF16)
    return jnp.tile(tile, (rows // tile.shape[0], 1))


def _peer_kernel(x_ref, hb_ref, cnt_ref, e1_ref, rank_ref, w2_ref, u_ref, vt_ref,
                 gfin_ref, o_ref, s_ref, a_ref, acc_ref, *, final_norm, e_tiles, n_tiles):
    g = pl.program_id(0)
    te = 2 * u_ref.shape[0]
    tm = hb_ref.shape[1]
    blocks = te // PEER_N_KEYS
    half_piece = PEER_PIECE // 2
    cur = g % 2
    prev = 1 - cur

    @pl.when(g == 0)
    def _():
        s_ref[...] = jnp.zeros_like(s_ref)
        a_ref[...] = jnp.zeros_like(a_ref)
        acc_ref[...] = jnp.zeros_like(acc_ref)

    tile2 = jnp.clip(g - 1, 0, n_tiles - 1)
    tile3 = jnp.clip(g - 2, 0, n_tiles - 1)
    i1_base = (tile2 % e_tiles) * blocks

    def gate_blocks(r):
        for sub in range(PEER_PIECE // PEER_N_KEYS):
            ib = r * (PEER_PIECE // PEER_N_KEYS) + sub
            i1 = i1_base + ib
            for lt in range(tm // LANES):
                cols = slice(lt * LANES, (lt + 1) * LANES)
                for part in range(PEER_N_KEYS // GATE_ROWS):
                    krows = slice(part * GATE_ROWS // 2, (part + 1) * GATE_ROWS // 2)
                    gate = None
                    for hd in range(PEER_HEADS):
                        cnt = _packed_rows(cnt_ref[hd, lt, pl.ds(i1, 1), :], GATE_ROWS)
                        e1 = _packed_rows(e1_ref[hd, lt, pl.ds(i1, 1), :], GATE_ROWS)
                        term = jnp.where(_halves(rank_ref[hd, lt, krows, :]) < cnt,
                                         _halves(w2_ref[hd, lt, krows, :]) * e1,
                                         jnp.zeros((), BF16))
                        gate = term if gate is None else gate + term
                    lo = ib * PEER_N_KEYS + part * GATE_ROWS
                    bwrow = pl.ds(pl.multiple_of(lo // 2, GATE_ROWS // 2), GATE_ROWS // 2)
                    s = _halves(s_ref[prev, bwrow, cols])
                    act = s * (1.0 + lax.erf(s * INV_SQRT2))
                    a_ref[prev, bwrow, cols] = _words(act * gate)

    def piece(r, carry):
        wrows = pl.ds(pl.multiple_of(r * half_piece, half_piece), half_piece)
        acc_ref[...] += _dot(_halves(vt_ref[r]), _halves(a_ref[cur, wrows, :]))
        gate_blocks(r)
        s_ref[cur, wrows, :] = _words(
            _dot(_halves(u_ref[wrows, :]), _halves(hb_ref[...])).astype(BF16))
        return carry

    lax.fori_loop(0, te // PEER_PIECE, piece, 0)

    @pl.when((g >= 2) & (tile3 % e_tiles == e_tiles - 1))
    def _():
        out = x_ref[...] + acc_ref[...].T
        if final_norm:
            out = _rms(out, gfin_ref[...])
            o_ref[...] = pltpu.einshape("(tb)d->btd", out, b=SUBLANES)
        else:
            o_ref[...] = out
        acc_ref[...] = jnp.zeros_like(acc_ref)


def _peer(x, hb_w, cnt, e1, rank_w, w2_w, u_w, vt_w, gfin, final_norm):
    n = x.shape[0]
    tm, te = PEER_TM, PEER_TE
    e_tiles = PEER_N_EXPERTS // te
    n_tiles = (n // tm) * e_tiles
    last = n_tiles - 1

    def t1(g):
        return jnp.minimum(g, last)

    def t2(g):
        return jnp.clip(g - 1, 0, last)

    def t3(g):
        return jnp.clip(g - 2, 0, last)

    def key_spec(rows):
        return pl.BlockSpec((PEER_HEADS, tm // LANES, rows, LANES),
                            lambda g: (0, t2(g) // e_tiles, 0, 0),
                            pipeline_mode=pl.Buffered(1))

    out_row_spec = pl.BlockSpec((tm, D_MODEL), lambda g: (t3(g) // e_tiles, 0))
    return pl.pallas_call(
        functools.partial(_peer_kernel, final_norm=final_norm, e_tiles=e_tiles,
                          n_tiles=n_tiles),
        grid=(n_tiles + 2,),
        in_specs=[pl.BlockSpec((tm, D_MODEL), lambda g: (t3(g) // e_tiles, 0),
                               pipeline_mode=pl.Buffered(1)),
                  pl.BlockSpec((D_MODEL // 2, tm), lambda g: (0, t1(g) // e_tiles)),
                  key_spec(PEER_N_KEYS), key_spec(PEER_N_KEYS),
                  key_spec(PEER_N_KEYS // 2), key_spec(PEER_N_KEYS // 2),
                  pl.BlockSpec((te // 2, D_MODEL), lambda g: (t1(g) % e_tiles, 0)),
                  pl.BlockSpec((te // PEER_PIECE, D_MODEL // 2, PEER_PIECE),
                               lambda g: (t3(g) % e_tiles, 0, 0)),
                  pl.BlockSpec(gfin.shape, lambda g: (0, 0))],
        out_specs=(pl.BlockSpec((SUBLANES, tm // SUBLANES, D_MODEL),
                                lambda g: (0, t3(g) // e_tiles, 0))
                   if final_norm else out_row_spec),
        out_shape=(jax.ShapeDtypeStruct((SUBLANES, n // SUBLANES, D_MODEL), F32)
                   if final_norm else jax.ShapeDtypeStruct((n, D_MODEL), F32)),
        scratch_shapes=[
            pltpu.VMEM((2, te // 2, tm), jnp.uint32),
            pltpu.VMEM((2, te // 2, tm), jnp.uint32),
            pltpu.VMEM((D_MODEL, tm), F32),
        ],
        compiler_params=pltpu.CompilerParams(
            dimension_semantics=("arbitrary",), vmem_limit_bytes=VMEM_LIMIT),
        name="peer_dense",
    )(x, hb_w, cnt, e1, rank_w, w2_w, u_w, vt_w, gfin)


def _tables_kernel(u_ref, v_ref, uw_ref, vw_ref):
    uw_ref[...] = _words(u_ref[0].astype(BF16))
    vw_ref[0] = _words(v_ref[0].T.astype(BF16))


def _expert_tables(u, v, layer):
    _, e, d = u.shape
    rows = PEER_PIECE
    table_spec = pl.BlockSpec((1, rows, d), lambda i: (layer, i, 0))
    return pl.pallas_call(
        _tables_kernel,
        grid=(e // rows,),
        in_specs=[table_spec, table_spec],
        out_specs=[pl.BlockSpec((rows // 2, d), lambda i: (i, 0)),
                   pl.BlockSpec((1, d // 2, rows), lambda i: (i, 0, 0))],
        out_shape=[jax.ShapeDtypeStruct((e // 2, d), jnp.uint32),
                   jax.ShapeDtypeStruct((e // rows, d // 2, rows), jnp.uint32)],
        compiler_params=pltpu.CompilerParams(
            dimension_semantics=("arbitrary",), vmem_limit_bytes=VMEM_LIMIT),
        name="expert_tables",
    )(u, v)


def _block_diag(blocks):
    g, r, c = blocks.shape
    eye = jnp.eye(g, dtype=blocks.dtype)
    return (blocks[:, :, None, :] * eye[:, None, :, None]).reshape(g * r, g * c)


def kernel(x, norm_mix, w_in, a_re, a_im, log_dt, b_re, b_im, c_re, c_im, d_skip, w_glu, b_glu, w_pool, pool_scale, g_out_ssm, g_out_pool, w_out, norm_ffn, w_q, k1, k2, u_experts, v_experts, norm_final):
    bsz, seq, dm = x.shape
    assert (bsz, dm) == (SUBLANES, D_MODEL) and seq % (MIX_ROWS // SUBLANES) == 0
    depth = w_in.shape[0]
    n = bsz * seq
    row = lambda a: a.reshape(1, -1).astype(F32)

    abr, abi, btr, bti = _discretise(a_re, a_im, log_dt, b_re, b_im)
    xt = x
    gfin = row(norm_final)

    for i in range(depth):
        hg = SSM_GROUPS // 2
        bmat = jnp.stack([_block_diag(b[lo:lo + hg]) for b in (btr[i], bti[i])
                          for lo in (0, hg)]).astype(BF16)
        cre, cim = (jnp.stack([_block_diag(jnp.transpose(c[lo:lo + hg], (0, 2, 1)))
                               for lo in (0, hg)]).astype(BF16)
                    for c in (c_re[i], c_im[i]))
        xt = _mixer(
            xt, row(norm_mix[i]), w_in[i].astype(BF16), bmat,
            abr[i].reshape(1, STATE_W), abi[i].reshape(1, STATE_W), cre, cim,
            row(d_skip[i]), w_glu[i].astype(BF16), row(b_glu[i]),
            _block_diag(w_pool[i]).astype(BF16), row(pool_scale[i]),
            row(g_out_ssm[i]), row(g_out_pool[i]), w_out[i].astype(BF16))
        hb, cnt, e1, rank, w2 = _route(
            xt, row(norm_ffn[i]), jnp.transpose(w_q[i]).astype(BF16),
            k1[i].astype(BF16), k2[i].astype(BF16))
        u_w, vt_w = _expert_tables(u_experts, v_experts, i)
        xt = _peer(xt, hb, cnt, e1, rank, w2, u_w, vt_w, gfin,
                   final_norm=(i == depth - 1))
    return xt
```

```python
import functools

import jax
import jax.numpy as jnp
from jax import lax
from jax.experimental import pallas as pl
from jax.experimental.pallas import tpu as pltpu

F32 = jnp.float32
BF16 = jnp.bfloat16

D_MODEL = 1024
SSM_WIDTH = 512
POOL_WIDTH = 512
SSM_GROUP = 16
SSM_GROUPS = 32
SSM_STATE = 64
STATE_W = SSM_GROUPS * SSM_STATE
HALF_SSM = SSM_WIDTH // 2
HALF_STATE = STATE_W // 2
POOL_WINDOWS = (2, 4, 8, 16)
POOL_GROUP_WIDTH = 128
PEER_HEADS = 8
PEER_N_KEYS = 128
PEER_N_EXPERTS = PEER_N_KEYS * PEER_N_KEYS
PEER_HALF = 128
PEER_TOPK = 16
RMS_EPS = 1e-6

SUBLANES = 8
LANES = 128
MXU_DEPTH = 256
MIX_ROWS = 512
POOL_HIST_ROWS = 128
ROUTE_TM = 512
HEADS_PER_ITER = 2
PEER_TM = 512
PEER_TE = 2048
PEER_PIECE = 1024
GATE_ROWS = 64
VMEM_LIMIT = 60 * 1024 * 1024

NEG_INF = float("-inf")
INV_SQRT2 = 0.7071067811865476


def _rms(x, g):
    return x * lax.rsqrt(jnp.mean(x * x, axis=-1, keepdims=True) + RMS_EPS) * g


def _gelu(x):
    return 0.5 * x * (1.0 + lax.erf(x * INV_SQRT2))


def _dot(a, b):
    return jnp.dot(a, b, preferred_element_type=F32)


def _words(x):
    return pltpu.bitcast(x, jnp.uint32)


def _halves(w):
    return pltpu.bitcast(w, BF16)


def _disc_kernel(are_ref, aim_ref, ldt_ref, bre_ref, bim_ref,
                 abr_ref, abi_ref, btr_ref, bti_ref):
    lam_re = are_ref[...]
    lam_im = aim_ref[...]
    dt = jnp.exp(ldt_ref[...])
    decay = jnp.exp(lam_re * dt)
    abar_re = decay * jnp.cos(lam_im * dt)
    abar_im = decay * jnp.sin(lam_im * dt)
    inv_den = 1.0 / (lam_re * lam_re + lam_im * lam_im)
    num_re = abar_re - 1.0
    zoh_re = (num_re * lam_re + abar_im * lam_im) * inv_den
    zoh_im = (abar_im * lam_re - num_re * lam_im) * inv_den
    b_re = bre_ref[...]
    b_im = bim_ref[...]
    abr_ref[...] = abar_re
    abi_ref[...] = abar_im
    btr_ref[...] = zoh_re * b_re - zoh_im * b_im
    bti_ref[...] = zoh_re * b_im + zoh_im * b_re


def _discretise(a_re, a_im, log_dt, b_re, b_im):
    nl = a_re.shape[0]
    rows = nl * SSM_GROUPS * SSM_GROUP
    shp = (nl, SSM_GROUPS, SSM_GROUP, SSM_STATE)

    def rep(a):
        return jnp.broadcast_to(a[:, :, None, :], shp).reshape(rows, SSM_STATE)

    ldt = jnp.broadcast_to(log_dt[:, :, None, None], shp).reshape(rows, SSM_STATE)
    bre = jnp.transpose(b_re, (0, 1, 3, 2)).reshape(rows, SSM_STATE)
    bim = jnp.transpose(b_im, (0, 1, 3, 2)).reshape(rows, SSM_STATE)
    out = jax.ShapeDtypeStruct((rows, SSM_STATE), F32)
    abr, abi, btr, bti = pl.pallas_call(
        _disc_kernel, out_shape=(out, out, out, out), name="s5_discretise",
    )(rep(a_re), rep(a_im), ldt, bre, bim)
    abr = abr.reshape(shp)[:, :, 0, :].reshape(nl, STATE_W)
    abi = abi.reshape(shp)[:, :, 0, :].reshape(nl, STATE_W)
    return abr, abi, btr.reshape(shp), bti.reshape(shp)


def _mixer_kernel(x_ref, gmix_ref, win_ref, bmat_ref, are_ref, aim_ref,
                  cre_ref, cim_ref, dskip_ref, wglu_ref, bglu_ref, wpool_ref,
                  pscale_ref, gssm_ref, gpool_ref, wout_ref, o_ref,
                  st_ref, sre_ref, sim_ref, ext_ref, *, batch_major_in):
    c = pl.program_id(0)
    rows = o_ref.shape[0]
    steps = rows // SUBLANES

    @pl.when(c == 0)
    def _():
        sre_ref[...] = jnp.zeros_like(sre_ref)
        sim_ref[...] = jnp.zeros_like(sim_ref)
        ext_ref[0:POOL_HIST_ROWS, :] = jnp.zeros((POOL_HIST_ROWS, POOL_WIDTH), F32)

    if batch_major_in:
        xr = pltpu.einshape("btd->(tb)d", x_ref[...])
    else:
        xr = x_ref[...]
    hn = _rms(xr, gmix_ref[...])
    proj = _dot(hn.astype(BF16), win_ref[...])
    u_ssm = proj[:, :SSM_WIDTH]
    u_pool = proj[:, SSM_WIDTH:]

    u_b = u_ssm.astype(BF16)
    for part in range(2):
        for hf in range(2):
            lo = part * STATE_W + hf * HALF_STATE
            st_ref[:, lo:lo + HALF_STATE] = _dot(
                u_b[:, hf * HALF_SSM:(hf + 1) * HALF_SSM], bmat_ref[2 * part + hf])
    a_re = jnp.broadcast_to(are_ref[...], (SUBLANES, STATE_W))
    a_im = jnp.broadcast_to(aim_ref[...], (SUBLANES, STATE_W))

    def step(t, carry):
        s_re, s_im = carry
        r = pl.multiple_of(t * SUBLANES, SUBLANES)
        in_re = st_ref[pl.ds(r, SUBLANES), 0:STATE_W]
        in_im = st_ref[pl.ds(r, SUBLANES), STATE_W:2 * STATE_W]
        n_re = a_re * s_re - a_im * s_im + in_re
        n_im = a_re * s_im + a_im * s_re + in_im
        st_ref[pl.ds(r, SUBLANES), 0:STATE_W] = n_re
        st_ref[pl.ds(r, SUBLANES), STATE_W:2 * STATE_W] = n_im
        return n_re, n_im

    s_re, s_im = lax.fori_loop(0, steps, step, (sre_ref[...], sim_ref[...]))
    sre_ref[...] = s_re
    sim_ref[...] = s_im

    y = []
    for hf in range(2):
        lo = hf * HALF_STATE
        y.append(_dot(st_ref[:, lo:lo + HALF_STATE].astype(BF16), cre_ref[hf])
                 - _dot(st_ref[:, STATE_W + lo:STATE_W + lo + HALF_STATE].astype(BF16),
                        cim_ref[hf]))
    y = jnp.concatenate(y, axis=1) + dskip_ref[...] * u_ssm
    y = _gelu(y)
    y = y * jax.nn.sigmoid(_dot(y.astype(BF16), wglu_ref[...]) + bglu_ref[...])
    ssm_n = _rms(y, gssm_ref[...])

    ext_ref[POOL_HIST_ROWS:, :] = u_pool
    t_idx = c * steps + jnp.right_shift(
        lax.broadcasted_iota(jnp.int32, (rows, POOL_GROUP_WIDTH), 0),
        SUBLANES.bit_length() - 1)
    pooled = []
    for gi, win in enumerate(POOL_WINDOWS):
        lo = gi * POOL_GROUP_WIDTH
        hi = lo + POOL_GROUP_WIDTH
        acc = ext_ref[POOL_HIST_ROWS:, lo:hi]
        for k in range(1, win):
            off = POOL_HIST_ROWS - SUBLANES * k
            acc = acc + ext_ref[off:off + rows, lo:hi]
        count = jnp.minimum(t_idx + 1, win).astype(F32)
        pooled.append(acc / count - ext_ref[POOL_HIST_ROWS:, lo:hi])
    ext_ref[0:POOL_HIST_ROWS, :] = ext_ref[rows:rows + POOL_HIST_ROWS, :]
    pooled = jnp.concatenate(pooled, axis=1)
    y_pool = _dot(pooled.astype(BF16), wpool_ref[...]) * pscale_ref[...]
    pool_n = _rms(y_pool, gpool_ref[...])

    res = (_dot(ssm_n.astype(BF16), wout_ref[0:SSM_WIDTH, :])
           + _dot(pool_n.astype(BF16), wout_ref[SSM_WIDTH:, :]))
    o_ref[...] = xr + res


def _const_spec(shape):
    zeros = (0,) * len(shape)
    return pl.BlockSpec(shape, lambda *_: zeros, pipeline_mode=pl.Buffered(1))


def _mixer(x, gmix, win, bmat, are, aim, cre, cim, dskip, wglu, bglu, wpool,
           pscale, gssm, gpool, wout):
    batch_major_in = x.ndim == 3
    n = x.shape[0] * x.shape[1] if batch_major_in else x.shape[0]
    consts = (gmix, win, bmat, are, aim, cre, cim, dskip, wglu, bglu, wpool,
              pscale, gssm, gpool, wout)
    row_spec = pl.BlockSpec((MIX_ROWS, D_MODEL), lambda c: (c, 0))
    x_spec = (pl.BlockSpec((SUBLANES, MIX_ROWS // SUBLANES, D_MODEL), lambda c: (0, c, 0))
              if batch_major_in else row_spec)
    return pl.pallas_call(
        functools.partial(_mixer_kernel, batch_major_in=batch_major_in),
        grid=(n // MIX_ROWS,),
        in_specs=[x_spec] + [_const_spec(a.shape) for a in consts],
        out_specs=row_spec,
        out_shape=jax.ShapeDtypeStruct((n, D_MODEL), F32),
        scratch_shapes=[
            pltpu.VMEM((MIX_ROWS, 2 * STATE_W), F32),
            pltpu.VMEM((SUBLANES, STATE_W), F32),
            pltpu.VMEM((SUBLANES, STATE_W), F32),
            pltpu.VMEM((POOL_HIST_ROWS + MIX_ROWS, POOL_WIDTH), F32),
        ],
        compiler_params=pltpu.CompilerParams(
            dimension_semantics=("arbitrary",), vmem_limit_bytes=VMEM_LIMIT),
        name="mixer",
    )(x, *consts)


def _sort16_pairs():
    n, pairs, p = 16, [], 1
    while p < n:
        k = p
        while k >= 1:
            for j in range(k % p, n - k, 2 * k):
                for i in range(min(k, n - j - k)):
                    if (i + j) // (2 * p) == (i + j + k) // (2 * p):
                        pairs.append((i + j, i + j + k))
            k //= 2
        p *= 2
    return pairs


_SORT16 = _sort16_pairs()
N_TOP = PEER_TOPK + 1
SORTED_ROWS = -(-N_TOP // SUBLANES) * SUBLANES


def _top_sorted(s, out_ref, cols):
    v = [s[SUBLANES * k:SUBLANES * (k + 1), :] for k in range(16)]
    for i, j in _SORT16:
        hi = jnp.maximum(v[i], v[j])
        lo = jnp.minimum(v[i], v[j])
        v[i], v[j] = hi, lo
    for i in range(N_TOP):
        head = v[0]
        m = jnp.max(head, axis=0, keepdims=True)
        out_ref[i:i + 1, cols] = m
        if i + 1 < N_TOP:
            pop = head == m
            depth = N_TOP - i
            v = [jnp.where(pop, v[k + 1] if k + 1 < len(v) else NEG_INF, v[k])
                 for k in range(depth - 1)]


def _route_kernel(x_ref, gffn_ref, wqt_ref, k1_ref, k2_ref,
                  hb_ref, cnt_ref, e1_ref, rank_ref, w2_ref,
                  qt_ref, l1_all, l2_all):
    tm = x_ref.shape[0]
    h = _rms(x_ref[...], gffn_ref[...])
    hbt = h.T.astype(BF16)
    hb_ref[...] = _words(hbt)
    qt_ref[...] = _dot(wqt_ref[...], hbt)
    row = lax.broadcasted_iota(jnp.int32, (SUBLANES, LANES), 0)

    def route_tile(hd, lt, s1, s2, l1_ref, l2_ref):
        cols = slice(lt * LANES, (lt + 1) * LANES)
        _top_sorted(s1, l1_ref, cols)
        _top_sorted(s2, l2_ref, cols)
        m1 = l1_ref[0:1, cols]
        m2 = l2_ref[0:1, cols]
        a = l1_ref[1:9, cols]
        b = l2_ref[1:9, cols]
        v2_1 = l2_ref[1:2, cols] + a
        cands = [
            m1 + l2_ref[0:8, cols],
            m1 + l2_ref[8:16, cols],
            m2 + a,
            m2 + l1_ref[9:17, cols],
            jnp.where(row < 7, l1_ref[1:2, cols] + b, m1 + l2_ref[9:17, cols]),
            jnp.where(row < 4, l1_ref[2:3, cols] + b, jnp.where(row < 7, v2_1, NEG_INF)),
            jnp.where(row < 3, l1_ref[3:4, cols] + b,
                      jnp.where(row == 3, v2_1,
                                jnp.where(row == 4, l2_ref[2:3, cols] + l1_ref[0:8, cols],
                                          NEG_INF))),
        ]
        tops = []
        for i in range(N_TOP):
            m = cands[0]
            for cnd in cands[1:]:
                m = jnp.maximum(m, cnd)
            m = jnp.max(m, axis=0, keepdims=True)
            tops.append(m)
            if i + 1 < N_TOP:
                cands = [jnp.where(cnd == m, NEG_INF, cnd) for cnd in cands]
        tau = 0.5 * (tops[PEER_TOPK - 1] + tops[PEER_TOPK])
        z = jnp.zeros_like(tau)
        for i in range(PEER_TOPK):
            z = z + jnp.exp(tops[i] - tops[0])
        theta = tau - s1
        cnt = jnp.zeros_like(s1)
        rank = jnp.zeros_like(s2)
        half = PEER_TOPK // 2
        for j in range(half):
            v2j = l2_ref[j:j + 1, cols]
            cnt = jnp.where(v2j >= theta, j + 1.0, cnt)
            rank = jnp.where(v2j > s2, j + 1.0, rank)
        theta_top = tau - m1
        top_more = l2_ref[half:half + 1, cols] >= theta_top
        cnt = jnp.where((s1 == m1) & top_more, half + 1.0, cnt)
        rank = jnp.where((rank == half) & (s2 < theta_top), half + 1.0, rank)
        cnt_ref[hd, lt] = cnt
        e1_ref[hd, lt] = jnp.exp(s1 - m1)
        rank_ref[hd, lt] = _words(rank.astype(BF16))
        w2_ref[hd, lt] = _words((jnp.exp(s2 - m2) * (0.5 / z)).astype(BF16))

    def head_pair(i, carry):
        for k in range(HEADS_PER_ITER):
            hd = i * HEADS_PER_ITER + k
            base = pl.multiple_of(hd * 2 * PEER_HALF, 2 * PEER_HALF)
            q1 = qt_ref[pl.ds(base, PEER_HALF), :].astype(BF16)
            q2 = qt_ref[pl.ds(base + PEER_HALF, PEER_HALF), :].astype(BF16)
            s1 = _dot(k1_ref[...], q1)
            s2 = _dot(k2_ref[...], q2)
            for lt in range(tm // LANES):
                cols = slice(lt * LANES, (lt + 1) * LANES)
                route_tile(hd, lt, s1[:, cols], s2[:, cols], l1_all.at[k], l2_all.at[k])
        return carry

    lax.fori_loop(0, PEER_HEADS // HEADS_PER_ITER, head_pair, 0)


def _route(x, gffn, wqt, k1, k2):
    n = x.shape[0]
    tm = ROUTE_TM
    def key_spec(rows):
        return pl.BlockSpec((PEER_HEADS, tm // LANES, rows, LANES), lambda i: (0, i, 0, 0))

    def key_shape(rows, dtype):
        return jax.ShapeDtypeStruct((PEER_HEADS, n // LANES, rows, LANES), dtype)

    return pl.pallas_call(
        _route_kernel,
        grid=(n // tm,),
        in_specs=[pl.BlockSpec((tm, D_MODEL), lambda i: (i, 0)),
                  _const_spec(gffn.shape), _const_spec(wqt.shape),
                  _const_spec(k1.shape), _const_spec(k2.shape)],
        out_specs=[pl.BlockSpec((D_MODEL // 2, tm), lambda i: (0, i)),
                   key_spec(PEER_N_KEYS), key_spec(PEER_N_KEYS),
                   key_spec(PEER_N_KEYS // 2), key_spec(PEER_N_KEYS // 2)],
        out_shape=[jax.ShapeDtypeStruct((D_MODEL // 2, n), jnp.uint32),
                   key_shape(PEER_N_KEYS, F32), key_shape(PEER_N_KEYS, F32),
                   key_shape(PEER_N_KEYS // 2, jnp.uint32),
                   key_shape(PEER_N_KEYS // 2, jnp.uint32)],
        scratch_shapes=[
            pltpu.VMEM((PEER_HEADS * 2 * PEER_HALF, tm), F32),
            pltpu.VMEM((HEADS_PER_ITER, SORTED_ROWS, tm), F32),
            pltpu.VMEM((HEADS_PER_ITER, SORTED_ROWS, tm), F32),
        ],
        compiler_params=pltpu.CompilerParams(
            dimension_semantics=("arbitrary",), vmem_limit_bytes=VMEM_LIMIT),
        name="peer_route",
    )(x, gffn, wqt, k1, k2)


def _packed_rows(row, rows):
    tile = jnp.broadcast_to(row, (2 * SUBLANES, LANES)).astype(BF16)
    return jnp.tile(tile, (rows // tile.shape[0], 1))


def _peer_kernel(x_ref, hb_ref, cnt_ref, e1_ref, rank_ref, w2_ref, u_ref, vt_ref,
                 gfin_ref, o_ref, s_ref, a_ref, acc_ref, *, final_norm, e_tiles, n_tiles):
    g = pl.program_id(0)
    te = 2 * u_ref.shape[0]
    tm = hb_ref.shape[1]
    blocks = te // PEER_N_KEYS
    half_piece = PEER_PIECE // 2
    cur = g % 2
    prev = 1 - cur

    @pl.when(g == 0)
    def _():
        s_ref[...] = jnp.zeros_like(s_ref)
        a_ref[...] = jnp.zeros_like(a_ref)
        acc_ref[...] = jnp.zeros_like(acc_ref)

    tile2 = jnp.clip(g - 1, 0, n_tiles - 1)
    tile3 = jnp.clip(g - 2, 0, n_tiles - 1)
    i1_base = (tile2 % e_tiles) * blocks

    def gate_blocks(r):
        for sub in range(PEER_PIECE // PEER_N_KEYS):
            ib = r * (PEER_PIECE // PEER_N_KEYS) + sub
            i1 = i1_base + ib
            for lt in range(tm // LANES):
                cols = slice(lt * LANES, (lt + 1) * LANES)
                for part in range(PEER_N_KEYS // GATE_ROWS):
                    krows = slice(part * GATE_ROWS // 2, (part + 1) * GATE_ROWS // 2)
                    gate = None
                    for hd in range(PEER_HEADS):
                        cnt = _packed_rows(cnt_ref[hd, lt, pl.ds(i1, 1), :], GATE_ROWS)
                        e1 = _packed_rows(e1_ref[hd, lt, pl.ds(i1, 1), :], GATE_ROWS)
                        term = jnp.where(_halves(rank_ref[hd, lt, krows, :]) < cnt,
                                         _halves(w2_ref[hd, lt, krows, :]) * e1,
                                         jnp.zeros((), BF16))
                        gate = term if gate is None else gate + term
                    lo = ib * PEER_N_KEYS + part * GATE_ROWS
                    bwrow = pl.ds(pl.multiple_of(lo // 2, GATE_ROWS // 2), GATE_ROWS // 2)
                    s = _halves(s_ref[prev, bwrow, cols])
                    act = s * (1.0 + lax.erf(s * INV_SQRT2))
                    a_ref[prev, bwrow, cols] = _words(act * gate)

    def piece(r, carry):
        wrows = pl.ds(pl.multiple_of(r * half_piece, half_piece), half_piece)
        acc_ref[...] += _dot(_halves(vt_ref[r]), _halves(a_ref[cur, wrows, :]))
        gate_blocks(r)
        s_ref[cur, wrows, :] = _words(
            _dot(_halves(u_ref[wrows, :]), _halves(hb_ref[...])).astype(BF16))
        return carry

    lax.fori_loop(0, te // PEER_PIECE, piece, 0)

    @pl.when((g >= 2) & (tile3 % e_tiles == e_tiles - 1))
    def _():
        out = x_ref[...] + acc_ref[...].T
        if final_norm:
            out = _rms(out, gfin_ref[...])
            o_ref[...] = pltpu.einshape("(tb)d->btd", out, b=SUBLANES)
        else:
            o_ref[...] = out
        acc_ref[...] = jnp.zeros_like(acc_ref)


def _peer(x, hb_w, cnt, e1, rank_w, w2_w, u_w, vt_w, gfin, final_norm):
    n = x.shape[0]
    tm, te = PEER_TM, PEER_TE
    e_tiles = PEER_N_EXPERTS // te
    n_tiles = (n // tm) * e_tiles
    last = n_tiles - 1

    def t1(g):
        return jnp.minimum(g, last)

    def t2(g):
        return jnp.clip(g - 1, 0, last)

    def t3(g):
        return jnp.clip(g - 2, 0, last)

    def key_spec(rows):
        return pl.BlockSpec((PEER_HEADS, tm // LANES, rows, LANES),
                            lambda g: (0, t2(g) // e_tiles, 0, 0),
                            pipeline_mode=pl.Buffered(1))

    out_row_spec = pl.BlockSpec((tm, D_MODEL), lambda g: (t3(g) // e_tiles, 0))
    return pl.pallas_call(
        functools.partial(_peer_kernel, final_norm=final_norm, e_tiles=e_tiles,
                          n_tiles=n_tiles),
        grid=(n_tiles + 2,),
        in_specs=[pl.BlockSpec((tm, D_MODEL), lambda g: (t3(g) // e_tiles, 0),
                               pipeline_mode=pl.Buffered(1)),
                  pl.BlockSpec((D_MODEL // 2, tm), lambda g: (0, t1(g) // e_tiles)),
                  key_spec(PEER_N_KEYS), key_spec(PEER_N_KEYS),
                  key_spec(PEER_N_KEYS // 2), key_spec(PEER_N_KEYS // 2),
                  pl.BlockSpec((te // 2, D_MODEL), lambda g: (t1(g) % e_tiles, 0)),
                  pl.BlockSpec((te // PEER_PIECE, D_MODEL // 2, PEER_PIECE),
                               lambda g: (t3(g) % e_tiles, 0, 0)),
                  pl.BlockSpec(gfin.shape, lambda g: (0, 0))],
        out_specs=(pl.BlockSpec((SUBLANES, tm // SUBLANES, D_MODEL),
                                lambda g: (0, t3(g) // e_tiles, 0))
                   if final_norm else out_row_spec),
        out_shape=(jax.ShapeDtypeStruct((SUBLANES, n // SUBLANES, D_MODEL), F32)
                   if final_norm else jax.ShapeDtypeStruct((n, D_MODEL), F32)),
        scratch_shapes=[
            pltpu.VMEM((2, te // 2, tm), jnp.uint32),
            pltpu.VMEM((2, te // 2, tm), jnp.uint32),
            pltpu.VMEM((D_MODEL, tm), F32),
        ],
        compiler_params=pltpu.CompilerParams(
            dimension_semantics=("arbitrary",), vmem_limit_bytes=VMEM_LIMIT),
        name="peer_dense",
    )(x, hb_w, cnt, e1, rank_w, w2_w, u_w, vt_w, gfin)


def _tables_kernel(u_ref, v_ref, uw_ref, vw_ref):
    uw_ref[...] = _words(u_ref[0].astype(BF16))
    vw_ref[0] = _words(v_ref[0].T.astype(BF16))


def _expert_tables(u, v, layer):
    _, e, d = u.shape
    rows = PEER_PIECE
    table_spec = pl.BlockSpec((1, rows, d), lambda i: (layer, i, 0))
    return pl.pallas_call(
        _tables_kernel,
        grid=(e // rows,),
        in_specs=[table_spec, table_spec],
        out_specs=[pl.BlockSpec((rows // 2, d), lambda i: (i, 0)),
                   pl.BlockSpec((1, d // 2, rows), lambda i: (i, 0, 0))],
        out_shape=[jax.ShapeDtypeStruct((e // 2, d), jnp.uint32),
                   jax.ShapeDtypeStruct((e // rows, d // 2, rows), jnp.uint32)],
        compiler_params=pltpu.CompilerParams(
            dimension_semantics=("arbitrary",), vmem_limit_bytes=VMEM_LIMIT),
        name="expert_tables",
    )(u, v)


def _block_diag(blocks):
    g, r, c = blocks.shape
    eye = jnp.eye(g, dtype=blocks.dtype)
    return (blocks[:, :, None, :] * eye[:, None, :, None]).reshape(g * r, g * c)


def kernel(x, norm_mix, w_in, a_re, a_im, log_dt, b_re, b_im, c_re, c_im, d_skip, w_glu, b_glu, w_pool, pool_scale, g_out_ssm, g_out_pool, w_out, norm_ffn, w_q, k1, k2, u_experts, v_experts, norm_final):
    bsz, seq, dm = x.shape
    assert (bsz, dm) == (SUBLANES, D_MODEL) and seq % (MIX_ROWS // SUBLANES) == 0
    depth = w_in.shape[0]
    n = bsz * seq
    row = lambda a: a.reshape(1, -1).astype(F32)

    abr, abi, btr, bti = _discretise(a_re, a_im, log_dt, b_re, b_im)
    xt = x
    gfin = row(norm_final)

    for i in range(depth):
        hg = SSM_GROUPS // 2
        bmat = jnp.stack([_block_diag(b[lo:lo + hg]) for b in (btr[i], bti[i])
                          for lo in (0, hg)]).astype(BF16)
        cre, cim = (jnp.stack([_block_diag(jnp.transpose(c[lo:lo + hg], (0, 2, 1)))
                               for lo in (0, hg)]).astype(BF16)
                    for c in (c_re[i], c_im[i]))
        xt = _mixer(
            xt, row(norm_mix[i]), w_in[i].astype(BF16), bmat,
            abr[i].reshape(1, STATE_W), abi[i].reshape(1, STATE_W), cre, cim,
            row(d_skip[i]), w_glu[i].astype(BF16), row(b_glu[i]),
            _block_diag(w_pool[i]).astype(BF16), row(pool_scale[i]),
            row(g_out_ssm[i]), row(g_out_pool[i]), w_out[i].astype(BF16))
        hb, cnt, e1, rank, w2 = _route(
            xt, row(norm_ffn[i]), jnp.transpose(w_q[i]).astype(BF16),
            k1[i].astype(BF16), k2[i].astype(BF16))
        u_w, vt_w = _expert_tables(u_experts, v_experts, i)
        xt = _peer(xt, hb, cnt, e1, rank, w2, u_w, vt_w, gfin,
                   final_norm=(i == depth - 1))
    return xt
```

```python
import functools

import jax
import jax.numpy as jnp
from jax import lax
from jax.experimental import pallas as pl
from jax.experimental.pallas import tpu as pltpu

F32 = jnp.float32
BF16 = jnp.bfloat16

D_MODEL = 1024
SSM_WIDTH = 512
POOL_WIDTH = 512
SSM_GROUP = 16
SSM_GROUPS = 32
SSM_STATE = 64
STATE_W = SSM_GROUPS * SSM_STATE
HALF_SSM = SSM_WIDTH // 2
HALF_STATE = STATE_W // 2
POOL_WINDOWS = (2, 4, 8, 16)
POOL_GROUP_WIDTH = 128
PEER_HEADS = 8
PEER_N_KEYS = 128
PEER_N_EXPERTS = PEER_N_KEYS * PEER_N_KEYS
PEER_HALF = 128
PEER_TOPK = 16
RMS_EPS = 1e-6

SUBLANES = 8
LANES = 128
MXU_DEPTH = 256
MIX_ROWS = 512
POOL_HIST_ROWS = 128
ROUTE_TM = 512
HEADS_PER_ITER = 4
PEER_TM = 512
PEER_TE = 2048
PEER_PIECE = 1024
GATE_ROWS = 64
VMEM_LIMIT = 60 * 1024 * 1024

NEG_INF = float("-inf")
INV_SQRT2 = 0.7071067811865476


def _rms(x, g):
    return x * lax.rsqrt(jnp.mean(x * x, axis=-1, keepdims=True) + RMS_EPS) * g


def _gelu(x):
    return 0.5 * x * (1.0 + lax.erf(x * INV_SQRT2))


def _dot(a, b):
    return jnp.dot(a, b, preferred_element_type=F32)


def _words(x):
    return pltpu.bitcast(x, jnp.uint32)


def _halves(w):
    return pltpu.bitcast(w, BF16)


def _disc_kernel(are_ref, aim_ref, ldt_ref, bre_ref, bim_ref,
                 abr_ref, abi_ref, btr_ref, bti_ref):
    lam_re = are_ref[...]
    lam_im = aim_ref[...]
    dt = jnp.exp(ldt_ref[...])
    decay = jnp.exp(lam_re * dt)
    abar_re = decay * jnp.cos(lam_im * dt)
    abar_im = decay * jnp.sin(lam_im * dt)
    inv_den = 1.0 / (lam_re * lam_re + lam_im * lam_im)
    num_re = abar_re - 1.0
    zoh_re = (num_re * lam_re + abar_im * lam_im) * inv_den
    zoh_im = (abar_im * lam_re - num_re * lam_im) * inv_den
    b_re = bre_ref[...]
    b_im = bim_ref[...]
    abr_ref[...] = abar_re
    abi_ref[...] = abar_im
    btr_ref[...] = zoh_re * b_re - zoh_im * b_im
    bti_ref[...] = zoh_re * b_im + zoh_im * b_re


def _discretise(a_re, a_im, log_dt, b_re, b_im):
    nl = a_re.shape[0]
    rows = nl * SSM_GROUPS * SSM_GROUP
    shp = (nl, SSM_GROUPS, SSM_GROUP, SSM_STATE)

    def rep(a):
        return jnp.broadcast_to(a[:, :, None, :], shp).reshape(rows, SSM_STATE)

    ldt = jnp.broadcast_to(log_dt[:, :, None, None], shp).reshape(rows, SSM_STATE)
    bre = jnp.transpose(b_re, (0, 1, 3, 2)).reshape(rows, SSM_STATE)
    bim = jnp.transpose(b_im, (0, 1, 3, 2)).reshape(rows, SSM_STATE)
    out = jax.ShapeDtypeStruct((rows, SSM_STATE), F32)
    abr, abi, btr, bti = pl.pallas_call(
        _disc_kernel, out_shape=(out, out, out, out), name="s5_discretise",
    )(rep(a_re), rep(a_im), ldt, bre, bim)
    abr = abr.reshape(shp)[:, :, 0, :].reshape(nl, STATE_W)
    abi = abi.reshape(shp)[:, :, 0, :].reshape(nl, STATE_W)
    return abr, abi, btr.reshape(shp), bti.reshape(shp)


def _mixer_kernel(x_ref, gmix_ref, win_ref, bmat_ref, are_ref, aim_ref,
                  cre_ref, cim_ref, dskip_ref, wglu_ref, bglu_ref, wpool_ref,
                  pscale_ref, gssm_ref, gpool_ref, wout_ref, o_ref,
                  st_ref, sre_ref, sim_ref, ext_ref, *, batch_major_in):
    c = pl.program_id(0)
    rows = o_ref.shape[0]
    steps = rows // SUBLANES

    @pl.when(c == 0)
    def _():
        sre_ref[...] = jnp.zeros_like(sre_ref)
        sim_ref[...] = jnp.zeros_like(sim_ref)
        ext_ref[0:POOL_HIST_ROWS, :] = jnp.zeros((POOL_HIST_ROWS, POOL_WIDTH), F32)

    if batch_major_in:
        xr = pltpu.einshape("btd->(tb)d", x_ref[...])
    else:
        xr = x_ref[...]
    hn = _rms(xr, gmix_ref[...])
    proj = _dot(hn.astype(BF16), win_ref[...])
    u_ssm = proj[:, :SSM_WIDTH]
    u_pool = proj[:, SSM_WIDTH:]

    u_b = u_ssm.astype(BF16)
    for part in range(2):
        for hf in range(2):
            lo = part * STATE_W + hf * HALF_STATE
            st_ref[:, lo:lo + HALF_STATE] = _dot(
                u_b[:, hf * HALF_SSM:(hf + 1) * HALF_SSM], bmat_ref[2 * part + hf])
    a_re = jnp.broadcast_to(are_ref[...], (SUBLANES, STATE_W))
    a_im = jnp.broadcast_to(aim_ref[...], (SUBLANES, STATE_W))

    def step(t, carry):
        s_re, s_im = carry
        r = pl.multiple_of(t * SUBLANES, SUBLANES)
        in_re = st_ref[pl.ds(r, SUBLANES), 0:STATE_W]
        in_im = st_ref[pl.ds(r, SUBLANES), STATE_W:2 * STATE_W]
        n_re = a_re * s_re - a_im * s_im + in_re
        n_im = a_re * s_im + a_im * s_re + in_im
        st_ref[pl.ds(r, SUBLANES), 0:STATE_W] = n_re
        st_ref[pl.ds(r, SUBLANES), STATE_W:2 * STATE_W] = n_im
        return n_re, n_im

    s_re, s_im = lax.fori_loop(0, steps, step, (sre_ref[...], sim_ref[...]))
    sre_ref[...] = s_re
    sim_ref[...] = s_im

    y = []
    for hf in range(2):
        lo = hf * HALF_STATE
        y.append(_dot(st_ref[:, lo:lo + HALF_STATE].astype(BF16), cre_ref[hf])
                 - _dot(st_ref[:, STATE_W + lo:STATE_W + lo + HALF_STATE].astype(BF16),
                        cim_ref[hf]))
    y = jnp.concatenate(y, axis=1) + dskip_ref[...] * u_ssm
    y = _gelu(y)
    y = y * jax.nn.sigmoid(_dot(y.astype(BF16), wglu_ref[...]) + bglu_ref[...])
    ssm_n = _rms(y, gssm_ref[...])

    ext_ref[POOL_HIST_ROWS:, :] = u_pool
    t_idx = c * steps + jnp.right_shift(
        lax.broadcasted_iota(jnp.int32, (rows, POOL_GROUP_WIDTH), 0),
        SUBLANES.bit_length() - 1)
    pooled = []
    for gi, win in enumerate(POOL_WINDOWS):
        lo = gi * POOL_GROUP_WIDTH
        hi = lo + POOL_GROUP_WIDTH
        acc = ext_ref[POOL_HIST_ROWS:, lo:hi]
        for k in range(1, win):
            off = POOL_HIST_ROWS - SUBLANES * k
            acc = acc + ext_ref[off:off + rows, lo:hi]
        count = jnp.minimum(t_idx + 1, win).astype(F32)
        pooled.append(acc / count - ext_ref[POOL_HIST_ROWS:, lo:hi])
    ext_ref[0:POOL_HIST_ROWS, :] = ext_ref[rows:rows + POOL_HIST_ROWS, :]
    pooled = jnp.concatenate(pooled, axis=1)
    y_pool = _dot(pooled.astype(BF16), wpool_ref[...]) * pscale_ref[...]
    pool_n = _rms(y_pool, gpool_ref[...])

    res = (_dot(ssm_n.astype(BF16), wout_ref[0:SSM_WIDTH, :])
           + _dot(pool_n.astype(BF16), wout_ref[SSM_WIDTH:, :]))
    o_ref[...] = xr + res


def _const_spec(shape):
    zeros = (0,) * len(shape)
    return pl.BlockSpec(shape, lambda *_: zeros, pipeline_mode=pl.Buffered(1))


def _mixer(x, gmix, win, bmat, are, aim, cre, cim, dskip, wglu, bglu, wpool,
           pscale, gssm, gpool, wout):
    batch_major_in = x.ndim == 3
    n = x.shape[0] * x.shape[1] if batch_major_in else x.shape[0]
    consts = (gmix, win, bmat, are, aim, cre, cim, dskip, wglu, bglu, wpool,
              pscale, gssm, gpool, wout)
    row_spec = pl.BlockSpec((MIX_ROWS, D_MODEL), lambda c: (c, 0))
    x_spec = (pl.BlockSpec((SUBLANES, MIX_ROWS // SUBLANES, D_MODEL), lambda c: (0, c, 0))
              if batch_major_in else row_spec)
    return pl.pallas_call(
        functools.partial(_mixer_kernel, batch_major_in=batch_major_in),
        grid=(n // MIX_ROWS,),
        in_specs=[x_spec] + [_const_spec(a.shape) for a in consts],
        out_specs=row_spec,
        out_shape=jax.ShapeDtypeStruct((n, D_MODEL), F32),
        scratch_shapes=[
            pltpu.VMEM((MIX_ROWS, 2 * STATE_W), F32),
            pltpu.VMEM((SUBLANES, STATE_W), F32),
            pltpu.VMEM((SUBLANES, STATE_W), F32),
            pltpu.VMEM((POOL_HIST_ROWS + MIX_ROWS, POOL_WIDTH), F32),
        ],
        compiler_params=pltpu.CompilerParams(
            dimension_semantics=("arbitrary",), vmem_limit_bytes=VMEM_LIMIT),
        name="mixer",
    )(x, *consts)


def _sort16_pairs():
    n, pairs, p = 16, [], 1
    while p < n:
        k = p
        while k >= 1:
            for j in range(k % p, n - k, 2 * k):
                for i in range(min(k, n - j - k)):
                    if (i + j) // (2 * p) == (i + j + k) // (2 * p):
                        pairs.append((i + j, i + j + k))
            k //= 2
        p *= 2
    return pairs


_SORT16 = _sort16_pairs()
N_TOP = PEER_TOPK + 1
SORTED_ROWS = -(-N_TOP // SUBLANES) * SUBLANES


def _top_sorted(s, out_ref, cols):
    v = [s[SUBLANES * k:SUBLANES * (k + 1), :] for k in range(16)]
    for i, j in _SORT16:
        hi = jnp.maximum(v[i], v[j])
        lo = jnp.minimum(v[i], v[j])
        v[i], v[j] = hi, lo
    for i in range(N_TOP):
        head = v[0]
        m = jnp.max(head, axis=0, keepdims=True)
        out_ref[i:i + 1, cols] = m
        if i + 1 < N_TOP:
            pop = head == m
            depth = N_TOP - i
            v = [jnp.where(pop, v[k + 1] if k + 1 < len(v) else NEG_INF, v[k])
                 for k in range(depth - 1)]


def _route_kernel(x_ref, gffn_ref, wqt_ref, k1_ref, k2_ref,
                  hb_ref, cnt_ref, e1_ref, rank_ref, w2_ref,
                  qt_ref, l1_all, l2_all):
    tm = x_ref.shape[0]
    h = _rms(x_ref[...], gffn_ref[...])
    hbt = h.T.astype(BF16)
    hb_ref[...] = _words(hbt)
    qt_ref[...] = _dot(wqt_ref[...], hbt)
    row = lax.broadcasted_iota(jnp.int32, (SUBLANES, LANES), 0)

    def route_tile(hd, lt, s1, s2, l1_ref, l2_ref):
        cols = slice(lt * LANES, (lt + 1) * LANES)
        _top_sorted(s1, l1_ref, cols)
        _top_sorted(s2, l2_ref, cols)
        m1 = l1_ref[0:1, cols]
        m2 = l2_ref[0:1, cols]
        a = l1_ref[1:9, cols]
        b = l2_ref[1:9, cols]
        v2_1 = l2_ref[1:2, cols] + a
        cands = [
            m1 + l2_ref[0:8, cols],
            m1 + l2_ref[8:16, cols],
            m2 + a,
            m2 + l1_ref[9:17, cols],
            jnp.where(row < 7, l1_ref[1:2, cols] + b, m1 + l2_ref[9:17, cols]),
            jnp.where(row < 4, l1_ref[2:3, cols] + b, jnp.where(row < 7, v2_1, NEG_INF)),
            jnp.where(row < 3, l1_ref[3:4, cols] + b,
                      jnp.where(row == 3, v2_1,
                                jnp.where(row == 4, l2_ref[2:3, cols] + l1_ref[0:8, cols],
                                          NEG_INF))),
        ]
        tops = []
        for i in range(N_TOP):
            m = cands[0]
            for cnd in cands[1:]:
                m = jnp.maximum(m, cnd)
            m = jnp.max(m, axis=0, keepdims=True)
            tops.append(m)
            if i + 1 < N_TOP:
                cands = [jnp.where(cnd == m, NEG_INF, cnd) for cnd in cands]
        tau = 0.5 * (tops[PEER_TOPK - 1] + tops[PEER_TOPK])
        z = jnp.zeros_like(tau)
        for i in range(PEER_TOPK):
            z = z + jnp.exp(tops[i] - tops[0])
        theta = tau - s1
        cnt = jnp.zeros_like(s1)
        rank = jnp.zeros_like(s2)
        half = PEER_TOPK // 2
        for j in range(half):
            v2j = l2_ref[j:j + 1, cols]
            cnt = jnp.where(v2j >= theta, j + 1.0, cnt)
            rank = jnp.where(v2j > s2, j + 1.0, rank)
        theta_top = tau - m1
        top_more = l2_ref[half:half + 1, cols] >= theta_top
        cnt = jnp.where((s1 == m1) & top_more, half + 1.0, cnt)
        rank = jnp.where((rank == half) & (s2 < theta_top), half + 1.0, rank)
        cnt_ref[hd, lt] = cnt
        e1_ref[hd, lt] = jnp.exp(s1 - m1)
        rank_ref[hd, lt] = _words(rank.astype(BF16))
        w2_ref[hd, lt] = _words((jnp.exp(s2 - m2) * (0.5 / z)).astype(BF16))

    def head_pair(i, carry):
        for k in range(HEADS_PER_ITER):
            hd = i * HEADS_PER_ITER + k
            base = pl.multiple_of(hd * 2 * PEER_HALF, 2 * PEER_HALF)
            q1 = qt_ref[pl.ds(base, PEER_HALF), :].astype(BF16)
            q2 = qt_ref[pl.ds(base + PEER_HALF, PEER_HALF), :].astype(BF16)
            s1 = _dot(k1_ref[...], q1)
            s2 = _dot(k2_ref[...], q2)
            for lt in range(tm // LANES):
                cols = slice(lt * LANES, (lt + 1) * LANES)
                route_tile(hd, lt, s1[:, cols], s2[:, cols], l1_all.at[k], l2_all.at[k])
        return carry

    lax.fori_loop(0, PEER_HEADS // HEADS_PER_ITER, head_pair, 0)


def _route(x, gffn, wqt, k1, k2):
    n = x.shape[0]
    tm = ROUTE_TM
    def key_spec(rows):
        return pl.BlockSpec((PEER_HEADS, tm // LANES, rows, LANES), lambda i: (0, i, 0, 0))

    def key_shape(rows, dtype):
        return jax.ShapeDtypeStruct((PEER_HEADS, n // LANES, rows, LANES), dtype)

    return pl.pallas_call(
        _route_kernel,
        grid=(n // tm,),
        in_specs=[pl.BlockSpec((tm, D_MODEL), lambda i: (i, 0)),
                  _const_spec(gffn.shape), _const_spec(wqt.shape),
                  _const_spec(k1.shape), _const_spec(k2.shape)],
        out_specs=[pl.BlockSpec((D_MODEL // 2, tm), lambda i: (0, i)),
                   key_spec(PEER_N_KEYS), key_spec(PEER_N_KEYS),
                   key_spec(PEER_N_KEYS // 2), key_spec(PEER_N_KEYS // 2)],
        out_shape=[jax.ShapeDtypeStruct((D_MODEL // 2, n), jnp.uint32),
                   key_shape(PEER_N_KEYS, F32), key_shape(PEER_N_KEYS, F32),
                   key_shape(PEER_N_KEYS // 2, jnp.uint32),
                   key_shape(PEER_N_KEYS // 2, jnp.uint32)],
        scratch_shapes=[
            pltpu.VMEM((PEER_HEADS * 2 * PEER_HALF, tm), F32),
            pltpu.VMEM((HEADS_PER_ITER, SORTED_ROWS, tm), F32),
            pltpu.VMEM((HEADS_PER_ITER, SORTED_ROWS, tm), F32),
        ],
        compiler_params=pltpu.CompilerParams(
            dimension_semantics=("arbitrary",), vmem_limit_bytes=VMEM_LIMIT),
        name="peer_route",
    )(x, gffn, wqt, k1, k2)


def _packed_rows(row, rows):
    tile = jnp.broadcast_to(row, (2 * SUBLANES, LANES)).astype(BF16)
    return jnp.tile(tile, (rows // tile.shape[0], 1))


def _peer_kernel(x_ref, hb_ref, cnt_ref, e1_ref, rank_ref, w2_ref, u_ref, vt_ref,
                 gfin_ref, o_ref, s_ref, a_ref, acc_ref, *, final_norm, e_tiles, n_tiles):
    g = pl.program_id(0)
    te = 2 * u_ref.shape[0]
    tm = hb_ref.shape[1]
    blocks = te // PEER_N_KEYS
    half_piece = PEER_PIECE // 2
    cur = g % 2
    prev = 1 - cur

    @pl.when(g == 0)
    def _():
        s_ref[...] = jnp.zeros_like(s_ref)
        a_ref[...] = jnp.zeros_like(a_ref)
        acc_ref[...] = jnp.zeros_like(acc_ref)

    tile2 = jnp.clip(g - 1, 0, n_tiles - 1)
    tile3 = jnp.clip(g - 2, 0, n_tiles - 1)
    i1_base = (tile2 % e_tiles) * blocks

    def gate_blocks(r):
        for sub in range(PEER_PIECE // PEER_N_KEYS):
            ib = r * (PEER_PIECE // PEER_N_KEYS) + sub
            i1 = i1_base + ib
            for lt in range(tm // LANES):
                cols = slice(lt * LANES, (lt + 1) * LANES)
                for part in range(PEER_N_KEYS // GATE_ROWS):
                    krows = slice(part * GATE_ROWS // 2, (part + 1) * GATE_ROWS // 2)
                    gate = None
                    for hd in range(PEER_HEADS):
                        cnt = _packed_rows(cnt_ref[hd, lt, pl.ds(i1, 1), :], GATE_ROWS)
                        e1 = _packed_rows(e1_ref[hd, lt, pl.ds(i1, 1), :], GATE_ROWS)
                        term = jnp.where(_halves(rank_ref[hd, lt, krows, :]) < cnt,
                                         _halves(w2_ref[hd, lt, krows, :]) * e1,
                                         jnp.zeros((), BF16))
                        gate = term if gate is None else gate + term
                    lo = ib * PEER_N_KEYS + part * GATE_ROWS
                    bwrow = pl.ds(pl.multiple_of(lo // 2, GATE_ROWS // 2), GATE_ROWS // 2)
                    s = _halves(s_ref[prev, bwrow, cols])
                    act = s * (1.0 + lax.erf(s * INV_SQRT2))
                    a_ref[prev, bwrow, cols] = _words(act * gate)

    def piece(r, carry):
        wrows = pl.ds(pl.multiple_of(r * half_piece, half_piece), half_piece)
        acc_ref[...] += _dot(_halves(vt_ref[r]), _halves(a_ref[cur, wrows, :]))
        gate_blocks(r)
        s_ref[cur, wrows, :] = _words(
            _dot(_halves(u_ref[wrows, :]), _halves(hb_ref[...])).astype(BF16))
        return carry

    lax.fori_loop(0, te // PEER_PIECE, piece, 0)

    @pl.when((g >= 2) & (tile3 % e_tiles == e_tiles - 1))
    def _():
        out = x_ref[...] + acc_ref[...].T
        if final_norm:
            out = _rms(out, gfin_ref[...])
            o_ref[...] = pltpu.einshape("(tb)d->btd", out, b=SUBLANES)
        else:
            o_ref[...] = out
        acc_ref[...] = jnp.zeros_like(acc_ref)


def _peer(x, hb_w, cnt, e1, rank_w, w2_w, u_w, vt_w, gfin, final_norm):
    n = x.shape[0]
    tm, te = PEER_TM, PEER_TE
    e_tiles = PEER_N_EXPERTS // te
    n_tiles = (n // tm) * e_tiles
    last = n_tiles - 1

    def t1(g):
        return jnp.minimum(g, last)

    def t2(g):
        return jnp.clip(g - 1, 0, last)

    def t3(g):
        return jnp.clip(g - 2, 0, last)

    def key_spec(rows):
        return pl.BlockSpec((PEER_HEADS, tm // LANES, rows, LANES),
                            lambda g: (0, t2(g) // e_tiles, 0, 0),
                            pipeline_mode=pl.Buffered(1))

    out_row_spec = pl.BlockSpec((tm, D_MODEL), lambda g: (t3(g) // e_tiles, 0))
    return pl.pallas_call(
        functools.partial(_peer_kernel, final_norm=final_norm, e_tiles=e_tiles,
                          n_tiles=n_tiles),
        grid=(n_tiles + 2,),
        in_specs=[pl.BlockSpec((tm, D_MODEL), lambda g: (t3(g) // e_tiles, 0),
                               pipeline_mode=pl.Buffered(1)),
                  pl.BlockSpec((D_MODEL // 2, tm), lambda g: (0, t1(g) // e_tiles)),
                  key_spec(PEER_N_KEYS), key_spec(PEER_N_KEYS),
                  key_spec(PEER_N_KEYS // 2), key_spec(PEER_N_KEYS // 2),
                  pl.BlockSpec((te // 2, D_MODEL), lambda g: (t1(g) % e_tiles, 0)),
                  pl.BlockSpec((te // PEER_PIECE, D_MODEL // 2, PEER_PIECE),
                               lambda g: (t3(g) % e_tiles, 0, 0)),
                  pl.BlockSpec(gfin.shape, lambda g: (0, 0))],
        out_specs=(pl.BlockSpec((SUBLANES, tm // SUBLANES, D_MODEL),
                                lambda g: (0, t3(g) // e_tiles, 0))
                   if final_norm else out_row_spec),
        out_shape=(jax.ShapeDtypeStruct((SUBLANES, n // SUBLANES, D_MODEL), F32)
                   if final_norm else jax.ShapeDtypeStruct((n, D_MODEL), F32)),
        scratch_shapes=[
            pltpu.VMEM((2, te // 2, tm), jnp.uint32),
            pltpu.VMEM((2, te // 2, tm), jnp.uint32),
            pltpu.VMEM((D_MODEL, tm), F32),
        ],
        compiler_params=pltpu.CompilerParams(
            dimension_semantics=("arbitrary",), vmem_limit_bytes=VMEM_LIMIT),
        name="peer_dense",
    )(x, hb_w, cnt, e1, rank_w, w2_w, u_w, vt_w, gfin)


def _tables_kernel(u_ref, v_ref, uw_ref, vw_ref):
    uw_ref[...] = _words(u_ref[0].astype(BF16))
    vw_ref[0] = _words(v_ref[0].T.astype(BF16))


def _expert_tables(u, v, layer):
    _, e, d = u.shape
    rows = PEER_PIECE
    table_spec = pl.BlockSpec((1, rows, d), lambda i: (layer, i, 0))
    return pl.pallas_call(
        _tables_kernel,
        grid=(e // rows,),
        in_specs=[table_spec, table_spec],
        out_specs=[pl.BlockSpec((rows // 2, d), lambda i: (i, 0)),
                   pl.BlockSpec((1, d // 2, rows), lambda i: (i, 0, 0))],
        out_shape=[jax.ShapeDtypeStruct((e // 2, d), jnp.uint32),
                   jax.ShapeDtypeStruct((e // rows, d // 2, rows), jnp.uint32)],
        compiler_params=pltpu.CompilerParams(
            dimension_semantics=("arbitrary",), vmem_limit_bytes=VMEM_LIMIT),
        name="expert_tables",
    )(u, v)


def _block_diag(blocks):
    g, r, c = blocks.shape
    eye = jnp.eye(g, dtype=blocks.dtype)
    return (blocks[:, :, None, :] * eye[:, None, :, None]).reshape(g * r, g * c)


def kernel(x, norm_mix, w_in, a_re, a_im, log_dt, b_re, b_im, c_re, c_im, d_skip, w_glu, b_glu, w_pool, pool_scale, g_out_ssm, g_out_pool, w_out, norm_ffn, w_q, k1, k2, u_experts, v_experts, norm_final):
    bsz, seq, dm = x.shape
    assert (bsz, dm) == (SUBLANES, D_MODEL) and seq % (MIX_ROWS // SUBLANES) == 0
    depth = w_in.shape[0]
    n = bsz * seq
    row = lambda a: a.reshape(1, -1).astype(F32)

    abr, abi, btr, bti = _discretise(a_re, a_im, log_dt, b_re, b_im)
    xt = x
    gfin = row(norm_final)

    for i in range(depth):
        hg = SSM_GROUPS // 2
        bmat = jnp.stack([_block_diag(b[lo:lo + hg]) for b in (btr[i], bti[i])
                          for lo in (0, hg)]).astype(BF16)
        cre, cim = (jnp.stack([_block_diag(jnp.transpose(c[lo:lo + hg], (0, 2, 1)))
                               for lo in (0, hg)]).astype(BF16)
                    for c in (c_re[i], c_im[i]))
        xt = _mixer(
            xt, row(norm_mix[i]), w_in[i].astype(BF16), bmat,
            abr[i].reshape(1, STATE_W), abi[i].reshape(1, STATE_W), cre, cim,
            row(d_skip[i]), w_glu[i].astype(BF16), row(b_glu[i]),
            _block_diag(w_pool[i]).astype(BF16), row(pool_scale[i]),
            row(g_out_ssm[i]), row(g_out_pool[i]), w_out[i].astype(BF16))
        hb, cnt, e1, rank, w2 = _route(
            xt, row(norm_ffn[i]), jnp.transpose(w_q[i]).astype(BF16),
            k1[i].astype(BF16), k2[i].astype(BF16))
        u_w, vt_w = _expert_tables(u_experts, v_experts, i)
        xt = _peer(xt, hb, cnt, e1, rank, w2, u_w, vt_w, gfin,
                   final_norm=(i == depth - 1))
    return xt
```

```python
import functools

import jax
import jax.numpy as jnp
from jax import lax
from jax.experimental import pallas as pl
from jax.experimental.pallas import tpu as pltpu

F32 = jnp.float32
BF16 = jnp.bfloat16

D_MODEL = 1024
SSM_WIDTH = 512
POOL_WIDTH = 512
SSM_GROUP = 16
SSM_GROUPS = 32
SSM_STATE = 64
STATE_W = SSM_GROUPS * SSM_STATE
HALF_SSM = SSM_WIDTH // 2
HALF_STATE = STATE_W // 2
POOL_WINDOWS = (2, 4, 8, 16)
POOL_GROUP_WIDTH = 128
PEER_HEADS = 8
PEER_N_KEYS = 128
PEER_N_EXPERTS = PEER_N_KEYS * PEER_N_KEYS
PEER_HALF = 128
PEER_TOPK = 16
RMS_EPS = 1e-6

SUBLANES = 8
LANES = 128
MXU_DEPTH = 256
MIX_ROWS = 512
POOL_HIST_ROWS = 128
ROUTE_TM = 512
HEADS_PER_ITER = 4
PEER_TM = 512
PEER_TE = 2048
PEER_PIECE = 1024
GATE_ROWS = 64
VMEM_LIMIT = 60 * 1024 * 1024

NEG_INF = float("-inf")
INV_SQRT2 = 0.7071067811865476


def _rms(x, g):
    return x * lax.rsqrt(jnp.mean(x * x, axis=-1, keepdims=True) + RMS_EPS) * g


def _gelu(x):
    return 0.5 * x * (1.0 + lax.erf(x * INV_SQRT2))


def _dot(a, b):
    return jnp.dot(a, b, preferred_element_type=F32)


def _words(x):
    return pltpu.bitcast(x, jnp.uint32)


def _halves(w):
    return pltpu.bitcast(w, BF16)


def _disc_kernel(are_ref, aim_ref, ldt_ref, bre_ref, bim_ref,
                 abr_ref, abi_ref, btr_ref, bti_ref):
    lam_re = are_ref[...]
    lam_im = aim_ref[...]
    dt = jnp.exp(ldt_ref[...])
    decay = jnp.exp(lam_re * dt)
    abar_re = decay * jnp.cos(lam_im * dt)
    abar_im = decay * jnp.sin(lam_im * dt)
    inv_den = 1.0 / (lam_re * lam_re + lam_im * lam_im)
    num_re = abar_re - 1.0
    zoh_re = (num_re * lam_re + abar_im * lam_im) * inv_den
    zoh_im = (abar_im * lam_re - num_re * lam_im) * inv_den
    b_re = bre_ref[...]
    b_im = bim_ref[...]
    abr_ref[...] = abar_re
    abi_ref[...] = abar_im
    btr_ref[...] = zoh_re * b_re - zoh_im * b_im
    bti_ref[...] = zoh_re * b_im + zoh_im * b_re


def _discretise(a_re, a_im, log_dt, b_re, b_im):
    nl = a_re.shape[0]
    rows = nl * SSM_GROUPS * SSM_GROUP
    shp = (nl, SSM_GROUPS, SSM_GROUP, SSM_STATE)

    def rep(a):
        return jnp.broadcast_to(a[:, :, None, :], shp).reshape(rows, SSM_STATE)

    ldt = jnp.broadcast_to(log_dt[:, :, None, None], shp).reshape(rows, SSM_STATE)
    bre = jnp.transpose(b_re, (0, 1, 3, 2)).reshape(rows, SSM_STATE)
    bim = jnp.transpose(b_im, (0, 1, 3, 2)).reshape(rows, SSM_STATE)
    out = jax.ShapeDtypeStruct((rows, SSM_STATE), F32)
    abr, abi, btr, bti = pl.pallas_call(
        _disc_kernel, out_shape=(out, out, out, out), name="s5_discretise",
    )(rep(a_re), rep(a_im), ldt, bre, bim)
    abr = abr.reshape(shp)[:, :, 0, :].reshape(nl, STATE_W)
    abi = abi.reshape(shp)[:, :, 0, :].reshape(nl, STATE_W)
    return abr, abi, btr.reshape(shp), bti.reshape(shp)


def _mixer_kernel(x_ref, gmix_ref, win_ref, bmat_ref, are_ref, aim_ref,
                  cre_ref, cim_ref, dskip_ref, wglu_ref, bglu_ref, wpool_ref,
                  pscale_ref, gssm_ref, gpool_ref, wout_ref, o_ref,
                  st_ref, sre_ref, sim_ref, ext_ref, *, batch_major_in):
    c = pl.program_id(0)
    rows = o_ref.shape[0]
    steps = rows // SUBLANES

    @pl.when(c == 0)
    def _():
        sre_ref[...] = jnp.zeros_like(sre_ref)
        sim_ref[...] = jnp.zeros_like(sim_ref)
        ext_ref[0:POOL_HIST_ROWS, :] = jnp.zeros((POOL_HIST_ROWS, POOL_WIDTH), F32)

    if batch_major_in:
        xr = pltpu.einshape("btd->(tb)d", x_ref[...])
    else:
        xr = x_ref[...]
    hn = _rms(xr, gmix_ref[...])
    proj = _dot(hn.astype(BF16), win_ref[...])
    u_ssm = proj[:, :SSM_WIDTH]
    u_pool = proj[:, SSM_WIDTH:]

    u_b = u_ssm.astype(BF16)
    for part in range(2):
        for hf in range(2):
            lo = part * STATE_W + hf * HALF_STATE
            st_ref[:, lo:lo + HALF_STATE] = _dot(
                u_b[:, hf * HALF_SSM:(hf + 1) * HALF_SSM], bmat_ref[2 * part + hf])
    a_re = jnp.broadcast_to(are_ref[...], (SUBLANES, STATE_W))
    a_im = jnp.broadcast_to(aim_ref[...], (SUBLANES, STATE_W))

    def step(t, carry):
        s_re, s_im = carry
        r = pl.multiple_of(t * SUBLANES, SUBLANES)
        in_re = st_ref[pl.ds(r, SUBLANES), 0:STATE_W]
        in_im = st_ref[pl.ds(r, SUBLANES), STATE_W:2 * STATE_W]
        n_re = a_re * s_re - a_im * s_im + in_re
        n_im = a_re * s_im + a_im * s_re + in_im
        st_ref[pl.ds(r, SUBLANES), 0:STATE_W] = n_re
        st_ref[pl.ds(r, SUBLANES), STATE_W:2 * STATE_W] = n_im
        return n_re, n_im

    s_re, s_im = lax.fori_loop(0, steps, step, (sre_ref[...], sim_ref[...]))
    sre_ref[...] = s_re
    sim_ref[...] = s_im

    y = []
    for hf in range(2):
        lo = hf * HALF_STATE
        y.append(_dot(st_ref[:, lo:lo + HALF_STATE].astype(BF16), cre_ref[hf])
                 - _dot(st_ref[:, STATE_W + lo:STATE_W + lo + HALF_STATE].astype(BF16),
                        cim_ref[hf]))
    y = jnp.concatenate(y, axis=1) + dskip_ref[...] * u_ssm
    y = _gelu(y)
    y = y * jax.nn.sigmoid(_dot(y.astype(BF16), wglu_ref[...]) + bglu_ref[...])
    ssm_n = _rms(y, gssm_ref[...])

    ext_ref[POOL_HIST_ROWS:, :] = u_pool
    t_idx = c * steps + jnp.right_shift(
        lax.broadcasted_iota(jnp.int32, (rows, POOL_GROUP_WIDTH), 0),
        SUBLANES.bit_length() - 1)
    pooled = []
    for gi, win in enumerate(POOL_WINDOWS):
        lo = gi * POOL_GROUP_WIDTH
        hi = lo + POOL_GROUP_WIDTH
        acc = ext_ref[POOL_HIST_ROWS:, lo:hi]
        for k in range(1, win):
            off = POOL_HIST_ROWS - SUBLANES * k
            acc = acc + ext_ref[off:off + rows, lo:hi]
        count = jnp.minimum(t_idx + 1, win).astype(F32)
        pooled.append(acc / count - ext_ref[POOL_HIST_ROWS:, lo:hi])
    ext_ref[0:POOL_HIST_ROWS, :] = ext_ref[rows:rows + POOL_HIST_ROWS, :]
    pooled = jnp.concatenate(pooled, axis=1)
    y_pool = _dot(pooled.astype(BF16), wpool_ref[...]) * pscale_ref[...]
    pool_n = _rms(y_pool, gpool_ref[...])

    res = (_dot(ssm_n.astype(BF16), wout_ref[0:SSM_WIDTH, :])
           + _dot(pool_n.astype(BF16), wout_ref[SSM_WIDTH:, :]))
    o_ref[...] = xr + res


def _const_spec(shape):
    zeros = (0,) * len(shape)
    return pl.BlockSpec(shape, lambda *_: zeros, pipeline_mode=pl.Buffered(1))


def _mixer(x, gmix, win, bmat, are, aim, cre, cim, dskip, wglu, bglu, wpool,
           pscale, gssm, gpool, wout):
    batch_major_in = x.ndim == 3
    n = x.shape[0] * x.shape[1] if batch_major_in else x.shape[0]
    consts = (gmix, win, bmat, are, aim, cre, cim, dskip, wglu, bglu, wpool,
              pscale, gssm, gpool, wout)
    row_spec = pl.BlockSpec((MIX_ROWS, D_MODEL), lambda c: (c, 0))
    x_spec = (pl.BlockSpec((SUBLANES, MIX_ROWS // SUBLANES, D_MODEL), lambda c: (0, c, 0))
              if batch_major_in else row_spec)
    return pl.pallas_call(
        functools.partial(_mixer_kernel, batch_major_in=batch_major_in),
        grid=(n // MIX_ROWS,),
        in_specs=[x_spec] + [_const_spec(a.shape) for a in consts],
        out_specs=row_spec,
        out_shape=jax.ShapeDtypeStruct((n, D_MODEL), F32),
        scratch_shapes=[
            pltpu.VMEM((MIX_ROWS, 2 * STATE_W), F32),
            pltpu.VMEM((SUBLANES, STATE_W), F32),
            pltpu.VMEM((SUBLANES, STATE_W), F32),
            pltpu.VMEM((POOL_HIST_ROWS + MIX_ROWS, POOL_WIDTH), F32),
        ],
        compiler_params=pltpu.CompilerParams(
            dimension_semantics=("arbitrary",), vmem_limit_bytes=VMEM_LIMIT),
        name="mixer",
    )(x, *consts)


def _sort16_pairs():
    n, pairs, p = 16, [], 1
    while p < n:
        k = p
        while k >= 1:
            for j in range(k % p, n - k, 2 * k):
                for i in range(min(k, n - j - k)):
                    if (i + j) // (2 * p) == (i + j + k) // (2 * p):
                        pairs.append((i + j, i + j + k))
            k //= 2
        p *= 2
    return pairs


_SORT16 = _sort16_pairs()
N_TOP = PEER_TOPK + 1
SORTED_ROWS = -(-N_TOP // SUBLANES) * SUBLANES


def _top_sorted(s, out_ref, cols):
    v = [s[SUBLANES * k:SUBLANES * (k + 1), :] for k in range(16)]
    for i, j in _SORT16:
        hi = jnp.maximum(v[i], v[j])
        lo = jnp.minimum(v[i], v[j])
        v[i], v[j] = hi, lo
    for i in range(N_TOP):
        head = v[0]
        m = jnp.max(head, axis=0, keepdims=True)
        out_ref[i:i + 1, cols] = m
        if i + 1 < N_TOP:
            pop = head == m
            depth = N_TOP - i
            v = [jnp.where(pop, v[k + 1] if k + 1 < len(v) else NEG_INF, v[k])
                 for k in range(depth - 1)]


def _route_kernel(x_ref, gffn_ref, wqt_ref, k1_ref, k2_ref,
                  hb_ref, cnt_ref, e1_ref, rank_ref, w2_ref,
                  qt_ref, l1_all, l2_all):
    tm = x_ref.shape[0]
    h = _rms(x_ref[...], gffn_ref[...])
    hbt = h.T.astype(BF16)
    hb_ref[...] = _words(hbt)
    qt_ref[...] = _dot(wqt_ref[...], hbt)
    row = lax.broadcasted_iota(jnp.int32, (SUBLANES, LANES), 0)

    def route_tile(hd, lt, s1, s2, l1_ref, l2_ref):
        cols = slice(lt * LANES, (lt + 1) * LANES)
        _top_sorted(s1, l1_ref, cols)
        _top_sorted(s2, l2_ref, cols)
        m1 = l1_ref[0:1, cols]
        m2 = l2_ref[0:1, cols]
        a = l1_ref[1:9, cols]
        b = l2_ref[1:9, cols]
        v2_1 = l2_ref[1:2, cols] + a
        cands = [
            m1 + l2_ref[0:8, cols],
            m1 + l2_ref[8:16, cols],
            m2 + a,
            m2 + l1_ref[9:17, cols],
            jnp.where(row < 7, l1_ref[1:2, cols] + b, m1 + l2_ref[9:17, cols]),
            jnp.where(row < 4, l1_ref[2:3, cols] + b, jnp.where(row < 7, v2_1, NEG_INF)),
            jnp.where(row < 3, l1_ref[3:4, cols] + b,
                      jnp.where(row == 3, v2_1,
                                jnp.where(row == 4, l2_ref[2:3, cols] + l1_ref[0:8, cols],
                                          NEG_INF))),
        ]
        tops = []
        for i in range(N_TOP):
            m = cands[0]
            for cnd in cands[1:]:
                m = jnp.maximum(m, cnd)
            m = jnp.max(m, axis=0, keepdims=True)
            tops.append(m)
            if i + 1 < N_TOP:
                cands = [jnp.where(cnd == m, NEG_INF, cnd) for cnd in cands]
        tau = 0.5 * (tops[PEER_TOPK - 1] + tops[PEER_TOPK])
        z = jnp.zeros_like(tau)
        for i in range(PEER_TOPK):
            z = z + jnp.exp(tops[i] - tops[0])
        theta = tau - s1
        cnt = jnp.zeros_like(s1)
        rank = jnp.zeros_like(s2)
        half = PEER_TOPK // 2
        for j in range(half):
            v2j = l2_ref[j:j + 1, cols]
            cnt = jnp.where(v2j >= theta, j + 1.0, cnt)
            rank = jnp.where(v2j > s2, j + 1.0, rank)
        theta_top = tau - m1
        top_more = l2_ref[half:half + 1, cols] >= theta_top
        cnt = jnp.where((s1 == m1) & top_more, half + 1.0, cnt)
        rank = jnp.where((rank == half) & (s2 < theta_top), half + 1.0, rank)
        cnt_ref[hd, lt] = cnt
        e1_ref[hd, lt] = jnp.exp(s1 - m1)
        rank_ref[hd, lt] = _words(rank.astype(BF16))
        w2_ref[hd, lt] = _words((jnp.exp(s2 - m2) * (0.5 / z)).astype(BF16))

    def head_pair(i, carry):
        for k in range(HEADS_PER_ITER):
            hd = i * HEADS_PER_ITER + k
            base = pl.multiple_of(hd * 2 * PEER_HALF, 2 * PEER_HALF)
            q1 = qt_ref[pl.ds(base, PEER_HALF), :].astype(BF16)
            q2 = qt_ref[pl.ds(base + PEER_HALF, PEER_HALF), :].astype(BF16)
            s1 = _dot(k1_ref[...], q1)
            s2 = _dot(k2_ref[...], q2)
            for lt in range(tm // LANES):
                cols = slice(lt * LANES, (lt + 1) * LANES)
                route_tile(hd, lt, s1[:, cols], s2[:, cols], l1_all.at[k], l2_all.at[k])
        return carry

    lax.fori_loop(0, PEER_HEADS // HEADS_PER_ITER, head_pair, 0)


def _route(x, gffn, wqt, k1, k2):
    n = x.shape[0]
    tm = ROUTE_TM
    def key_spec(rows):
        return pl.BlockSpec((PEER_HEADS, tm // LANES, rows, LANES), lambda i: (0, i, 0, 0))

    def key_shape(rows, dtype):
        return jax.ShapeDtypeStruct((PEER_HEADS, n // LANES, rows, LANES), dtype)

    return pl.pallas_call(
        _route_kernel,
        grid=(n // tm,),
        in_specs=[pl.BlockSpec((tm, D_MODEL), lambda i: (i, 0)),
                  _const_spec(gffn.shape), _const_spec(wqt.shape),
                  _const_spec(k1.shape), _const_spec(k2.shape)],
        out_specs=[pl.BlockSpec((D_MODEL // 2, tm), lambda i: (0, i)),
                   key_spec(PEER_N_KEYS), key_spec(PEER_N_KEYS),
                   key_spec(PEER_N_KEYS // 2), key_spec(PEER_N_KEYS // 2)],
        out_shape=[jax.ShapeDtypeStruct((D_MODEL // 2, n), jnp.uint32),
                   key_shape(PEER_N_KEYS, F32), key_shape(PEER_N_KEYS, F32),
                   key_shape(PEER_N_KEYS // 2, jnp.uint32),
                   key_shape(PEER_N_KEYS // 2, jnp.uint32)],
        scratch_shapes=[
            pltpu.VMEM((PEER_HEADS * 2 * PEER_HALF, tm), F32),
            pltpu.VMEM((HEADS_PER_ITER, SORTED_ROWS, tm), F32),
            pltpu.VMEM((HEADS_PER_ITER, SORTED_ROWS, tm), F32),
        ],
        compiler_params=pltpu.CompilerParams(
            dimension_semantics=("arbitrary",), vmem_limit_bytes=VMEM_LIMIT),
        name="peer_route",
    )(x, gffn, wqt, k1, k2)


def _packed_rows(row, rows):
    tile = jnp.broadcast_to(row, (2 * SUBLANES, LANES)).astype(BF16)
    return jnp.tile(tile, (rows // tile.shape[0], 1))


def _peer_kernel(x_ref, hb_ref, cnt_ref, e1_ref, rank_ref, w2_ref, u_ref, vt_ref,
                 gfin_ref, o_ref, s_ref, a_ref, acc_ref, *, final_norm, e_tiles, n_tiles):
    g = pl.program_id(0)
    te = 2 * u_ref.shape[1]
    tm = hb_ref.shape[1]
    blocks = te // PEER_N_KEYS
    half_piece = PEER_PIECE // 2
    cur = g % 2
    prev = 1 - cur

    @pl.when(g == 0)
    def _():
        s_ref[...] = jnp.zeros_like(s_ref)
        a_ref[...] = jnp.zeros_like(a_ref)
        acc_ref[...] = jnp.zeros_like(acc_ref)

    tile2 = jnp.clip(g - 1, 0, n_tiles - 1)
    tile3 = jnp.clip(g - 2, 0, n_tiles - 1)
    i1_base = (tile2 % e_tiles) * blocks

    def gate_blocks(r):
        for sub in range(PEER_PIECE // PEER_N_KEYS):
            ib = r * (PEER_PIECE // PEER_N_KEYS) + sub
            i1 = i1_base + ib
            for lt in range(tm // LANES):
                cols = slice(lt * LANES, (lt + 1) * LANES)
                for part in range(PEER_N_KEYS // GATE_ROWS):
                    krows = slice(part * GATE_ROWS // 2, (part + 1) * GATE_ROWS // 2)
                    gate = None
                    for hd in range(PEER_HEADS):
                        cnt = _packed_rows(cnt_ref[hd, lt, pl.ds(i1, 1), :], GATE_ROWS)
                        e1 = _packed_rows(e1_ref[hd, lt, pl.ds(i1, 1), :], GATE_ROWS)
                        term = jnp.where(_halves(rank_ref[hd, lt, krows, :]) < cnt,
                                         _halves(w2_ref[hd, lt, krows, :]) * e1,
                                         jnp.zeros((), BF16))
                        gate = term if gate is None else gate + term
                    lo = ib * PEER_N_KEYS + part * GATE_ROWS
                    bwrow = pl.ds(pl.multiple_of(lo // 2, GATE_ROWS // 2), GATE_ROWS // 2)
                    s = _halves(s_ref[prev, bwrow, cols])
                    act = s * (1.0 + lax.erf(s * INV_SQRT2))
                    a_ref[prev, bwrow, cols] = _words(act * gate)

    def piece(r, carry):
        wrows = pl.ds(pl.multiple_of(r * half_piece, half_piece), half_piece)
        acc_ref[...] += _dot(_halves(vt_ref[0, r]), _halves(a_ref[cur, wrows, :]))
        gate_blocks(r)
        s_ref[cur, wrows, :] = _words(
            _dot(_halves(u_ref[0, wrows, :]), _halves(hb_ref[...])).astype(BF16))
        return carry

    lax.fori_loop(0, te // PEER_PIECE, piece, 0)

    @pl.when((g >= 2) & (tile3 % e_tiles == e_tiles - 1))
    def _():
        out = x_ref[...] + acc_ref[...].T
        if final_norm:
            out = _rms(out, gfin_ref[...])
            o_ref[...] = pltpu.einshape("(tb)d->btd", out, b=SUBLANES)
        else:
            o_ref[...] = out
        acc_ref[...] = jnp.zeros_like(acc_ref)


def _peer(x, hb_w, cnt, e1, rank_w, w2_w, u_w, vt_w, gfin, layer, final_norm):
    n = x.shape[0]
    tm, te = PEER_TM, PEER_TE
    e_tiles = PEER_N_EXPERTS // te
    n_tiles = (n // tm) * e_tiles
    last = n_tiles - 1

    def t1(g):
        return jnp.minimum(g, last)

    def t2(g):
        return jnp.clip(g - 1, 0, last)

    def t3(g):
        return jnp.clip(g - 2, 0, last)

    def key_spec(rows):
        return pl.BlockSpec((PEER_HEADS, tm // LANES, rows, LANES),
                            lambda g: (0, t2(g) // e_tiles, 0, 0),
                            pipeline_mode=pl.Buffered(1))

    out_row_spec = pl.BlockSpec((tm, D_MODEL), lambda g: (t3(g) // e_tiles, 0))
    return pl.pallas_call(
        functools.partial(_peer_kernel, final_norm=final_norm, e_tiles=e_tiles,
                          n_tiles=n_tiles),
        grid=(n_tiles + 2,),
        in_specs=[pl.BlockSpec((tm, D_MODEL), lambda g: (t3(g) // e_tiles, 0),
                               pipeline_mode=pl.Buffered(1)),
                  pl.BlockSpec((D_MODEL // 2, tm), lambda g: (0, t1(g) // e_tiles)),
                  key_spec(PEER_N_KEYS), key_spec(PEER_N_KEYS),
                  key_spec(PEER_N_KEYS // 2), key_spec(PEER_N_KEYS // 2),
                  pl.BlockSpec((1, te // 2, D_MODEL), lambda g: (layer, t1(g) % e_tiles, 0)),
                  pl.BlockSpec((1, te // PEER_PIECE, D_MODEL // 2, PEER_PIECE),
                               lambda g: (layer, t3(g) % e_tiles, 0, 0)),
                  pl.BlockSpec(gfin.shape, lambda g: (0, 0))],
        out_specs=(pl.BlockSpec((SUBLANES, tm // SUBLANES, D_MODEL),
                                lambda g: (0, t3(g) // e_tiles, 0))
                   if final_norm else out_row_spec),
        out_shape=(jax.ShapeDtypeStruct((SUBLANES, n // SUBLANES, D_MODEL), F32)
                   if final_norm else jax.ShapeDtypeStruct((n, D_MODEL), F32)),
        scratch_shapes=[
            pltpu.VMEM((2, te // 2, tm), jnp.uint32),
            pltpu.VMEM((2, te // 2, tm), jnp.uint32),
            pltpu.VMEM((D_MODEL, tm), F32),
        ],
        compiler_params=pltpu.CompilerParams(
            dimension_semantics=("arbitrary",), vmem_limit_bytes=VMEM_LIMIT),
        name="peer_dense",
    )(x, hb_w, cnt, e1, rank_w, w2_w, u_w, vt_w, gfin)


def _tables_kernel(u_ref, v_ref, uw_ref, vw_ref):
    uw_ref[0] = _words(u_ref[0].astype(BF16))
    vw_ref[0, 0] = _words(v_ref[0].T.astype(BF16))


def _expert_tables(u, v):
    nl, e, d = u.shape
    rows = PEER_PIECE
    table_spec = pl.BlockSpec((1, rows, d), lambda l, i: (l, i, 0))
    return pl.pallas_call(
        _tables_kernel,
        grid=(nl, e // rows),
        in_specs=[table_spec, table_spec],
        out_specs=[pl.BlockSpec((1, rows // 2, d), lambda l, i: (l, i, 0)),
                   pl.BlockSpec((1, 1, d // 2, rows), lambda l, i: (l, i, 0, 0))],
        out_shape=[jax.ShapeDtypeStruct((nl, e // 2, d), jnp.uint32),
                   jax.ShapeDtypeStruct((nl, e // rows, d // 2, rows), jnp.uint32)],
        compiler_params=pltpu.CompilerParams(
            dimension_semantics=("arbitrary", "arbitrary"), vmem_limit_bytes=VMEM_LIMIT),
        name="expert_tables",
    )(u, v)


def _block_diag(blocks):
    g, r, c = blocks.shape
    eye = jnp.eye(g, dtype=blocks.dtype)
    return (blocks[:, :, None, :] * eye[:, None, :, None]).reshape(g * r, g * c)


def kernel(x, norm_mix, w_in, a_re, a_im, log_dt, b_re, b_im, c_re, c_im, d_skip, w_glu, b_glu, w_pool, pool_scale, g_out_ssm, g_out_pool, w_out, norm_ffn, w_q, k1, k2, u_experts, v_experts, norm_final):
    bsz, seq, dm = x.shape
    assert (bsz, dm) == (SUBLANES, D_MODEL) and seq % (MIX_ROWS // SUBLANES) == 0
    depth = w_in.shape[0]
    n = bsz * seq
    row = lambda a: a.reshape(1, -1).astype(F32)

    abr, abi, btr, bti = _discretise(a_re, a_im, log_dt, b_re, b_im)
    xt = x
    gfin = row(norm_final)
    u_w, vt_w = _expert_tables(u_experts, v_experts)

    for i in range(depth):
        hg = SSM_GROUPS // 2
        bmat = jnp.stack([_block_diag(b[lo:lo + hg]) for b in (btr[i], bti[i])
                          for lo in (0, hg)]).astype(BF16)
        cre, cim = (jnp.stack([_block_diag(jnp.transpose(c[lo:lo + hg], (0, 2, 1)))
                               for lo in (0, hg)]).astype(BF16)
                    for c in (c_re[i], c_im[i]))
        xt = _mixer(
            xt, row(norm_mix[i]), w_in[i].astype(BF16), bmat,
            abr[i].reshape(1, STATE_W), abi[i].reshape(1, STATE_W), cre, cim,
            row(d_skip[i]), w_glu[i].astype(BF16), row(b_glu[i]),
            _block_diag(w_pool[i]).astype(BF16), row(pool_scale[i]),
            row(g_out_ssm[i]), row(g_out_pool[i]), w_out[i].astype(BF16))
        hb, cnt, e1, rank, w2 = _route(
            xt, row(norm_ffn[i]), jnp.transpose(w_q[i]).astype(BF16),
            k1[i].astype(BF16), k2[i].astype(BF16))
        xt = _peer(xt, hb, cnt, e1, rank, w2, u_w, vt_w, gfin, layer=i,
                   final_norm=(i == depth - 1))
    return xt
```
